```python
import numpy as np
import jax
import jax.numpy as jnp
from jax import lax

D_MODEL = 2048
BATCH = 4
SEQ = 2048
DEPTH = 4
DEC_BATCH = 32
DEC_SEQ = 1
PAST_LEN = 16384
PAGE_SIZE = 128

HEAD_DIM = 128
N_HEADS = D_MODEL // HEAD_DIM
N_MIXERS = 3
Q_BLOCK = 128
SWA_KV = 4
SWA_WINDOW = 128
NSA_KV = 4
NSA_CMP_LEN = 32
NSA_CMP_STRIDE = 16
NSA_CMP_HID = 2 * HEAD_DIM
NSA_SEL_LEN = 64
NSA_TOP = 16
NSA_WINDOW = 512
NSA_QBLOCK = 32
SEL_FORCE = 1e4
SB_KV = 8
D_FF = 5632
ROPE_THETA = 10000.0
LN_EPS = 1e-5
DN_ALPHA = (2 * DEPTH) ** 0.25
DN_BETA = (8 * DEPTH) ** -0.25
N_A = (DEPTH + 2) // 3
N_B = (DEPTH + 1) // 3
N_C = DEPTH // 3

kernel_name = 'hybrid_swa_nsa_stickbreak_macaron_decoder_step'


def layer_norm(x, g, b):
    xf = x.astype(jnp.float32)
    mu = jnp.mean(xf, -1, keepdims=True)
    var = jnp.mean(jnp.square(xf - mu), -1, keepdims=True)
    y = (xf - mu) * lax.rsqrt(var + LN_EPS) * g.astype(jnp.float32) + b.astype(jnp.float32)
    return y.astype(x.dtype)


def swiglu(x, w_in, w_out):
    gate, up = jnp.split(x @ w_in, 2, axis=-1)
    return (jax.nn.silu(gate) * up) @ w_out


def rope(x, pos):
    half = x.shape[-1] // 2
    inv = ROPE_THETA ** (-jnp.arange(half, dtype=jnp.float32) / half)
    ang = pos.astype(jnp.float32)[:, None] * inv[None, :]
    cos = jnp.cos(ang)[None, :, None, :]
    sin = jnp.sin(ang)[None, :, None, :]
    xf = x.astype(jnp.float32)
    x1, x2 = xf[..., :half], xf[..., half:]
    return jnp.concatenate([x1 * cos - x2 * sin, x2 * cos + x1 * sin], -1).astype(x.dtype)


def masked_softmax(logits, mask):
    logits = jnp.where(mask, logits, -jnp.inf)
    m = jnp.max(logits, -1, keepdims=True)
    m = jnp.where(jnp.isfinite(m), m, 0.0)
    e = jnp.where(mask, jnp.exp(logits - m), 0.0)
    s = jnp.sum(e, -1, keepdims=True)
    return e / jnp.where(s > 0, s, 1.0)


def gather_pages(pool, page_table):
    g = pool[page_table]
    return g.reshape(g.shape[0], g.shape[1] * g.shape[2], g.shape[3], g.shape[4])


def local_attn(q, k, v, q_pos, k_pos, window, sink):
    B, NB, QB, H, hd = q.shape
    G = k.shape[3]
    R = H // G
    qg = q.reshape(B, NB, QB, G, R, hd)
    logits = jnp.einsum('bnqgrd,bnkgd->bngrqk', qg, k).astype(jnp.float32) * (hd ** -0.5)
    rel = q_pos[:, :, None] - k_pos[:, None, :]
    mask = ((rel >= 0) & (rel <= window) & (k_pos[:, None, :] >= 0))[None, :, None, None]
    logits = jnp.where(mask, logits, -jnp.inf)
    m = jnp.max(logits, -1, keepdims=True)
    if sink is not None:
        s = sink.astype(jnp.float32).reshape(G, R)[None, None, :, :, None, None]
        m = jnp.maximum(m, s)
    p = jnp.exp(logits - m)
    den = jnp.sum(p, -1, keepdims=True)
    if sink is not None:
        den = den + jnp.exp(s - m)
    o = jnp.einsum('bngrqk,bnkgd->bnqgrd', (p / den).astype(v.dtype), v)
    return o.reshape(B, NB * QB, H, hd)


def banded_views(k, window, qb):
    B, T, G, hd = k.shape
    nb = T // qb
    nw = window // qb + 1
    kp = jnp.pad(k, ((0, 0), (window, 0), (0, 0), (0, 0))).reshape(B, nb + nw - 1, qb, G, hd)
    views = jnp.stack([kp[:, j:j + nb] for j in range(nw)], axis=2)
    return views.reshape(B, nb, nw * qb, G, hd)


def window_prompt(q, k, v, window, sink):
    B, T, H, hd = q.shape
    nb = T // Q_BLOCK
    k_pos = jnp.arange(nb)[:, None] * Q_BLOCK - window + jnp.arange(window + Q_BLOCK)[None, :]
    q_pos = jnp.arange(T).reshape(nb, Q_BLOCK)
    return local_attn(q.reshape(B, nb, Q_BLOCK, H, hd), banded_views(k, window, Q_BLOCK),
                      banded_views(v, window, Q_BLOCK), q_pos, k_pos, window, sink)


def window_sample(q, k_new, v_new, k_buf, v_buf, past_len, window, sink):
    tn = q.shape[1]
    wb = k_buf.shape[1]
    kk = jnp.concatenate([k_buf, k_new], axis=1)
    vv = jnp.concatenate([v_buf, v_new], axis=1)
    k_pos = past_len - wb + jnp.arange(wb + tn)
    q_pos = past_len + jnp.arange(tn)
    out = local_attn(q[:, None], kk[:, None], vv[:, None], q_pos[None], k_pos[None], window, sink)
    keep = min(window, past_len + tn)
    return out, kk[:, wb + tn - keep:], vv[:, wb + tn - keep:]


def stick_breaking(q, k, v, q_pos):
    B, Tq, H, hd = q.shape
    T, G = k.shape[1], k.shape[2]
    R = H // G
    qb = Q_BLOCK if Tq % Q_BLOCK == 0 else Tq
    nb = Tq // qb
    k_pos = jnp.arange(T)

    def block(args):
        qblk, qp = args
        z = jnp.einsum('bqgrd,bkgd->bgrqk', qblk.reshape(B, qb, G, R, hd), k).astype(jnp.float32) * (hd ** -0.5)
        causal = k_pos[None, :] < qp[:, None]
        log_beta = jax.nn.log_sigmoid(z)
        log_keep = jnp.where(causal, jax.nn.log_sigmoid(-z), 0.0)
        later = lax.cumsum(log_keep, axis=4, reverse=True) - log_keep
        a = jnp.where(causal, jnp.exp(log_beta + later), 0.0)
        o = jnp.einsum('bgrqk,bkgd->bqgrd', a.astype(v.dtype), v)
        return o.reshape(B, qb, H, hd)

    qs = q.reshape(B, nb, qb, H, hd).transpose(1, 0, 2, 3, 4)
    out = lax.map(block, (qs, q_pos.reshape(nb, qb)))
    return out.transpose(1, 0, 2, 3, 4).reshape(B, Tq, H, hd)


def nsa_compress(k, a, pe, w1, w2):
    B, T, G, hd = k.shape
    ratio = NSA_CMP_LEN // NSA_CMP_STRIDE
    n_seg = T // NSA_CMP_STRIDE
    n_cmp = n_seg - ratio + 1
    segs = k[:, :n_seg * NSA_CMP_STRIDE].reshape(B, n_seg, NSA_CMP_STRIDE, G, hd)
    a_r = a.reshape(ratio, NSA_CMP_STRIDE, hd)
    u = jnp.einsum('bnsgd,sd->bngd', segs, a_r[0])[:, :n_cmp]
    for r in range(1, ratio):
        u = u + jnp.einsum('bnsgd,sd->bngd', segs, a_r[r])[:, r:r + n_cmp]
    u = u + jnp.sum(a * pe, axis=0)
    return jax.nn.gelu(u @ w1) @ w2


def cmp_to_sel_matrix(n_cmp, n_sel):
    cs = np.arange(n_cmp) * NSA_CMP_STRIDE
    ss = np.arange(n_sel) * NSA_SEL_LEN
    m = (cs[:, None] < ss[None, :] + NSA_SEL_LEN) & (cs[:, None] + NSA_CMP_LEN > ss[None, :])
    return jnp.asarray(m, jnp.float32)


def nsa_global(q, kc, vc, ks, vs, q_pos, cmp_a, cmp_pe, cmp_w1, cmp_w2):
    B, Tq, H, hd = q.shape
    T, G = ks.shape[1], ks.shape[2]
    R = H // G
    scale = hd ** -0.5
    kcmp = nsa_compress(kc, cmp_a[0], cmp_pe[0], cmp_w1[0], cmp_w2[0])
    vcmp = nsa_compress(vc, cmp_a[1], cmp_pe[1], cmp_w1[1], cmp_w2[1])
    n_cmp = kcmp.shape[1]
    cmp_end = jnp.arange(n_cmp) * NSA_CMP_STRIDE + NSA_CMP_LEN - 1
    n_sel = -(-T // NSA_SEL_LEN)
    pad = n_sel * NSA_SEL_LEN - T

    def to_blocks(t):
        t = jnp.pad(t, ((0, 0), (0, pad), (0, 0), (0, 0)))
        return t.reshape(B, n_sel, NSA_SEL_LEN, G, hd).transpose(0, 3, 1, 2, 4)

    ksb, vsb = to_blocks(ks), to_blocks(vs)
    cover = cmp_to_sel_matrix(n_cmp, n_sel)
    n_top = min(NSA_TOP, n_sel)
    sel_ids = jnp.arange(n_sel)
    gather = jax.vmap(jax.vmap(lambda blk, ix: blk[ix]))

    def block(args):
        qblk, qp = args
        qb = qblk.shape[1]
        qg = qblk.reshape(B, qb, G, R, hd)
        lc = jnp.einsum('bqgrd,bngd->bgrqn', qg, kcmp).astype(jnp.float32) * scale
        pc = masked_softmax(lc, cmp_end[None, :] <= qp[:, None])
        o_cmp = jnp.einsum('bgrqn,bngd->bqgrd', pc.astype(vcmp.dtype), vcmp)
        imp = jnp.einsum('bgrqn,nj->bgqj', pc, cover)
        cur = qp // NSA_SEL_LEN
        vis = sel_ids[None, :] * NSA_SEL_LEN <= qp[:, None]
        forced = (sel_ids[None, :] == 0) | (sel_ids[None, :] == cur[:, None]) | (sel_ids[None, :] == cur[:, None] - 1)
        score = jnp.where(vis, jnp.where(forced, SEL_FORCE, imp), -1.0)
        top_val, top_idx = lax.top_k(score, n_top)
        kg = gather(ksb, top_idx)
        vg = gather(vsb, top_idx)
        kpos = top_idx[..., None] * NSA_SEL_LEN + jnp.arange(NSA_SEL_LEN)
        smask = (top_val >= 0)[..., None] & (kpos <= qp[None, None, :, None, None])
        ls = jnp.einsum('bqgrd,bgqnld->bgrqnl', qg, kg).astype(jnp.float32) * scale
        ps = masked_softmax(ls.reshape(B, G, R, qb, n_top * NSA_SEL_LEN),
                            smask.reshape(B, G, 1, qb, n_top * NSA_SEL_LEN))
        o_slc = jnp.einsum('bgrqnl,bgqnld->bqgrd',
                           ps.reshape(B, G, R, qb, n_top, NSA_SEL_LEN).astype(vg.dtype), vg)
        return o_cmp.reshape(B, qb, H, hd), o_slc.reshape(B, qb, H, hd)

    qb = NSA_QBLOCK if Tq % NSA_QBLOCK == 0 else Tq
    nb = Tq // qb
    qs = q.reshape(B, nb, qb, H, hd).transpose(1, 0, 2, 3, 4)
    o_cmp, o_slc = lax.map(block, (qs, q_pos.reshape(nb, qb)))
    return (o_cmp.transpose(1, 0, 2, 3, 4).reshape(B, Tq, H, hd),
            o_slc.transpose(1, 0, 2, 3, 4).reshape(B, Tq, H, hd))


def swa_project(h, w_in, pos):
    B, T, _ = h.shape
    qd, kd = N_HEADS * HEAD_DIM, SWA_KV * HEAD_DIM
    proj = h @ w_in
    q = rope(proj[..., :qd].reshape(B, T, N_HEADS, HEAD_DIM), pos)
    k = rope(proj[..., qd:qd + kd].reshape(B, T, SWA_KV, HEAD_DIM), pos)
    v = proj[..., qd + kd:].reshape(B, T, SWA_KV, HEAD_DIM)
    return q, k, v


def nsa_project(h, w_in, pos):
    B, T, _ = h.shape
    qd, kd = N_HEADS * HEAD_DIM, NSA_KV * HEAD_DIM
    proj = h @ w_in
    q = rope(proj[..., :qd].reshape(B, T, N_HEADS, HEAD_DIM), pos)
    kv = proj[..., qd:qd + 6 * kd].reshape(B, T, 6, NSA_KV, HEAD_DIM)
    gates = jax.nn.sigmoid(proj[..., qd + 6 * kd:].astype(jnp.float32)).reshape(B, T, N_HEADS, 3)
    kc = rope(kv[:, :, 0], pos)
    ks = rope(kv[:, :, 2], pos)
    kw = rope(kv[:, :, 4], pos)
    return q, kc, kv[:, :, 1], ks, kv[:, :, 3], kw, kv[:, :, 5], gates


def nsa_combine(gates, o_cmp, o_slc, o_win):
    g = gates.astype(o_cmp.dtype)
    return g[..., 0:1] * o_cmp + g[..., 1:2] * o_slc + g[..., 2:3] * o_win


def sb_project(h, w_in):
    B, T, _ = h.shape
    qd, kd = N_HEADS * HEAD_DIM, SB_KV * HEAD_DIM
    proj = h @ w_in
    q = proj[..., :qd].reshape(B, T, N_HEADS, HEAD_DIM)
    k = proj[..., qd:qd + kd].reshape(B, T, SB_KV, HEAD_DIM)
    v = proj[..., qd + kd:].reshape(B, T, SB_KV, HEAD_DIM)
    return q, k, v


def setup_inputs(seed: int = 0) -> dict:
    key = jax.random.key(seed)
    k = jax.random.split(key, 28)
    hd = HEAD_DIM
    hq = N_HEADS * hd
    n_pages = PAST_LEN // PAGE_SIZE
    n_pool = (DEC_BATCH * n_pages * 5) // 4
    swa_buf = min(SWA_WINDOW, PAST_LEN)
    nsa_buf = min(NSA_WINDOW, PAST_LEN)
    swa_cols = (N_HEADS + 2 * SWA_KV) * hd
    nsa_cols = (N_HEADS + 6 * NSA_KV) * hd + 3 * N_HEADS
    sb_cols = (N_HEADS + 2 * SB_KV) * hd

    def nrm(kk, shape, scale=1.0):
        return jax.random.normal(kk, shape, jnp.float32) * scale

    page_table = jax.random.permutation(k[12], n_pool)[:DEC_BATCH * n_pages].reshape(DEC_BATCH, n_pages).astype(jnp.int32)
    return {
        'x_prompt': nrm(k[0], (BATCH, SEQ, D_MODEL)),
        'x_sample': nrm(k[1], (DEC_BATCH, DEC_SEQ, D_MODEL)),
        'cache_swa_k': nrm(k[2], (N_A, DEC_BATCH, swa_buf, SWA_KV, hd)),
        'cache_swa_v': nrm(k[3], (N_A, DEC_BATCH, swa_buf, SWA_KV, hd)),
        'cache_nsa_kc': nrm(k[4], (N_B, n_pool, PAGE_SIZE, NSA_KV, hd)),
        'cache_nsa_vc': nrm(k[5], (N_B, n_pool, PAGE_SIZE, NSA_KV, hd)),
        'cache_nsa_ks': nrm(k[6], (N_B, n_pool, PAGE_SIZE, NSA_KV, hd)),
        'cache_nsa_vs': nrm(k[7], (N_B, n_pool, PAGE_SIZE, NSA_KV, hd)),
        'cache_nsa_kw': nrm(k[8], (N_B, DEC_BATCH, nsa_buf, NSA_KV, hd)),
        'cache_nsa_vw': nrm(k[9], (N_B, DEC_BATCH, nsa_buf, NSA_KV, hd)),
        'cache_sb_k': nrm(k[10], (N_C, n_pool, PAGE_SIZE, SB_KV, hd)),
        'cache_sb_v': nrm(k[11], (N_C, n_pool, PAGE_SIZE, SB_KV, hd)),
        'page_table': page_table,
        'ln_g': 1.0 + nrm(k[13], (DEPTH, 3, D_MODEL), 0.02),
        'ln_b': nrm(k[14], (DEPTH, 3, D_MODEL), 0.02),
        'ffn_w_in': nrm(k[15], (DEPTH, 2, D_MODEL, 2 * D_FF), D_MODEL ** -0.5),
        'ffn_w_out': nrm(k[16], (DEPTH, 2, D_FF, D_MODEL), D_FF ** -0.5 * DN_BETA),
        'swa_w_in': nrm(k[17], (N_A, D_MODEL, swa_cols), D_MODEL ** -0.5),
        'swa_sink': nrm(k[18], (N_A, N_HEADS), 0.5),
        'swa_w_o': nrm(k[19], (N_A, hq, D_MODEL), hq ** -0.5 * DN_BETA),
        'nsa_w_in': nrm(k[20], (N_B, D_MODEL, nsa_cols), D_MODEL ** -0.5),
        'nsa_cmp_a': nrm(k[21], (N_B, 2, NSA_CMP_LEN, hd), NSA_CMP_LEN ** -0.5),
        'nsa_cmp_pe': nrm(k[22], (N_B, 2, NSA_CMP_LEN, hd)),
        'nsa_cmp_w1': nrm(k[23], (N_B, 2, hd, NSA_CMP_HID), hd ** -0.5),
        'nsa_cmp_w2': nrm(k[24], (N_B, 2, NSA_CMP_HID, hd), NSA_CMP_HID ** -0.5),
        'nsa_w_o': nrm(k[25], (N_B, hq, D_MODEL), hq ** -0.5 * DN_BETA),
        'sb_w_in': nrm(k[26], (N_C, D_MODEL, sb_cols), D_MODEL ** -0.5),
        'sb_w_o': nrm(k[27], (N_C, hq, D_MODEL), hq ** -0.5 * DN_BETA),
    }


def reference(x_prompt, x_sample, cache_swa_k, cache_swa_v, cache_nsa_kc, cache_nsa_vc, cache_nsa_ks,
              cache_nsa_vs, cache_nsa_kw, cache_nsa_vw, cache_sb_k, cache_sb_v, page_table, ln_g, ln_b,
              ffn_w_in, ffn_w_out, swa_w_in, swa_sink, swa_w_o, nsa_w_in, nsa_cmp_a, nsa_cmp_pe,
              nsa_cmp_w1, nsa_cmp_w2, nsa_w_o, sb_w_in, sb_w_o):
    xp, xs = x_prompt, x_sample
    B, T, _ = xp.shape
    DB, TN, _ = xs.shape
    past_len = page_table.shape[1] * cache_nsa_kc.shape[2]
    pos_p = jnp.arange(T, dtype=jnp.int32)
    pos_s = past_len + jnp.arange(TN, dtype=jnp.int32)

    def with_past(pool, new):
        return jnp.concatenate([gather_pages(pool, page_table), new], axis=1)

    def ffn_step(x, l, i):
        return layer_norm(DN_ALPHA * x + 0.5 * swiglu(x, ffn_w_in[l, i], ffn_w_out[l, i]),
                          ln_g[l, 2 * i], ln_b[l, 2 * i])

    swa_kp, swa_vp, swa_ks, swa_vs = [], [], [], []
    nkc_p, nvc_p, nks_p, nvs_p, nkw_p, nvw_p = [], [], [], [], [], []
    nkc_s, nvc_s, nks_s, nvs_s, nkw_s, nvw_s = [], [], [], [], [], []
    sbk_p, sbv_p, sbk_s, sbv_s = [], [], [], []

    for l in range(DEPTH):
        kind, j = l % N_MIXERS, l // N_MIXERS
        xp = ffn_step(xp, l, 0)
        xs = ffn_step(xs, l, 0)
        if kind == 0:
            q, k, v = swa_project(xp, swa_w_in[j], pos_p)
            op = window_prompt(q, k, v, SWA_WINDOW, swa_sink[j])
            keep = min(SWA_WINDOW, T)
            swa_kp.append(k[:, T - keep:])
            swa_vp.append(v[:, T - keep:])
            q, k, v = swa_project(xs, swa_w_in[j], pos_s)
            osm, kb, vb = window_sample(q, k, v, cache_swa_k[j], cache_swa_v[j], past_len, SWA_WINDOW, swa_sink[j])
            swa_ks.append(kb)
            swa_vs.append(vb)
            w_o = swa_w_o[j]
        elif kind == 1:
            cmp = (nsa_cmp_a[j], nsa_cmp_pe[j], nsa_cmp_w1[j], nsa_cmp_w2[j])
            q, kc, vc, ks, vs, kw, vw, g = nsa_project(xp, nsa_w_in[j], pos_p)
            oc, osl = nsa_global(q, kc, vc, ks, vs, pos_p, *cmp)
            ow = window_prompt(q, kw, vw, NSA_WINDOW, None)
            op = nsa_combine(g, oc, osl, ow)
            keep = min(NSA_WINDOW, T)
            nkc_p.append(kc)
            nvc_p.append(vc)
            nks_p.append(ks)
            nvs_p.append(vs)
            nkw_p.append(kw[:, T - keep:])
            nvw_p.append(vw[:, T - keep:])
            q, kc, vc, ks, vs, kw, vw, g = nsa_project(xs, nsa_w_in[j], pos_s)
            oc, osl = nsa_global(q, with_past(cache_nsa_kc[j], kc), with_past(cache_nsa_vc[j], vc),
                                 with_past(cache_nsa_ks[j], ks), with_past(cache_nsa_vs[j], vs), pos_s, *cmp)
            ow, kwb, vwb = window_sample(q, kw, vw, cache_nsa_kw[j], cache_nsa_vw[j], past_len, NSA_WINDOW, None)
            osm = nsa_combine(g, oc, osl, ow)
            nkc_s.append(kc)
            nvc_s.append(vc)
            nks_s.append(ks)
            nvs_s.append(vs)
            nkw_s.append(kwb)
            nvw_s.append(vwb)
            w_o = nsa_w_o[j]
        else:
            q, k, v = sb_project(xp, sb_w_in[j])
            op = stick_breaking(q, k, v, pos_p)
            sbk_p.append(k)
            sbv_p.append(v)
            q, k, v = sb_project(xs, sb_w_in[j])
            osm = stick_breaking(q, with_past(cache_sb_k[j], k), with_past(cache_sb_v[j], v), pos_s)
            sbk_s.append(k)
            sbv_s.append(v)
            w_o = sb_w_o[j]
        xp = layer_norm(DN_ALPHA * xp + op.reshape(B, T, -1) @ w_o, ln_g[l, 1], ln_b[l, 1])
        xs = layer_norm(DN_ALPHA * xs + osm.reshape(DB, TN, -1) @ w_o, ln_g[l, 1], ln_b[l, 1])
        xp = ffn_step(xp, l, 1)
        xs = ffn_step(xs, l, 1)

    return (xp, xs,
            jnp.stack(swa_kp), jnp.stack(swa_vp),
            jnp.stack(nkc_p), jnp.stack(nvc_p), jnp.stack(nks_p), jnp.stack(nvs_p),
            jnp.stack(nkw_p), jnp.stack(nvw_p),
            jnp.stack(sbk_p), jnp.stack(sbv_p),
            jnp.stack(swa_ks), jnp.stack(swa_vs),
            jnp.stack(nkc_s), jnp.stack(nvc_s), jnp.stack(nks_s), jnp.stack(nvs_s),
            jnp.stack(nkw_s), jnp.stack(nvw_s),
            jnp.stack(sbk_s), jnp.stack(sbv_s))
```

```python
import functools

import numpy as np
import jax
import jax.numpy as jnp
from jax import lax
from jax.experimental import pallas as pl
from jax.experimental.pallas import tpu as pltpu

F32 = jnp.float32
BF16 = jnp.bfloat16

HEAD_DIM = 128
N_HEADS = 16
Q_BLOCK = 128
SWA_KV = 4
SWA_WINDOW = 128
NSA_KV = 4
NSA_CMP_LEN = 32
NSA_CMP_STRIDE = 16
NSA_SEL_LEN = 64
NSA_TOP = 16
NSA_WINDOW = 512
SEL_FORCE = 1e4
SB_KV = 8
ROPE_THETA = 10000.0
LN_EPS = 1e-5
DEPTH = 4
DN_ALPHA = (2 * DEPTH) ** 0.25
SCALE = HEAD_DIM ** -0.5
NEG_INF = float("-inf")

LANES = 128
SUBLANES = 8
VMEM_LIMIT = 56 * 1024 * 1024


def _cparams(sem):
    return pltpu.CompilerParams(dimension_semantics=sem, vmem_limit_bytes=VMEM_LIMIT)


def _nt(a, b):
    return lax.dot_general(a, b, (((1,), (1,)), ((), ())), preferred_element_type=F32)


def _split3(x):
    hi = x.astype(BF16)
    r = x - hi.astype(F32)
    mid = r.astype(BF16)
    lo = (r - mid.astype(F32)).astype(BF16)
    return hi, mid, lo


def _dot3(x, u):
    hi, mid, lo = _split3(x)
    return (jnp.dot(hi, u, preferred_element_type=F32)
            + jnp.dot(mid, u, preferred_element_type=F32)
            + jnp.dot(lo, u, preferred_element_type=F32))


def _dot3r(u, x):
    hi, mid, lo = _split3(x)
    return (jnp.dot(u, hi, preferred_element_type=F32)
            + jnp.dot(u, mid, preferred_element_type=F32)
            + jnp.dot(u, lo, preferred_element_type=F32))


def _stack_heads(q, r):
    return jnp.concatenate([q[:, h * HEAD_DIM:(h + 1) * HEAD_DIM] for h in range(r)], axis=0)


def _tile_rows(x, r):
    return jnp.concatenate([x] * r, axis=0) if r > 1 else x


def _glu_kernel(x_ref, wg_ref, wu_ref, o_ref):
    x = x_ref[...]
    g = jnp.dot(x, wg_ref[...], preferred_element_type=F32)
    u = jnp.dot(x, wu_ref[...], preferred_element_type=F32)
    o_ref[...] = (g * jax.nn.sigmoid(g) * u).astype(o_ref.dtype)


def glu(xb, w_in, tm, tn):
    m, k = xb.shape
    f = w_in.shape[1] // 2
    nj = f // tn
    return pl.pallas_call(
        _glu_kernel,
        grid=(m // tm, nj),
        in_specs=[pl.BlockSpec((tm, k), lambda i, j: (i, 0)),
                  pl.BlockSpec((k, tn), lambda i, j: (0, j)),
                  pl.BlockSpec((k, tn), lambda i, j: (0, j + nj))],
        out_specs=pl.BlockSpec((tm, tn), lambda i, j: (i, j)),
        out_shape=jax.ShapeDtypeStruct((m, f), BF16),
        compiler_params=_cparams(("parallel", "arbitrary")),
        name="glu",
    )(xb, w_in, w_in)


def _resln_kernel(h_ref, w_ref, x_ref, g_ref, b_ref, o32_ref, o16_ref, acc_ref, *, scale, nk):
    k = pl.program_id(1)

    @pl.when(k == 0)
    def _():
        acc_ref[...] = jnp.zeros_like(acc_ref)

    acc_ref[...] += jnp.dot(h_ref[...], w_ref[...], preferred_element_type=F32)

    @pl.when(k == nk - 1)
    def _():
        y = DN_ALPHA * x_ref[...] + scale * acc_ref[...]
        mu = jnp.mean(y, axis=-1, keepdims=True)
        d = y - mu
        var = jnp.mean(d * d, axis=-1, keepdims=True)
        out = d * lax.rsqrt(var + LN_EPS) * g_ref[...] + b_ref[...]
        o32_ref[...] = out
        o16_ref[...] = out.astype(BF16)


def res_ln(hb, w, x, g, b, scale, tm, tk):
    m, kdim = hb.shape
    d = w.shape[1]
    nk = kdim // tk
    return pl.pallas_call(
        functools.partial(_resln_kernel, scale=scale, nk=nk),
        grid=(m // tm, nk),
        in_specs=[pl.BlockSpec((tm, tk), lambda i, k: (i, k)),
                  pl.BlockSpec((tk, d), lambda i, k: (k, 0)),
                  pl.BlockSpec((tm, d), lambda i, k: (i, 0)),
                  pl.BlockSpec((1, d), lambda i, k: (0, 0)),
                  pl.BlockSpec((1, d), lambda i, k: (0, 0))],
        out_specs=[pl.BlockSpec((tm, d), lambda i, k: (i, 0)),
                   pl.BlockSpec((tm, d), lambda i, k: (i, 0))],
        out_shape=[jax.ShapeDtypeStruct((m, d), F32), jax.ShapeDtypeStruct((m, d), BF16)],
        scratch_shapes=[pltpu.VMEM((tm, d), F32)],
        compiler_params=_cparams(("parallel", "arbitrary")),
        name="res_ln",
    )(hb, w, x, g.reshape(1, d), b.reshape(1, d))


def _proj_kernel(flag_ref, x_ref, w_ref, c_ref, s_ref, o_ref, *, nh):
    j = pl.program_id(1)
    y = jnp.dot(x_ref[...], w_ref[...], preferred_element_type=F32)

    @pl.when(flag_ref[j] == 0)
    def _():
        o_ref[...] = y

    @pl.when(flag_ref[j] != 0)
    def _():
        c = c_ref[...]
        s = s_ref[...]
        for h in range(nh):
            yh = y[:, h * HEAD_DIM:(h + 1) * HEAD_DIM]
            o_ref[:, h * HEAD_DIM:(h + 1) * HEAD_DIM] = yh * c + pltpu.roll(yh, HEAD_DIM // 2, 1) * s


def proj(xb, w, rope_flags, cos_t, sin_t, tm, tn):
    m, k = xb.shape
    n = w.shape[1]
    tbl_blocks = cos_t.shape[0] // tm
    grid_spec = pltpu.PrefetchScalarGridSpec(
        num_scalar_prefetch=1,
        grid=(m // tm, n // tn),
        in_specs=[pl.BlockSpec((tm, k), lambda i, j, f: (i, 0)),
                  pl.BlockSpec((k, tn), lambda i, j, f: (0, j)),
                  pl.BlockSpec((tm, HEAD_DIM), lambda i, j, f: (i % tbl_blocks, 0)),
                  pl.BlockSpec((tm, HEAD_DIM), lambda i, j, f: (i % tbl_blocks, 0))],
        out_specs=pl.BlockSpec((tm, tn), lambda i, j, f: (i, j)),
    )
    return pl.pallas_call(
        functools.partial(_proj_kernel, nh=tn // HEAD_DIM),
        grid_spec=grid_spec,
        out_shape=jax.ShapeDtypeStruct((m, n), F32),
        compiler_params=_cparams(("parallel", "arbitrary")),
        name="proj",
    )(rope_flags, xb, w, cos_t, sin_t)


def _win_kernel(*refs, window, r, has_sink):
    if has_sink:
        sink_ref, q_ref, k_ref, v_ref, o_ref = refs
    else:
        q_ref, k_ref, v_ref, o_ref = refs
    g = pl.program_id(1)
    i = pl.program_id(2)
    kl = window + Q_BLOCK
    start = pl.multiple_of(jnp.maximum(i * Q_BLOCK - window, 0), Q_BLOCK)
    kb = k_ref[pl.ds(start, kl), :].astype(BF16)
    vb = v_ref[pl.ds(start, kl), :].astype(BF16)
    qs = _stack_heads(q_ref[...], r).astype(BF16)
    logits = _nt(qs, kb) * SCALE
    q_pos = i * Q_BLOCK + lax.broadcasted_iota(jnp.int32, (Q_BLOCK, kl), 0)
    k_pos = start + lax.broadcasted_iota(jnp.int32, (Q_BLOCK, kl), 1)
    rel = q_pos - k_pos
    mask = _tile_rows((rel >= 0) & (rel <= window), r)
    logits = jnp.where(mask, logits, NEG_INF)
    m = jnp.max(logits, axis=-1, keepdims=True)
    if has_sink:
        s = jnp.concatenate([jnp.full((Q_BLOCK, 1), sink_ref[g * r + h], F32) for h in range(r)], axis=0)
        m = jnp.maximum(m, s)
    p = jnp.exp(logits - m)
    den = jnp.sum(p, axis=-1, keepdims=True)
    if has_sink:
        den = den + jnp.exp(s - m)
    o = jnp.dot((p / den).astype(BF16), vb, preferred_element_type=F32)
    for h in range(r):
        o_ref[:, h * HEAD_DIM:(h + 1) * HEAD_DIM] = o[h * Q_BLOCK:(h + 1) * Q_BLOCK].astype(o_ref.dtype)


def window_prompt_attn(y, bsz, t, g_kv, q_col, k_col, v_col, window, sink, out_dtype):
    r = N_HEADS // g_kv
    nb = t // Q_BLOCK
    assert window + Q_BLOCK <= t
    rw = r * HEAD_DIM
    y3 = y.reshape(bsz, t, y.shape[1])
    in_specs = [pl.BlockSpec((None, Q_BLOCK, rw), lambda b, g, i: (b, i, q_col // r + g)),
                pl.BlockSpec((None, t, HEAD_DIM), lambda b, g, i: (b, 0, k_col + g)),
                pl.BlockSpec((None, t, HEAD_DIM), lambda b, g, i: (b, 0, v_col + g))]
    args = [y3, y3, y3]
    if sink is not None:
        in_specs = [pl.BlockSpec(memory_space=pltpu.SMEM)] + in_specs
        args = [sink] + args
    out = pl.pallas_call(
        functools.partial(_win_kernel, window=window, r=r, has_sink=sink is not None),
        grid=(bsz, g_kv, nb),
        in_specs=in_specs,
        out_specs=pl.BlockSpec((None, Q_BLOCK, rw), lambda b, g, i: (b, i, g)),
        out_shape=jax.ShapeDtypeStruct((bsz, t, N_HEADS * HEAD_DIM), out_dtype),
        compiler_params=_cparams(("parallel", "parallel", "arbitrary")),
        name="window_prompt",
    )(*args)
    return out.reshape(bsz * t, N_HEADS * HEAD_DIM)


def _softplus_neg_abs(z):
    return jnp.log1p(jnp.exp(-jnp.abs(z)))


def _sb_kernel(q_ref, k_ref, v_ref, o_ref, *, r):
    i = pl.program_id(2)
    rq = r * Q_BLOCK
    qs = _stack_heads(q_ref[...], r).astype(BF16)
    row = lax.broadcasted_iota(jnp.int32, (Q_BLOCK, Q_BLOCK), 0)
    col = lax.broadcasted_iota(jnp.int32, (Q_BLOCK, Q_BLOCK), 1)
    later_sel = (row > col).astype(BF16)

    def body(t, carry):
        acc, cs = carry
        j = i - t
        off = pl.multiple_of(j * Q_BLOCK, Q_BLOCK)
        kb = k_ref[pl.ds(off, Q_BLOCK), :].astype(BF16)
        vb = v_ref[pl.ds(off, Q_BLOCK), :].astype(BF16)
        z = _nt(qs, kb) * SCALE
        causal = _tile_rows((off + col) < (i * Q_BLOCK + row), r)
        sp = _softplus_neg_abs(z)
        log_beta = jnp.minimum(z, 0.0) - sp
        log_keep = jnp.where(causal, jnp.minimum(-z, 0.0) - sp, 0.0)
        later = _dot3(log_keep, later_sel) + cs
        a = jnp.where(causal, jnp.exp(log_beta + later), 0.0)
        acc = acc + jnp.dot(a.astype(BF16), vb, preferred_element_type=F32)
        cs = cs + jnp.sum(log_keep, axis=-1, keepdims=True)
        return acc, cs

    acc, _ = lax.fori_loop(0, i + 1, body, (jnp.zeros((rq, HEAD_DIM), F32), jnp.zeros((rq, 1), F32)))
    for h in range(r):
        o_ref[:, h * HEAD_DIM:(h + 1) * HEAD_DIM] = acc[h * Q_BLOCK:(h + 1) * Q_BLOCK].astype(o_ref.dtype)


def sb_prompt_attn(y, bsz, t, out_dtype):
    r = N_HEADS // SB_KV
    nb = t // Q_BLOCK
    rw = r * HEAD_DIM
    y3 = y.reshape(bsz, t, y.shape[1])
    out = pl.pallas_call(
        functools.partial(_sb_kernel, r=r),
        grid=(bsz, SB_KV, nb),
        in_specs=[pl.BlockSpec((None, Q_BLOCK, rw), lambda b, g, i: (b, i, g)),
                  pl.BlockSpec((None, t, HEAD_DIM), lambda b, g, i: (b, 0, N_HEADS + g)),
                  pl.BlockSpec((None, t, HEAD_DIM), lambda b, g, i: (b, 0, N_HEADS + SB_KV + g))],
        out_specs=pl.BlockSpec((None, Q_BLOCK, rw), lambda b, g, i: (b, i, g)),
        out_shape=jax.ShapeDtypeStruct((bsz, t, N_HEADS * HEAD_DIM), out_dtype),
        compiler_params=_cparams(("parallel", "parallel", "arbitrary")),
        name="sb_prompt",
    )(y3, y3, y3)
    return out.reshape(bsz * t, N_HEADS * HEAD_DIM)


SEG_PER_PAGE = 128 // NSA_CMP_STRIDE


def _pool_kernel(*refs, pp, g_kv, prefetch):
    refs = refs[prefetch:]
    x_refs = refs[:pp]
    a0_ref, a1_ref, u0_ref, u1_ref = refs[pp:]
    for s_i, x_ref in enumerate(x_refs):
        rows = slice(s_i * SEG_PER_PAGE, (s_i + 1) * SEG_PER_PAGE)
        for g in range(g_kv):
            acc0 = jnp.zeros((SEG_PER_PAGE, HEAD_DIM), F32)
            acc1 = jnp.zeros((SEG_PER_PAGE, HEAD_DIM), F32)
            for s in range(NSA_CMP_STRIDE):
                xs = x_ref[pl.ds(s * g_kv + g, SEG_PER_PAGE, stride=NSA_CMP_STRIDE * g_kv), :]
                acc0 = acc0 + xs * a0_ref[s]
                acc1 = acc1 + xs * a1_ref[s]
            u0_ref[rows, g * HEAD_DIM:(g + 1) * HEAD_DIM] = acc0
            u1_ref[rows, g * HEAD_DIM:(g + 1) * HEAD_DIM] = acc1


def _pool_weights(a):
    a_r = a.reshape(NSA_CMP_LEN // NSA_CMP_STRIDE, NSA_CMP_STRIDE, 1, HEAD_DIM)
    a_r = jnp.broadcast_to(a_r, (2, NSA_CMP_STRIDE, SEG_PER_PAGE, HEAD_DIM))
    return a_r[0], a_r[1]


def pool_prompt(x, a, pp):
    bsz, t, g_kv, _ = x.shape
    gw = g_kv * HEAD_DIM
    n_pages = t // 128
    xr = x.reshape(bsz, t * g_kv, HEAD_DIM)
    a0, a1 = _pool_weights(a)
    in_specs = [pl.BlockSpec((None, 128 * g_kv, HEAD_DIM), lambda b, p, s=s: (b, p * pp + s, 0))
                for s in range(pp)]
    in_specs += [pl.BlockSpec(a0.shape, lambda b, p: (0, 0, 0))] * 2
    seg = pp * SEG_PER_PAGE
    return pl.pallas_call(
        functools.partial(_pool_kernel, pp=pp, g_kv=g_kv, prefetch=0),
        grid=(bsz, n_pages // pp),
        in_specs=in_specs,
        out_specs=[pl.BlockSpec((None, seg, gw), lambda b, p: (b, p, 0))] * 2,
        out_shape=[jax.ShapeDtypeStruct((bsz, t // NSA_CMP_STRIDE, gw), F32)] * 2,
        compiler_params=_cparams(("parallel", "arbitrary")),
        name="pool_prompt",
    )(*([xr] * pp), a0, a1)


def pool_paged(pool, page_table, a, pp):
    db, n_pages = page_table.shape
    n_pool, page, g_kv, _ = pool.shape
    assert page == 128
    gw = g_kv * HEAD_DIM
    pr = pool.reshape(n_pool, page * g_kv, HEAD_DIM)
    a0, a1 = _pool_weights(a)
    in_specs = [pl.BlockSpec((None, page * g_kv, HEAD_DIM), lambda b, p, pt, s=s: (pt[b, p * pp + s], 0, 0))
                for s in range(pp)]
    in_specs += [pl.BlockSpec(a0.shape, lambda b, p, pt: (0, 0, 0))] * 2
    seg = pp * SEG_PER_PAGE
    grid_spec = pltpu.PrefetchScalarGridSpec(
        num_scalar_prefetch=1,
        grid=(db, n_pages // pp),
        in_specs=in_specs,
        out_specs=[pl.BlockSpec((None, seg, gw), lambda b, p, pt: (b, p, 0))] * 2,
    )
    return pl.pallas_call(
        functools.partial(_pool_kernel, pp=pp, g_kv=g_kv, prefetch=1),
        grid_spec=grid_spec,
        out_shape=[jax.ShapeDtypeStruct((db, n_pages * SEG_PER_PAGE, gw), F32)] * 2,
        compiler_params=_cparams(("parallel", "arbitrary")),
        name="pool_paged",
    )(page_table, *([pr] * pp), a0, a1)


def _cmp_mlp_kernel(u0_ref, u1_ref, a_ref, pe_ref, w1_ref, w2_ref, o_ref):
    n_seg = u0_ref.shape[0]
    c = jnp.sum(a_ref[...] * pe_ref[...], axis=0, keepdims=True)
    u = u0_ref[...] + pltpu.roll(u1_ref[...], n_seg - 1, 0)
    u = u + c
    h = jnp.dot(u.astype(BF16), w1_ref[...].astype(BF16), preferred_element_type=F32)
    h = jax.nn.gelu(h)
    o_ref[...] = jnp.dot(h.astype(BF16), w2_ref[...].astype(BF16), preferred_element_type=F32)


def cmp_mlp(u0, u1, a, pe, w1, w2):
    bx, n_seg, gw = u0.shape
    g_kv = gw // HEAD_DIM
    full = lambda arr: pl.BlockSpec(arr.shape, lambda b, g: (0,) * arr.ndim)
    return pl.pallas_call(
        _cmp_mlp_kernel,
        grid=(bx, g_kv),
        in_specs=[pl.BlockSpec((None, n_seg, HEAD_DIM), lambda b, g: (b, 0, g)),
                  pl.BlockSpec((None, n_seg, HEAD_DIM), lambda b, g: (b, 0, g)),
                  full(a), full(pe), full(w1), full(w2)],
        out_specs=pl.BlockSpec((None, n_seg, HEAD_DIM), lambda b, g: (b, 0, g)),
        out_shape=jax.ShapeDtypeStruct((bx, n_seg, gw), F32),
        compiler_params=_cparams(("parallel", "arbitrary")),
        name="cmp_mlp",
    )(u0, u1, a, pe, w1, w2)


def _cover_matrix(n_rows, n_cmp, n_cols, n_sel):
    cs = np.arange(n_rows) * NSA_CMP_STRIDE
    ss = np.arange(n_cols) * NSA_SEL_LEN
    m = (cs[:, None] < ss[None, :] + NSA_SEL_LEN) & (cs[:, None] + NSA_CMP_LEN > ss[None, :])
    m = m & (np.arange(n_rows)[:, None] < n_cmp) & (np.arange(n_cols)[None, :] < n_sel)
    return jnp.asarray(m, BF16)


def _masked_softmax(logits, mask):
    logits = jnp.where(mask, logits, NEG_INF)
    m = jnp.max(logits, axis=-1, keepdims=True)
    m = jnp.where(m == NEG_INF, 0.0, m)
    e = jnp.exp(logits - m)
    s = jnp.sum(e, axis=-1, keepdims=True)
    return e / jnp.where(s > 0, s, 1.0)


def _nsa_prompt_cmp_kernel(q_ref, kc_ref, vc_ref, cover_ref, ocmp_ref, sel_ref, *, r, n_cmp, n_sel, n_top):
    i = pl.program_id(2)
    qs = _stack_heads(q_ref[...], r).astype(BF16)
    nc = kc_ref.shape[0]

    lc = _nt(qs, kc_ref[...].astype(BF16)) * SCALE
    qp_c = i * Q_BLOCK + lax.broadcasted_iota(jnp.int32, (Q_BLOCK, nc), 0)
    n_c = lax.broadcasted_iota(jnp.int32, (Q_BLOCK, nc), 1)
    cmask = _tile_rows((n_c * NSA_CMP_STRIDE + NSA_CMP_LEN - 1 <= qp_c) & (n_c < n_cmp), r)
    pc = _masked_softmax(lc, cmask)
    ocmp = jnp.dot(pc.astype(BF16), vc_ref[...].astype(BF16), preferred_element_type=F32)
    for h in range(r):
        ocmp_ref[:, h * HEAD_DIM:(h + 1) * HEAD_DIM] = ocmp[h * Q_BLOCK:(h + 1) * Q_BLOCK]

    p_sum = pc[0:Q_BLOCK]
    for h in range(1, r):
        p_sum = p_sum + pc[h * Q_BLOCK:(h + 1) * Q_BLOCK]
    imp = _dot3(p_sum, cover_ref[...])
    qp = i * Q_BLOCK + lax.broadcasted_iota(jnp.int32, (Q_BLOCK, LANES), 0)
    jj = lax.broadcasted_iota(jnp.int32, (Q_BLOCK, LANES), 1)
    cur = qp // NSA_SEL_LEN
    vis = jj * NSA_SEL_LEN <= qp
    forced = (jj == 0) | (jj == cur) | (jj == cur - 1)
    score = jnp.where(vis, jnp.where(forced, SEL_FORCE, imp), -1.0)
    score = jnp.where(jj < n_sel, score, -2.0)
    rank = jnp.zeros((Q_BLOCK, LANES), F32)
    for t in range(n_sel):
        st = score[:, t:t + 1]
        beats = (st > score) | ((st == score) & (jj > t))
        rank = rank + jnp.where(beats, 1.0, 0.0)
    sel_ref[...] = jnp.where((rank < n_top) & (score >= 0.0), 1.0, 0.0).astype(sel_ref.dtype)


def _nsa_prompt_slc_kernel(q_ref, sel_ref, ks_ref, vs_ref, oslc_ref, *, r):
    i = pl.program_id(2)
    rq = r * Q_BLOCK
    qs = _stack_heads(q_ref[...], r).astype(BF16)
    sel = sel_ref[...]
    row = lax.broadcasted_iota(jnp.int32, (Q_BLOCK, Q_BLOCK), 0)
    col = lax.broadcasted_iota(jnp.int32, (Q_BLOCK, Q_BLOCK), 1)

    def body(kb, carry):
        m, l, acc = carry
        off = pl.multiple_of(kb * Q_BLOCK, Q_BLOCK)
        k = ks_ref[pl.ds(off, Q_BLOCK), :].astype(BF16)
        v = vs_ref[pl.ds(off, Q_BLOCK), :].astype(BF16)
        s = _nt(qs, k) * SCALE
        expand = jnp.where(row == (off + col) // NSA_SEL_LEN, 1.0, 0.0).astype(BF16)
        picked = jnp.dot(sel, expand, preferred_element_type=F32) > 0.5
        mask = _tile_rows(picked & ((off + col) <= (i * Q_BLOCK + row)), r)
        s = jnp.where(mask, s, NEG_INF)
        m_new = jnp.maximum(m, jnp.max(s, axis=-1, keepdims=True))
        m_safe = jnp.where(m_new == NEG_INF, 0.0, m_new)
        p = jnp.exp(s - m_safe)
        alpha = jnp.exp(m - m_safe)
        l = alpha * l + jnp.sum(p, axis=-1, keepdims=True)
        acc = alpha * acc + jnp.dot(p.astype(BF16), v, preferred_element_type=F32)
        return m_new, l, acc

    init = (jnp.full((rq, 1), NEG_INF, F32), jnp.zeros((rq, 1), F32), jnp.zeros((rq, HEAD_DIM), F32))
    _, l, acc = lax.fori_loop(0, i + 1, body, init)
    oslc = acc / jnp.where(l > 0, l, 1.0)
    for h in range(r):
        oslc_ref[:, h * HEAD_DIM:(h + 1) * HEAD_DIM] = oslc[h * Q_BLOCK:(h + 1) * Q_BLOCK]


def nsa_prompt_global(y, kcmp, vcmp, bsz, t):
    r = N_HEADS // NSA_KV
    nb = t // Q_BLOCK
    rw = r * HEAD_DIM
    n_seg = t // NSA_CMP_STRIDE
    n_cmp = n_seg - NSA_CMP_LEN // NSA_CMP_STRIDE + 1
    n_sel = -(-t // NSA_SEL_LEN)
    assert n_sel <= LANES and t % NSA_SEL_LEN == 0
    n_top = min(NSA_TOP, n_sel)
    y3 = y.reshape(bsz, t, y.shape[1])
    cover = _cover_matrix(n_seg, n_cmp, LANES, n_sel)
    ks_col = N_HEADS + 2 * NSA_KV
    vs_col = N_HEADS + 3 * NSA_KV
    o_shape = jax.ShapeDtypeStruct((bsz, t, N_HEADS * HEAD_DIM), F32)
    q_spec = pl.BlockSpec((None, Q_BLOCK, rw), lambda b, g, i: (b, i, g))
    sel_spec = pl.BlockSpec((None, None, Q_BLOCK, LANES), lambda b, g, i: (b, g, i, 0))
    ocmp, sel = pl.pallas_call(
        functools.partial(_nsa_prompt_cmp_kernel, r=r, n_cmp=n_cmp, n_sel=n_sel, n_top=n_top),
        grid=(bsz, NSA_KV, nb),
        in_specs=[q_spec,
                  pl.BlockSpec((None, n_seg, HEAD_DIM), lambda b, g, i: (b, 0, g)),
                  pl.BlockSpec((None, n_seg, HEAD_DIM), lambda b, g, i: (b, 0, g)),
                  pl.BlockSpec(cover.shape, lambda b, g, i: (0, 0))],
        out_specs=[q_spec, sel_spec],
        out_shape=[o_shape, jax.ShapeDtypeStruct((bsz, NSA_KV, t, LANES), BF16)],
        compiler_params=_cparams(("parallel", "parallel", "arbitrary")),
        name="nsa_prompt_cmp",
    )(y3, kcmp, vcmp, cover)
    oslc = pl.pallas_call(
        functools.partial(_nsa_prompt_slc_kernel, r=r),
        grid=(bsz, NSA_KV, nb),
        in_specs=[q_spec, sel_spec,
                  pl.BlockSpec((None, t, HEAD_DIM), lambda b, g, i: (b, 0, ks_col + g)),
                  pl.BlockSpec((None, t, HEAD_DIM), lambda b, g, i: (b, 0, vs_col + g))],
        out_specs=q_spec,
        out_shape=o_shape,
        compiler_params=_cparams(("parallel", "parallel", "arbitrary")),
        name="nsa_prompt_slc",
    )(y3, sel, y3, y3)
    return ocmp.reshape(bsz * t, -1), oslc.reshape(bsz * t, -1)


def _combine_kernel(gl_ref, oc_ref, os_ref, ow_ref, o_ref):
    gates = jax.nn.sigmoid(gl_ref[...])
    for h in range(N_HEADS):
        sl = slice(h * HEAD_DIM, (h + 1) * HEAD_DIM)
        acc = gates[:, 3 * h:3 * h + 1] * oc_ref[:, sl]
        acc = acc + gates[:, 3 * h + 1:3 * h + 2] * os_ref[:, sl]
        acc = acc + gates[:, 3 * h + 2:3 * h + 3] * ow_ref[:, sl]
        o_ref[:, sl] = acc.astype(o_ref.dtype)


def nsa_combine(gate_logits, o_cmp, o_slc, o_win, tm):
    m, d = o_cmp.shape
    row = lambda w: pl.BlockSpec((tm, w), lambda i: (i, 0))
    return pl.pallas_call(
        _combine_kernel,
        grid=(m // tm,),
        in_specs=[row(gate_logits.shape[1]), row(d), row(d), row(d)],
        out_specs=row(d),
        out_shape=jax.ShapeDtypeStruct((m, d), BF16),
        compiler_params=_cparams(("parallel",)),
        name="nsa_combine",
    )(gate_logits, o_cmp, o_slc, o_win)


def _block_diag_q(q, g_kv):
    db = q.shape[0]
    r = N_HEADS // g_kv
    qh = q.reshape(db, N_HEADS, 1, HEAD_DIM)
    onehot = (jnp.arange(N_HEADS)[:, None] // r == jnp.arange(g_kv)[None, :]).astype(q.dtype)
    return (qh * onehot[None, :, :, None]).reshape(db, N_HEADS, g_kv * HEAD_DIM).astype(BF16)


def _diag_extract(o_all, g_kv):
    r = N_HEADS // g_kv
    hrow = lax.broadcasted_iota(jnp.int32, (N_HEADS, HEAD_DIM), 0)
    out = jnp.zeros((N_HEADS, HEAD_DIM), F32)
    for g in range(g_kv):
        out = out + jnp.where(hrow // r == g, o_all[:, g * HEAD_DIM:(g + 1) * HEAD_DIM], 0.0)
    return out


def _bf16_round(x):
    return x.astype(BF16).astype(F32)


def _win_sample_kernel(*refs, window, past_len, g_kv, has_sink):
    if has_sink:
        sink_ref, q_ref, kb_ref, vb_ref, kn_ref, vn_ref, o_ref = refs
    else:
        q_ref, kb_ref, vb_ref, kn_ref, vn_ref, o_ref = refs
    wb = kb_ref.shape[0]
    q = q_ref[...]
    z = _nt(q, kb_ref[...].astype(BF16)) * SCALE
    zn = jnp.sum(q.astype(F32) * _bf16_round(kn_ref[...]), axis=-1, keepdims=True) * SCALE
    k_pos = past_len - wb + lax.broadcasted_iota(jnp.int32, (N_HEADS, wb), 1)
    rel = past_len - k_pos
    mask = (rel >= 0) & (rel <= window) & (k_pos >= 0)
    z = jnp.where(mask, z, NEG_INF)
    m = jnp.maximum(jnp.max(z, axis=-1, keepdims=True), zn)
    if has_sink:
        m = jnp.maximum(m, sink_ref[...])
    p = jnp.exp(z - m)
    pn = jnp.exp(zn - m)
    den = jnp.sum(p, axis=-1, keepdims=True) + pn
    if has_sink:
        den = den + jnp.exp(sink_ref[...] - m)
    o_all = jnp.dot((p / den).astype(BF16), vb_ref[...].astype(BF16), preferred_element_type=F32)
    o_all = o_all + _bf16_round(pn / den) * _bf16_round(vn_ref[...])
    o_ref[...] = _diag_extract(o_all, g_kv).astype(o_ref.dtype)


def window_sample_attn(q, k_new, v_new, k_buf, v_buf, past_len, window, sink, out_dtype):
    db, wb, gw = k_buf.shape
    g_kv = gw // HEAD_DIM
    qbd = _block_diag_q(q, g_kv)
    in_specs = [pl.BlockSpec((None, N_HEADS, gw), lambda b: (b, 0, 0)),
                pl.BlockSpec((None, wb, gw), lambda b: (b, 0, 0)),
                pl.BlockSpec((None, wb, gw), lambda b: (b, 0, 0)),
                pl.BlockSpec((None, 1, gw), lambda b: (b, 0, 0)),
                pl.BlockSpec((None, 1, gw), lambda b: (b, 0, 0))]
    args = [qbd, k_buf, v_buf, k_new.reshape(db, 1, gw), v_new.reshape(db, 1, gw)]
    if sink is not None:
        in_specs = [pl.BlockSpec((N_HEADS, 1), lambda b: (0, 0))] + in_specs
        args = [sink.reshape(N_HEADS, 1)] + args
    out = pl.pallas_call(
        functools.partial(_win_sample_kernel, window=window, past_len=past_len, g_kv=g_kv,
                          has_sink=sink is not None),
        grid=(db,),
        in_specs=in_specs,
        out_specs=pl.BlockSpec((None, N_HEADS, HEAD_DIM), lambda b: (b, 0, 0)),
        out_shape=jax.ShapeDtypeStruct((db, N_HEADS, HEAD_DIM), out_dtype),
        compiler_params=_cparams(("parallel",)),
        name="window_sample",
    )(*args)
    return out.reshape(db, N_HEADS * HEAD_DIM)


def _sb_sample_kernel(*refs, pp, n_pages, q_pos):
    refs = refs[1:]
    q_ref = refs[0]
    k_refs = refs[1:1 + pp]
    v_refs = refs[1 + pp:1 + 2 * pp]
    o_ref, cs_ref, acc_ref = refs[1 + 2 * pp:]
    p = pl.program_id(1)

    @pl.when(p == 0)
    def _():
        cs_ref[...] = jnp.zeros_like(cs_ref)
        acc_ref[...] = jnp.zeros_like(acc_ref)

    q = q_ref[...]
    row = lax.broadcasted_iota(jnp.int32, (LANES, LANES), 0)
    col = lax.broadcasted_iota(jnp.int32, (LANES, LANES), 1)
    later_sel = (row > col).astype(BF16)
    lane = lax.broadcasted_iota(jnp.int32, (N_HEADS, LANES), 1)
    cs = cs_ref[...]
    acc = acc_ref[...]
    for s in range(pp):
        page = n_pages - 1 - (p * pp + s)
        z = _nt(q, k_refs[s][...].astype(BF16)) * SCALE
        causal = (page * 128 + lane) < q_pos
        sp = _softplus_neg_abs(z)
        log_beta = jnp.minimum(z, 0.0) - sp
        log_keep = jnp.where(causal, jnp.minimum(-z, 0.0) - sp, 0.0)
        later = _dot3(log_keep, later_sel) + cs
        a = jnp.where(causal, jnp.exp(log_beta + later), 0.0)
        acc = acc + jnp.dot(a.astype(BF16), v_refs[s][...].astype(BF16), preferred_element_type=F32)
        cs = cs + jnp.sum(log_keep, axis=-1, keepdims=True)
    cs_ref[...] = cs
    acc_ref[...] = acc

    @pl.when(p == pl.num_programs(1) - 1)
    def _():
        o_ref[...] = _diag_extract(acc, SB_KV).astype(o_ref.dtype)


def sb_sample_attn(q, pool_k, pool_v, page_table, pp, out_dtype):
    db, n_pages = page_table.shape
    gw = pool_k.shape[-1]
    q_pos = n_pages * 128
    qbd = _block_diag_q(q, SB_KV)
    page_spec = lambda s: pl.BlockSpec(
        (None, 128, gw), lambda b, p, pt, s=s: (pt[b, n_pages - 1 - (p * pp + s)], 0, 0))
    grid_spec = pltpu.PrefetchScalarGridSpec(
        num_scalar_prefetch=1,
        grid=(db, n_pages // pp),
        in_specs=([pl.BlockSpec((None, N_HEADS, gw), lambda b, p, pt: (b, 0, 0))]
                  + [page_spec(s) for s in range(pp)] * 2),
        out_specs=pl.BlockSpec((None, N_HEADS, HEAD_DIM), lambda b, p, pt: (b, 0, 0)),
        scratch_shapes=[pltpu.VMEM((N_HEADS, 1), F32), pltpu.VMEM((N_HEADS, gw), F32)],
    )
    out = pl.pallas_call(
        functools.partial(_sb_sample_kernel, pp=pp, n_pages=n_pages, q_pos=q_pos),
        grid_spec=grid_spec,
        out_shape=jax.ShapeDtypeStruct((db, N_HEADS, HEAD_DIM), out_dtype),
        compiler_params=_cparams(("parallel", "arbitrary")),
        name="sb_sample",
    )(page_table, qbd, *([pool_k] * pp), *([pool_v] * pp))
    return out.reshape(db, N_HEADS * HEAD_DIM)


def _nsa_sample_cmp_kernel(q_ref, kc_ref, vc_ref, cover_ref, ocmp_ref, idx_ref, val_ref,
                           *, n_cmp, n_sel, n_top, q_pos):
    q = q_ref[...]
    nc = kc_ref.shape[0]
    ns = cover_ref.shape[1]
    r = N_HEADS // NSA_KV
    lc = _nt(q, kc_ref[...].astype(BF16)) * SCALE
    n_c = lax.broadcasted_iota(jnp.int32, (N_HEADS, nc), 1)
    cmask = (n_c * NSA_CMP_STRIDE + NSA_CMP_LEN - 1 <= q_pos) & (n_c < n_cmp)
    pc = _masked_softmax(lc, cmask)
    o_all = jnp.dot(pc.astype(BF16), vc_ref[...].astype(BF16), preferred_element_type=F32)
    ocmp_ref[...] = _diag_extract(o_all, NSA_KV)

    grow = lax.broadcasted_iota(jnp.int32, (SUBLANES, N_HEADS), 0)
    hcol = lax.broadcasted_iota(jnp.int32, (SUBLANES, N_HEADS), 1)
    group_sel = (hcol // r == grow).astype(BF16)
    p_sum = _dot3r(group_sel, pc)
    imp = _dot3(p_sum, cover_ref[...])
    jj = lax.broadcasted_iota(jnp.int32, (SUBLANES, ns), 1)
    cur = q_pos // NSA_SEL_LEN
    vis = jj * NSA_SEL_LEN <= q_pos
    forced = (jj == 0) | (jj == cur) | (jj == cur - 1)
    score = jnp.where(vis, jnp.where(forced, SEL_FORCE, imp), -1.0)
    score = jnp.where(jj < n_sel, score, -2.0)
    jf = jj.astype(F32)
    tl = lax.broadcasted_iota(jnp.int32, (SUBLANES, LANES), 1)
    idx_out = jnp.zeros((SUBLANES, LANES), F32)
    val_out = jnp.full((SUBLANES, LANES), -1.0, F32)
    for t in range(n_top):
        mx = jnp.max(score, axis=-1, keepdims=True)
        first = jnp.min(jnp.where(score == mx, jf, float(ns)), axis=-1, keepdims=True)
        idx_out = jnp.where(tl == t, first, idx_out)
        val_out = jnp.where(tl == t, mx, val_out)
        score = jnp.where(jf == first, NEG_INF, score)
    idx_ref[...] = idx_out.astype(jnp.int32)
    val_ref[...] = val_out


def nsa_sample_cmp(q, kcmp, vcmp, n_cmp, n_sel, q_pos):
    db, nc, gw = kcmp.shape
    ns = -(-n_sel // LANES) * LANES
    n_top = min(NSA_TOP, n_sel)
    cover = _cover_matrix(nc, n_cmp, ns, n_sel)
    qbd = _block_diag_q(q, NSA_KV)
    return pl.pallas_call(
        functools.partial(_nsa_sample_cmp_kernel, n_cmp=n_cmp, n_sel=n_sel, n_top=n_top, q_pos=q_pos),
        grid=(db,),
        in_specs=[pl.BlockSpec((None, N_HEADS, gw), lambda b: (b, 0, 0)),
                  pl.BlockSpec((None, nc, gw), lambda b: (b, 0, 0)),
                  pl.BlockSpec((None, nc, gw), lambda b: (b, 0, 0)),
                  pl.BlockSpec(cover.shape, lambda b: (0, 0))],
        out_specs=[pl.BlockSpec((None, N_HEADS, HEAD_DIM), lambda b: (b, 0, 0)),
                   pl.BlockSpec((None, SUBLANES, LANES), lambda b: (b, 0, 0)),
                   pl.BlockSpec((None, SUBLANES, LANES), lambda b: (b, 0, 0))],
        out_shape=[jax.ShapeDtypeStruct((db, N_HEADS, HEAD_DIM), F32),
                   jax.ShapeDtypeStruct((db, SUBLANES, LANES), jnp.int32),
                   jax.ShapeDtypeStruct((db, SUBLANES, LANES), F32)],
        compiler_params=_cparams(("parallel",)),
        name="nsa_sample_cmp",
    )(qbd, kcmp, vcmp, cover)


def _nsa_sample_slc_kernel(*refs, n_top, n_sel, q_pos):
    idx_ref, ok_ref, pt_ref = refs[:3]
    refs = refs[3:]
    q_ref = refs[0]
    k_refs = refs[1:1 + n_top]
    v_refs = refs[1 + n_top:1 + 2 * n_top]
    kn_ref, vn_ref, o_ref = refs[1 + 2 * n_top:]
    b = pl.program_id(0)
    g = pl.program_id(1)
    r = N_HEADS // NSA_KV
    base = (b * NSA_KV + g) * n_top

    @pl.when(g == 0)
    def _():
        o_ref[...] = jnp.zeros_like(o_ref)

    q = q_ref[...]
    lane = lax.broadcasted_iota(jnp.int32, (N_HEADS, NSA_SEL_LEN), 1)
    zs = []
    new_ok = jnp.int32(0)
    for t in range(n_top):
        j = idx_ref[base + t]
        ok = ok_ref[base + t]
        in_pool = j < n_sel - 1
        z = _nt(q, k_refs[t][...].astype(BF16)) * SCALE
        last_pos = jnp.where((ok > 0) & in_pool, q_pos, -1)
        zs.append(jnp.where((j * NSA_SEL_LEN + lane) <= last_pos, z, NEG_INF))
        new_ok = new_ok | jnp.where((ok > 0) & jnp.logical_not(in_pool), 1, 0)
    new_vis = (new_ok > 0) & ((n_sel - 1) * NSA_SEL_LEN <= q_pos)
    zn = jnp.sum(q.astype(F32) * _bf16_round(kn_ref[...]), axis=-1, keepdims=True) * SCALE
    zn = zn + jnp.where(new_vis, 0.0, NEG_INF)
    m = zn
    for z in zs:
        m = jnp.maximum(m, jnp.max(z, axis=-1, keepdims=True))
    m = jnp.where(m == NEG_INF, 0.0, m)
    pn = jnp.exp(zn - m)
    den = pn
    ps = []
    for z in zs:
        e = jnp.exp(z - m)
        ps.append(e)
        den = den + jnp.sum(e, axis=-1, keepdims=True)
    den = jnp.where(den > 0, den, 1.0)
    o = _bf16_round(pn / den) * _bf16_round(vn_ref[...])
    for t in range(n_top):
        o = o + jnp.dot((ps[t] / den).astype(BF16), v_refs[t][...].astype(BF16),
                        preferred_element_type=F32)
    hrow = lax.broadcasted_iota(jnp.int32, (N_HEADS, HEAD_DIM), 0)
    o_ref[...] = jnp.where(hrow // r == g, o, o_ref[...])


def nsa_sample_slc(q, top_idx, top_val, pool_k, pool_v, page_table, k_new, v_new, n_sel, q_pos):
    db, n_pages = page_table.shape
    n_top = min(NSA_TOP, n_sel)
    n_pool = pool_k.shape[0]
    half = 128 // NSA_SEL_LEN
    pk = pool_k.reshape(n_pool * half, NSA_SEL_LEN, NSA_KV * HEAD_DIM)
    pv = pool_v.reshape(n_pool * half, NSA_SEL_LEN, NSA_KV * HEAD_DIM)
    idx = top_idx[:, :NSA_KV, :n_top].reshape(-1)
    ok = (top_val[:, :NSA_KV, :n_top] >= 0.0).astype(jnp.int32).reshape(-1)

    def blk_spec(t):
        def index_map(b, g, idx_ref, ok_ref, pt_ref):
            j = jnp.minimum(idx_ref[(b * NSA_KV + g) * n_top + t], n_sel - 2)
            return (pt_ref[b, j // half] * half + j % half, 0, g)
        return pl.BlockSpec((None, NSA_SEL_LEN, HEAD_DIM), index_map)

    row_spec = pl.BlockSpec((None, 1, HEAD_DIM), lambda b, g, *_: (b * NSA_KV + g, 0, 0))
    grid_spec = pltpu.PrefetchScalarGridSpec(
        num_scalar_prefetch=3,
        grid=(db, NSA_KV),
        in_specs=([pl.BlockSpec((None, N_HEADS, HEAD_DIM), lambda b, g, *_: (b, 0, 0))]
                  + [blk_spec(t) for t in range(n_top)] * 2 + [row_spec, row_spec]),
        out_specs=pl.BlockSpec((None, N_HEADS, HEAD_DIM), lambda b, g, *_: (b, 0, 0)),
    )
    return pl.pallas_call(
        functools.partial(_nsa_sample_slc_kernel, n_top=n_top, n_sel=n_sel, q_pos=q_pos),
        grid_spec=grid_spec,
        out_shape=jax.ShapeDtypeStruct((db, N_HEADS, HEAD_DIM), F32),
        compiler_params=_cparams(("parallel", "arbitrary")),
        name="nsa_sample_slc",
    )(idx, ok, page_table, q.reshape(db, N_HEADS, HEAD_DIM).astype(BF16),
      *([pk] * n_top), *([pv] * n_top),
      k_new.reshape(db * NSA_KV, 1, HEAD_DIM), v_new.reshape(db * NSA_KV, 1, HEAD_DIM))


def _rope_tables(pos):
    half = HEAD_DIM // 2
    inv = ROPE_THETA ** (-jnp.arange(half, dtype=F32) / half)
    ang = pos.astype(F32)[:, None] * inv[None, :]
    cos, sin = jnp.cos(ang), jnp.sin(ang)
    return jnp.concatenate([cos, cos], -1), jnp.concatenate([-sin, sin], -1)


def _pick(n, cands):
    for c in cands:
        if n % c == 0:
            return c
    return n


def _ffn_step(x32, x16, w_in, w_out, g, b):
    m = x32.shape[0]
    f = w_out.shape[0]
    h = glu(x16, w_in, _pick(m, (1024, 512, 256, 128)), _pick(f, (512, 256, 128)))
    return res_ln(h, w_out, x32, g, b, 0.5, _pick(m, (512, 256, 128)), _pick(f, (512, 256, 128)))


def _project(x16, w, flags, tables, tn):
    m = x16.shape[0]
    tm = _pick(min(m, tables[0].shape[0]), (1024, 512, 256, 128))
    return proj(x16, w, jnp.asarray(flags, jnp.int32), tables[0], tables[1], tm, tn)


def _out_step(o16, w_o, x32, g, b):
    m = x32.shape[0]
    return res_ln(o16, w_o, x32, g, b, 1.0, _pick(m, (512, 256, 128)), _pick(w_o.shape[0], (512, 256, 128)))


def kernel(x_prompt, x_sample, cache_swa_k, cache_swa_v, cache_nsa_kc, cache_nsa_vc, cache_nsa_ks,
           cache_nsa_vs, cache_nsa_kw, cache_nsa_vw, cache_sb_k, cache_sb_v, page_table, ln_g, ln_b,
           ffn_w_in, ffn_w_out, swa_w_in, swa_sink, swa_w_o, nsa_w_in, nsa_cmp_a, nsa_cmp_pe,
           nsa_cmp_w1, nsa_cmp_w2, nsa_w_o, sb_w_in, sb_w_o):
    bsz, t, d = x_prompt.shape
    db, tn_new, _ = x_sample.shape
    assert tn_new == 1
    depth = ffn_w_in.shape[0]
    n_pages = page_table.shape[1]
    page = cache_nsa_kc.shape[2]
    past_len = n_pages * page
    hq = N_HEADS * HEAD_DIM

    tab_p = _rope_tables(jnp.arange(t, dtype=jnp.int32))
    tab_s = _rope_tables(jnp.full((db,), past_len, jnp.int32))

    xp32 = x_prompt.reshape(bsz * t, d)
    xs32 = x_sample.reshape(db, d)
    xp16, xs16 = xp32.astype(BF16), xs32.astype(BF16)

    outs = {k: [] for k in ("swa_kp", "swa_vp", "swa_ks", "swa_vs", "nkc_p", "nvc_p", "nks_p", "nvs_p",
                            "nkw_p", "nvw_p", "nkc_s", "nvc_s", "nks_s", "nvs_s", "nkw_s", "nvw_s",
                            "sbk_p", "sbv_p", "sbk_s", "sbv_s")}

    for l in range(depth):
        kind, j = l % 3, l // 3
        w_in0 = ffn_w_in[l, 0].astype(BF16)
        w_out0 = ffn_w_out[l, 0].astype(BF16)
        xp32, xp16 = _ffn_step(xp32, xp16, w_in0, w_out0, ln_g[l, 0], ln_b[l, 0])
        xs32, xs16 = _ffn_step(xs32, xs16, w_in0, w_out0, ln_g[l, 0], ln_b[l, 0])

        if kind == 0:
            kd = SWA_KV * HEAD_DIM
            w = swa_w_in[j].astype(BF16)
            flags = [1] * ((hq + kd) // 512) + [0] * (kd // 512)
            yp = _project(xp16, w, flags, tab_p, 512)
            ys = _project(xs16, w, flags, tab_s, 512)
            op = window_prompt_attn(yp, bsz, t, SWA_KV, 0, N_HEADS, N_HEADS + SWA_KV, SWA_WINDOW,
                                    swa_sink[j], BF16)
            keep = min(SWA_WINDOW, t)
            yp3 = yp.reshape(bsz, t, -1)
            outs["swa_kp"].append(yp3[:, t - keep:, hq:hq + kd].reshape(bsz, keep, SWA_KV, HEAD_DIM))
            outs["swa_vp"].append(yp3[:, t - keep:, hq + kd:].reshape(bsz, keep, SWA_KV, HEAD_DIM))
            k_new, v_new = ys[:, hq:hq + kd], ys[:, hq + kd:]
            k_buf = cache_swa_k[j].reshape(db, -1, kd)
            v_buf = cache_swa_v[j].reshape(db, -1, kd)
            osm = window_sample_attn(ys[:, :hq], k_new, v_new, k_buf, v_buf, past_len, SWA_WINDOW,
                                     swa_sink[j], BF16)
            wb = k_buf.shape[1]
            keep_s = min(SWA_WINDOW, past_len + 1)
            kk = jnp.concatenate([k_buf, k_new[:, None]], axis=1)[:, wb + 1 - keep_s:]
            vv = jnp.concatenate([v_buf, v_new[:, None]], axis=1)[:, wb + 1 - keep_s:]
            outs["swa_ks"].append(kk.reshape(db, keep_s, SWA_KV, HEAD_DIM))
            outs["swa_vs"].append(vv.reshape(db, keep_s, SWA_KV, HEAD_DIM))
            w_o = swa_w_o[j]
        elif kind == 1:
            kd = NSA_KV * HEAD_DIM
            main = hq + 6 * kd
            w = nsa_w_in[j][:, :main].astype(BF16)
            w_gate = jnp.pad(nsa_w_in[j][:, main:], ((0, 0), (0, LANES - 3 * N_HEADS))).astype(BF16)
            flags = [1] * (hq // 512) + [1, 0, 1, 0, 1, 0]
            a_k, a_v = nsa_cmp_a[j, 0], nsa_cmp_a[j, 1]
            mlp_k = (a_k, nsa_cmp_pe[j, 0], nsa_cmp_w1[j, 0], nsa_cmp_w2[j, 0])
            mlp_v = (a_v, nsa_cmp_pe[j, 1], nsa_cmp_w1[j, 1], nsa_cmp_w2[j, 1])
            yp = _project(xp16, w, flags, tab_p, 512)
            gp = _project(xp16, w_gate, [0], tab_p, LANES)
            pp = _pick(t // 128, (8, 4, 2, 1))
            yp3 = yp.reshape(bsz, t, -1)
            seg = lambda c: yp3[:, :, hq + c * kd:hq + (c + 1) * kd].reshape(bsz, t, NSA_KV, HEAD_DIM)
            kc_p, vc_p = seg(0), seg(1)
            kcmp = cmp_mlp(*pool_prompt(kc_p, a_k, pp), *mlp_k)
            vcmp = cmp_mlp(*pool_prompt(vc_p, a_v, pp), *mlp_v)
            oc, osl = nsa_prompt_global(yp, kcmp, vcmp, bsz, t)
            ow = window_prompt_attn(yp, bsz, t, NSA_KV, 0, N_HEADS + 4 * NSA_KV, N_HEADS + 5 * NSA_KV,
                                    NSA_WINDOW, None, F32)
            op = nsa_combine(gp, oc, osl, ow, _pick(bsz * t, (512, 256, 128)))
            keep = min(NSA_WINDOW, t)
            outs["nkc_p"].append(kc_p)
            outs["nvc_p"].append(vc_p)
            outs["nks_p"].append(seg(2))
            outs["nvs_p"].append(seg(3))
            outs["nkw_p"].append(seg(4)[:, t - keep:])
            outs["nvw_p"].append(seg(5)[:, t - keep:])
            ys = _project(xs16, w, flags, tab_s, 512)
            gs = _project(xs16, w_gate, [0], tab_s, LANES)
            sseg = lambda c: ys[:, hq + c * kd:hq + (c + 1) * kd]
            n_pool = cache_nsa_kc.shape[1]
            pool = lambda c: c[j].reshape(n_pool, page, kd)
            assert page == 128
            total = past_len + 1
            n_seg = total // NSA_CMP_STRIDE
            assert n_seg == n_pages * SEG_PER_PAGE
            n_cmp = n_seg - NSA_CMP_LEN // NSA_CMP_STRIDE + 1
            n_sel = -(-total // NSA_SEL_LEN)
            pps = _pick(n_pages, (8, 4, 2, 1))
            kcmp_s = cmp_mlp(*pool_paged(cache_nsa_kc[j], page_table, a_k, pps), *mlp_k)
            vcmp_s = cmp_mlp(*pool_paged(cache_nsa_vc[j], page_table, a_v, pps), *mlp_v)
            oc_s, top_idx, top_val = nsa_sample_cmp(ys[:, :hq], kcmp_s, vcmp_s, n_cmp, n_sel, past_len)
            osl_s = nsa_sample_slc(ys[:, :hq], top_idx, top_val, pool(cache_nsa_ks), pool(cache_nsa_vs),
                                   page_table, sseg(2), sseg(3), n_sel, past_len)
            kw_buf = cache_nsa_kw[j].reshape(db, -1, kd)
            vw_buf = cache_nsa_vw[j].reshape(db, -1, kd)
            ow_s = window_sample_attn(ys[:, :hq], sseg(4), sseg(5), kw_buf, vw_buf, past_len, NSA_WINDOW,
                                      None, F32)
            osm = nsa_combine(gs, oc_s.reshape(db, hq), osl_s.reshape(db, hq), ow_s, db)
            wb = kw_buf.shape[1]
            keep_s = min(NSA_WINDOW, past_len + 1)
            kk = jnp.concatenate([kw_buf, sseg(4)[:, None]], axis=1)[:, wb + 1 - keep_s:]
            vv = jnp.concatenate([vw_buf, sseg(5)[:, None]], axis=1)[:, wb + 1 - keep_s:]
            for name, c in (("nkc_s", 0), ("nvc_s", 1), ("nks_s", 2), ("nvs_s", 3)):
                outs[name].append(sseg(c).reshape(db, 1, NSA_KV, HEAD_DIM))
            outs["nkw_s"].append(kk.reshape(db, keep_s, NSA_KV, HEAD_DIM))
            outs["nvw_s"].append(vv.reshape(db, keep_s, NSA_KV, HEAD_DIM))
            w_o = nsa_w_o[j]
        else:
            kd = SB_KV * HEAD_DIM
            w = sb_w_in[j].astype(BF16)
            flags = [0] * (w.shape[1] // 512)
            yp = _project(xp16, w, flags, tab_p, 512)
            ys = _project(xs16, w, flags, tab_s, 512)
            op = sb_prompt_attn(yp, bsz, t, BF16)
            yp3 = yp.reshape(bsz, t, -1)
            outs["sbk_p"].append(yp3[:, :, hq:hq + kd].reshape(bsz, t, SB_KV, HEAD_DIM))
            outs["sbv_p"].append(yp3[:, :, hq + kd:].reshape(bsz, t, SB_KV, HEAD_DIM))
            n_pool = cache_sb_k.shape[1]
            osm = sb_sample_attn(ys[:, :hq], cache_sb_k[j].reshape(n_pool, page, kd),
                                 cache_sb_v[j].reshape(n_pool, page, kd), page_table,
                                 _pick(n_pages, (4, 2, 1)), BF16)
            outs["sbk_s"].append(ys[:, hq:hq + kd].reshape(db, 1, SB_KV, HEAD_DIM))
            outs["sbv_s"].append(ys[:, hq + kd:].reshape(db, 1, SB_KV, HEAD_DIM))
            w_o = sb_w_o[j]

        w_o16 = w_o.astype(BF16)
        xp32, xp16 = _out_step(op, w_o16, xp32, ln_g[l, 1], ln_b[l, 1])
        xs32, xs16 = _out_step(osm, w_o16, xs32, ln_g[l, 1], ln_b[l, 1])
        w_in1 = ffn_w_in[l, 1].astype(BF16)
        w_out1 = ffn_w_out[l, 1].astype(BF16)
        xp32, xp16 = _ffn_step(xp32, xp16, w_in1, w_out1, ln_g[l, 2], ln_b[l, 2])
        xs32, xs16 = _ffn_step(xs32, xs16, w_in1, w_out1, ln_g[l, 2], ln_b[l, 2])

    st = lambda name: jnp.stack(outs[name])
    return (xp32.reshape(bsz, t, d), xs32.reshape(db, 1, d),
            st("swa_kp"), st("swa_vp"),
            st("nkc_p"), st("nvc_p"), st("nks_p"), st("nvs_p"), st("nkw_p"), st("nvw_p"),
            st("sbk_p"), st("sbv_p"),
            st("swa_ks"), st("swa_vs"),
            st("nkc_s"), st("nvc_s"), st("nks_s"), st("nvs_s"), st("nkw_s"), st("nvw_s"),
            st("sbk_s"), st("sbv_s"))
```

```python
import functools

import numpy as np
import jax
import jax.numpy as jnp
from jax import lax
from jax.experimental import pallas as pl
from jax.experimental.pallas import tpu as pltpu

F32 = jnp.float32
BF16 = jnp.bfloat16

HEAD_DIM = 128
N_HEADS = 16
Q_BLOCK = 128
SWA_KV = 4
SWA_WINDOW = 128
NSA_KV = 4
NSA_CMP_LEN = 32
NSA_CMP_STRIDE = 16
NSA_SEL_LEN = 64
NSA_TOP = 16
NSA_WINDOW = 512
SEL_FORCE = 1e4
SB_KV = 8
ROPE_THETA = 10000.0
LN_EPS = 1e-5
DEPTH = 4
DN_ALPHA = (2 * DEPTH) ** 0.25
SCALE = HEAD_DIM ** -0.5
NEG_INF = float("-inf")

LANES = 128
SUBLANES = 8
VMEM_LIMIT = 56 * 1024 * 1024


def _cparams(sem):
    return pltpu.CompilerParams(dimension_semantics=sem, vmem_limit_bytes=VMEM_LIMIT)


def _nt(a, b):
    return lax.dot_general(a, b, (((1,), (1,)), ((), ())), preferred_element_type=F32)


def _split3(x):
    hi = x.astype(BF16)
    r = x - hi.astype(F32)
    mid = r.astype(BF16)
    lo = (r - mid.astype(F32)).astype(BF16)
    return hi, mid, lo


def _dot3(x, u):
    hi, mid, lo = _split3(x)
    return (jnp.dot(hi, u, preferred_element_type=F32)
            + jnp.dot(mid, u, preferred_element_type=F32)
            + jnp.dot(lo, u, preferred_element_type=F32))


def _dot2(x, u):
    hi = x.astype(BF16)
    mid = (x - hi.astype(F32)).astype(BF16)
    return jnp.dot(hi, u, preferred_element_type=F32) + jnp.dot(mid, u, preferred_element_type=F32)


def _dot3r(u, x):
    hi, mid, lo = _split3(x)
    return (jnp.dot(u, hi, preferred_element_type=F32)
            + jnp.dot(u, mid, preferred_element_type=F32)
            + jnp.dot(u, lo, preferred_element_type=F32))


def _stack_heads(q, r):
    return jnp.concatenate([q[:, h * HEAD_DIM:(h + 1) * HEAD_DIM] for h in range(r)], axis=0)


def _tile_rows(x, r):
    return jnp.concatenate([x] * r, axis=0) if r > 1 else x


def _glu_kernel(x_ref, wg_ref, wu_ref, o_ref):
    x = x_ref[...]
    g = jnp.dot(x, wg_ref[...], preferred_element_type=F32)
    u = jnp.dot(x, wu_ref[...], preferred_element_type=F32)
    o_ref[...] = (g * jax.nn.sigmoid(g) * u).astype(o_ref.dtype)


def glu(xb, w_in, tm, tn):
    m, k = xb.shape
    f = w_in.shape[1] // 2
    nj = f // tn
    return pl.pallas_call(
        _glu_kernel,
        grid=(m // tm, nj),
        in_specs=[pl.BlockSpec((tm, k), lambda i, j: (i, 0)),
                  pl.BlockSpec((k, tn), lambda i, j: (0, j)),
                  pl.BlockSpec((k, tn), lambda i, j: (0, j + nj))],
        out_specs=pl.BlockSpec((tm, tn), lambda i, j: (i, j)),
        out_shape=jax.ShapeDtypeStruct((m, f), BF16),
        compiler_params=_cparams(("parallel", "arbitrary")),
        name="glu",
    )(xb, w_in, w_in)


def _resln_kernel(h_ref, w_ref, x_ref, g_ref, b_ref, o32_ref, o16_ref, acc_ref, *, scale, nk):
    k = pl.program_id(1)

    @pl.when(k == 0)
    def _():
        acc_ref[...] = jnp.zeros_like(acc_ref)

    acc_ref[...] += jnp.dot(h_ref[...], w_ref[...], preferred_element_type=F32)

    @pl.when(k == nk - 1)
    def _():
        y = DN_ALPHA * x_ref[...] + scale * acc_ref[...]
        mu = jnp.mean(y, axis=-1, keepdims=True)
        d = y - mu
        var = jnp.mean(d * d, axis=-1, keepdims=True)
        out = d * lax.rsqrt(var + LN_EPS) * g_ref[...] + b_ref[...]
        o32_ref[...] = out
        o16_ref[...] = out.astype(BF16)


def res_ln(hb, w, x, g, b, scale, tm, tk):
    m, kdim = hb.shape
    d = w.shape[1]
    nk = kdim // tk
    return pl.pallas_call(
        functools.partial(_resln_kernel, scale=scale, nk=nk),
        grid=(m // tm, nk),
        in_specs=[pl.BlockSpec((tm, tk), lambda i, k: (i, k)),
                  pl.BlockSpec((tk, d), lambda i, k: (k, 0)),
                  pl.BlockSpec((tm, d), lambda i, k: (i, 0)),
                  pl.BlockSpec((1, d), lambda i, k: (0, 0)),
                  pl.BlockSpec((1, d), lambda i, k: (0, 0))],
        out_specs=[pl.BlockSpec((tm, d), lambda i, k: (i, 0)),
                   pl.BlockSpec((tm, d), lambda i, k: (i, 0))],
        out_shape=[jax.ShapeDtypeStruct((m, d), F32), jax.ShapeDtypeStruct((m, d), BF16)],
        scratch_shapes=[pltpu.VMEM((tm, d), F32)],
        compiler_params=_cparams(("parallel", "arbitrary")),
        name="res_ln",
    )(hb, w, x, g.reshape(1, d), b.reshape(1, d))


def _proj_kernel(flag_ref, x_ref, w_ref, c_ref, s_ref, o_ref, *, nh):
    j = pl.program_id(1)
    y = jnp.dot(x_ref[...], w_ref[...], preferred_element_type=F32)

    @pl.when(flag_ref[j] == 0)
    def _():
        o_ref[...] = y

    @pl.when(flag_ref[j] != 0)
    def _():
        c = c_ref[...]
        s = s_ref[...]
        for h in range(nh):
            yh = y[:, h * HEAD_DIM:(h + 1) * HEAD_DIM]
            o_ref[:, h * HEAD_DIM:(h + 1) * HEAD_DIM] = yh * c + pltpu.roll(yh, HEAD_DIM // 2, 1) * s


def proj(xb, w, rope_flags, cos_t, sin_t, tm, tn):
    m, k = xb.shape
    n = w.shape[1]
    tbl_blocks = cos_t.shape[0] // tm
    grid_spec = pltpu.PrefetchScalarGridSpec(
        num_scalar_prefetch=1,
        grid=(m // tm, n // tn),
        in_specs=[pl.BlockSpec((tm, k), lambda i, j, f: (i, 0)),
                  pl.BlockSpec((k, tn), lambda i, j, f: (0, j)),
                  pl.BlockSpec((tm, HEAD_DIM), lambda i, j, f: (i % tbl_blocks, 0)),
                  pl.BlockSpec((tm, HEAD_DIM), lambda i, j, f: (i % tbl_blocks, 0))],
        out_specs=pl.BlockSpec((tm, tn), lambda i, j, f: (i, j)),
    )
    return pl.pallas_call(
        functools.partial(_proj_kernel, nh=tn // HEAD_DIM),
        grid_spec=grid_spec,
        out_shape=jax.ShapeDtypeStruct((m, n), F32),
        compiler_params=_cparams(("parallel", "arbitrary")),
        name="proj",
    )(rope_flags, xb, w, cos_t, sin_t)


def _win_kernel(*refs, window, r, qb, has_sink):
    if has_sink:
        sink_ref, q_ref, k_ref, v_ref, o_ref = refs
    else:
        q_ref, k_ref, v_ref, o_ref = refs
    g = pl.program_id(1)
    kl = window + Q_BLOCK
    if has_sink:
        s = jnp.concatenate([jnp.full((Q_BLOCK, 1), sink_ref[g * r + h], F32) for h in range(r)], axis=0)
    for u in range(qb):
        i = pl.program_id(2) * qb + u
        rows = slice(u * Q_BLOCK, (u + 1) * Q_BLOCK)
        start = pl.multiple_of(jnp.maximum(i * Q_BLOCK - window, 0), Q_BLOCK)
        kb = k_ref[pl.ds(start, kl), :].astype(BF16)
        vb = v_ref[pl.ds(start, kl), :].astype(BF16)
        qs = _stack_heads(q_ref[rows, :], r).astype(BF16)
        logits = _nt(qs, kb) * SCALE
        q_pos = i * Q_BLOCK + lax.broadcasted_iota(jnp.int32, (Q_BLOCK, kl), 0)
        k_pos = start + lax.broadcasted_iota(jnp.int32, (Q_BLOCK, kl), 1)
        rel = q_pos - k_pos
        mask = _tile_rows((rel >= 0) & (rel <= window), r)
        logits = jnp.where(mask, logits, NEG_INF)
        m = jnp.max(logits, axis=-1, keepdims=True)
        if has_sink:
            m = jnp.maximum(m, s)
        p = jnp.exp(logits - m)
        den = jnp.sum(p, axis=-1, keepdims=True)
        if has_sink:
            den = den + jnp.exp(s - m)
        o = jnp.dot((p / den).astype(BF16), vb, preferred_element_type=F32)
        for h in range(r):
            o_ref[rows, h * HEAD_DIM:(h + 1) * HEAD_DIM] = o[h * Q_BLOCK:(h + 1) * Q_BLOCK].astype(o_ref.dtype)


def window_prompt_attn(y, bsz, t, g_kv, q_col, k_col, v_col, window, sink, out_dtype):
    r = N_HEADS // g_kv
    assert window + Q_BLOCK <= t
    rw = r * HEAD_DIM
    qb = _pick(t // Q_BLOCK, (2, 1))
    nb = t // (Q_BLOCK * qb)
    y3 = y.reshape(bsz, t, y.shape[1])
    in_specs = [pl.BlockSpec((None, Q_BLOCK * qb, rw), lambda b, g, i: (b, i, q_col // r + g)),
                pl.BlockSpec((None, t, HEAD_DIM), lambda b, g, i: (b, 0, k_col + g)),
                pl.BlockSpec((None, t, HEAD_DIM), lambda b, g, i: (b, 0, v_col + g))]
    args = [y3, y3, y3]
    if sink is not None:
        in_specs = [pl.BlockSpec(memory_space=pltpu.SMEM)] + in_specs
        args = [sink] + args
    out = pl.pallas_call(
        functools.partial(_win_kernel, window=window, r=r, qb=qb, has_sink=sink is not None),
        grid=(bsz, g_kv, nb),
        in_specs=in_specs,
        out_specs=pl.BlockSpec((None, Q_BLOCK * qb, rw), lambda b, g, i: (b, i, g)),
        out_shape=jax.ShapeDtypeStruct((bsz, t, N_HEADS * HEAD_DIM), out_dtype),
        compiler_params=_cparams(("parallel", "parallel", "arbitrary")),
        name="window_prompt",
    )(*args)
    return out.reshape(bsz * t, N_HEADS * HEAD_DIM)


SB_UNROLL = 4


def _softplus_neg_abs(z):
    return jnp.log1p(jnp.exp(-jnp.abs(z)))


def _sb_kernel(q_ref, k_ref, v_ref, o_ref, *, r):
    i = pl.program_id(2)
    rq = r * Q_BLOCK
    qs = _stack_heads(q_ref[...], r).astype(BF16)
    row = lax.broadcasted_iota(jnp.int32, (Q_BLOCK, Q_BLOCK), 0)
    col = lax.broadcasted_iota(jnp.int32, (Q_BLOCK, Q_BLOCK), 1)
    later_sel = (row > col).astype(BF16)

    def body(t, carry):
        acc, cs = carry
        for u in range(SB_UNROLL):
            j = i - (t * SB_UNROLL + u)
            q_lim = jnp.where(j >= 0, i * Q_BLOCK, -Q_BLOCK)
            off = pl.multiple_of(jnp.maximum(j, 0) * Q_BLOCK, Q_BLOCK)
            kb = k_ref[pl.ds(off, Q_BLOCK), :].astype(BF16)
            vb = v_ref[pl.ds(off, Q_BLOCK), :].astype(BF16)
            z = _nt(qs, kb) * SCALE
            causal = _tile_rows((off + col) < (q_lim + row), r)
            sp = _softplus_neg_abs(z)
            log_beta = jnp.minimum(z, 0.0) - sp
            log_keep = jnp.where(causal, jnp.minimum(-z, 0.0) - sp, 0.0)
            later = _dot2(log_keep, later_sel) + cs
            a = jnp.where(causal, jnp.exp(log_beta + later), 0.0)
            acc = acc + jnp.dot(a.astype(BF16), vb, preferred_element_type=F32)
            cs = cs + jnp.sum(log_keep, axis=-1, keepdims=True)
        return acc, cs

    n_it = (i + SB_UNROLL) // SB_UNROLL
    acc, _ = lax.fori_loop(0, n_it, body, (jnp.zeros((rq, HEAD_DIM), F32), jnp.zeros((rq, 1), F32)))
    for h in range(r):
        o_ref[:, h * HEAD_DIM:(h + 1) * HEAD_DIM] = acc[h * Q_BLOCK:(h + 1) * Q_BLOCK].astype(o_ref.dtype)


def sb_prompt_attn(y, bsz, t, out_dtype):
    r = N_HEADS // SB_KV
    nb = t // Q_BLOCK
    rw = r * HEAD_DIM
    y3 = y.reshape(bsz, t, y.shape[1])
    out = pl.pallas_call(
        functools.partial(_sb_kernel, r=r),
        grid=(bsz, SB_KV, nb),
        in_specs=[pl.BlockSpec((None, Q_BLOCK, rw), lambda b, g, i: (b, i, g)),
                  pl.BlockSpec((None, t, HEAD_DIM), lambda b, g, i: (b, 0, N_HEADS + g)),
                  pl.BlockSpec((None, t, HEAD_DIM), lambda b, g, i: (b, 0, N_HEADS + SB_KV + g))],
        out_specs=pl.BlockSpec((None, Q_BLOCK, rw), lambda b, g, i: (b, i, g)),
        out_shape=jax.ShapeDtypeStruct((bsz, t, N_HEADS * HEAD_DIM), out_dtype),
        compiler_params=_cparams(("parallel", "parallel", "arbitrary")),
        name="sb_prompt",
    )(y3, y3, y3)
    return out.reshape(bsz * t, N_HEADS * HEAD_DIM)


SEG_PER_PAGE = 128 // NSA_CMP_STRIDE


def _pool_kernel(*refs, pp, g_kv, prefetch):
    refs = refs[prefetch:]
    x_refs = refs[:pp]
    w0_ref, w1_ref, u0_ref, u1_ref = refs[pp:]
    vps = NSA_CMP_STRIDE * g_kv // SUBLANES
    sub = lax.broadcasted_iota(jnp.int32, (SUBLANES, HEAD_DIM), 0)
    for s_i, x_ref in enumerate(x_refs):
        rows = slice(s_i * SEG_PER_PAGE, (s_i + 1) * SEG_PER_PAGE)
        tiles = [[jnp.zeros((SEG_PER_PAGE, HEAD_DIM), F32) for _ in range(g_kv)] for _ in range(2)]
        for n in range(SEG_PER_PAGE):
            accs = [jnp.zeros((SUBLANES, HEAD_DIM), F32), jnp.zeros((SUBLANES, HEAD_DIM), F32)]
            for v in range(vps):
                xv = x_ref[(n * vps + v) * SUBLANES:(n * vps + v + 1) * SUBLANES, :]
                accs[0] = accs[0] + xv * w0_ref[v]
                accs[1] = accs[1] + xv * w1_ref[v]
            for half in range(2):
                acc = accs[half]
                sh = g_kv
                while sh < SUBLANES:
                    acc = acc + pltpu.roll(acc, sh, 0)
                    sh *= 2
                for g in range(g_kv):
                    k = (n - g) % SUBLANES
                    moved = pltpu.roll(acc, k, 0) if k else acc
                    tiles[half][g] = jnp.where(sub == n, moved, tiles[half][g])
        for g in range(g_kv):
            u0_ref[rows, g * HEAD_DIM:(g + 1) * HEAD_DIM] = tiles[0][g]
            u1_ref[rows, g * HEAD_DIM:(g + 1) * HEAD_DIM] = tiles[1][g]


def _pool_weights(a, g_kv):
    ppv = SUBLANES // g_kv
    a_r = a.reshape(NSA_CMP_LEN // NSA_CMP_STRIDE, NSA_CMP_STRIDE // ppv, ppv, 1, HEAD_DIM)
    a_r = jnp.broadcast_to(a_r, a_r.shape[:3] + (g_kv, HEAD_DIM))
    a_r = a_r.reshape(2, NSA_CMP_STRIDE // ppv, SUBLANES, HEAD_DIM)
    return a_r[0], a_r[1]


def pool_prompt(x, a, pp):
    bsz, t, g_kv, _ = x.shape
    gw = g_kv * HEAD_DIM
    n_pages = t // 128
    xr = x.reshape(bsz, t * g_kv, HEAD_DIM)
    a0, a1 = _pool_weights(a, g_kv)
    in_specs = [pl.BlockSpec((None, 128 * g_kv, HEAD_DIM), lambda b, p, s=s: (b, p * pp + s, 0))
                for s in range(pp)]
    in_specs += [pl.BlockSpec(a0.shape, lambda b, p: (0, 0, 0))] * 2
    seg = pp * SEG_PER_PAGE
    return pl.pallas_call(
        functools.partial(_pool_kernel, pp=pp, g_kv=g_kv, prefetch=0),
        grid=(bsz, n_pages // pp),
        in_specs=in_specs,
        out_specs=[pl.BlockSpec((None, seg, gw), lambda b, p: (b, p, 0))] * 2,
        out_shape=[jax.ShapeDtypeStruct((bsz, t // NSA_CMP_STRIDE, gw), F32)] * 2,
        compiler_params=_cparams(("parallel", "arbitrary")),
        name="pool_prompt",
    )(*([xr] * pp), a0, a1)


def pool_paged(pool, page_table, a, pp):
    db, n_pages = page_table.shape
    n_pool, page, g_kv, _ = pool.shape
    assert page == 128
    gw = g_kv * HEAD_DIM
    pr = pool.reshape(n_pool, page * g_kv, HEAD_DIM)
    a0, a1 = _pool_weights(a, g_kv)
    in_specs = [pl.BlockSpec((None, page * g_kv, HEAD_DIM), lambda b, p, pt, s=s: (pt[b, p * pp + s], 0, 0))
                for s in range(pp)]
    in_specs += [pl.BlockSpec(a0.shape, lambda b, p, pt: (0, 0, 0))] * 2
    seg = pp * SEG_PER_PAGE
    grid_spec = pltpu.PrefetchScalarGridSpec(
        num_scalar_prefetch=1,
        grid=(db, n_pages // pp),
        in_specs=in_specs,
        out_specs=[pl.BlockSpec((None, seg, gw), lambda b, p, pt: (b, p, 0))] * 2,
    )
    return pl.pallas_call(
        functools.partial(_pool_kernel, pp=pp, g_kv=g_kv, prefetch=1),
        grid_spec=grid_spec,
        out_shape=[jax.ShapeDtypeStruct((db, n_pages * SEG_PER_PAGE, gw), F32)] * 2,
        compiler_params=_cparams(("parallel", "arbitrary")),
        name="pool_paged",
    )(page_table, *([pr] * pp), a0, a1)


def _cmp_mlp_kernel(u0_ref, u1_ref, a_ref, pe_ref, w1_ref, w2_ref, o_ref):
    n_seg = u0_ref.shape[0]
    c = jnp.sum(a_ref[...] * pe_ref[...], axis=0, keepdims=True)
    u = u0_ref[...] + pltpu.roll(u1_ref[...], n_seg - 1, 0)
    u = u + c
    h = jnp.dot(u.astype(BF16), w1_ref[...].astype(BF16), preferred_element_type=F32)
    h = jax.nn.gelu(h)
    o_ref[...] = jnp.dot(h.astype(BF16), w2_ref[...].astype(BF16), preferred_element_type=F32)


def cmp_mlp(u0, u1, a, pe, w1, w2):
    bx, n_seg, gw = u0.shape
    g_kv = gw // HEAD_DIM
    full = lambda arr: pl.BlockSpec(arr.shape, lambda b, g: (0,) * arr.ndim)
    return pl.pallas_call(
        _cmp_mlp_kernel,
        grid=(bx, g_kv),
        in_specs=[pl.BlockSpec((None, n_seg, HEAD_DIM), lambda b, g: (b, 0, g)),
                  pl.BlockSpec((None, n_seg, HEAD_DIM), lambda b, g: (b, 0, g)),
                  full(a), full(pe), full(w1), full(w2)],
        out_specs=pl.BlockSpec((None, n_seg, HEAD_DIM), lambda b, g: (b, 0, g)),
        out_shape=jax.ShapeDtypeStruct((bx, n_seg, gw), F32),
        compiler_params=_cparams(("parallel", "arbitrary")),
        name="cmp_mlp",
    )(u0, u1, a, pe, w1, w2)


def _cover_matrix(n_rows, n_cmp, n_cols, n_sel):
    cs = np.arange(n_rows) * NSA_CMP_STRIDE
    ss = np.arange(n_cols) * NSA_SEL_LEN
    m = (cs[:, None] < ss[None, :] + NSA_SEL_LEN) & (cs[:, None] + NSA_CMP_LEN > ss[None, :])
    m = m & (np.arange(n_rows)[:, None] < n_cmp) & (np.arange(n_cols)[None, :] < n_sel)
    return jnp.asarray(m, BF16)


def _masked_softmax(logits, mask):
    logits = jnp.where(mask, logits, NEG_INF)
    m = jnp.max(logits, axis=-1, keepdims=True)
    m = jnp.where(m == NEG_INF, 0.0, m)
    e = jnp.exp(logits - m)
    s = jnp.sum(e, axis=-1, keepdims=True)
    return e / jnp.where(s > 0, s, 1.0)


def _nsa_prompt_cmp_kernel(q_ref, kc_ref, vc_ref, cover_ref, ocmp_ref, sel_ref, *, r, n_cmp, n_sel, n_top):
    i = pl.program_id(2)
    qs = _stack_heads(q_ref[...], r).astype(BF16)
    nc = kc_ref.shape[0]

    lc = _nt(qs, kc_ref[...].astype(BF16)) * SCALE
    qp_c = i * Q_BLOCK + lax.broadcasted_iota(jnp.int32, (Q_BLOCK, nc), 0)
    n_c = lax.broadcasted_iota(jnp.int32, (Q_BLOCK, nc), 1)
    cmask = _tile_rows((n_c * NSA_CMP_STRIDE + NSA_CMP_LEN - 1 <= qp_c) & (n_c < n_cmp), r)
    pc = _masked_softmax(lc, cmask)
    ocmp = jnp.dot(pc.astype(BF16), vc_ref[...].astype(BF16), preferred_element_type=F32)
    for h in range(r):
        ocmp_ref[:, h * HEAD_DIM:(h + 1) * HEAD_DIM] = ocmp[h * Q_BLOCK:(h + 1) * Q_BLOCK]

    p_sum = pc[0:Q_BLOCK]
    for h in range(1, r):
        p_sum = p_sum + pc[h * Q_BLOCK:(h + 1) * Q_BLOCK]
    imp = _dot3(p_sum, cover_ref[...])
    qp = i * Q_BLOCK + lax.broadcasted_iota(jnp.int32, (Q_BLOCK, LANES), 0)
    jj = lax.broadcasted_iota(jnp.int32, (Q_BLOCK, LANES), 1)
    cur = qp // NSA_SEL_LEN
    vis = jj * NSA_SEL_LEN <= qp
    forced = (jj == 0) | (jj == cur) | (jj == cur - 1)
    score = jnp.where(vis, jnp.where(forced, SEL_FORCE, imp), -1.0)
    score = jnp.where(jj < n_sel, score, -2.0)
    rank = jnp.zeros((Q_BLOCK, LANES), F32)
    for t in range(n_sel):
        st = score[:, t:t + 1]
        beats = (st > score) | ((st == score) & (jj > t))
        rank = rank + jnp.where(beats, 1.0, 0.0)
    sel_ref[...] = jnp.where((rank < n_top) & (score >= 0.0), 1.0, 0.0).astype(sel_ref.dtype)


def _nsa_prompt_slc_kernel(q_ref, sel_ref, ks_ref, vs_ref, oslc_ref, *, r, kc):
    i = pl.program_id(2)
    rq = r * Q_BLOCK
    qs = _stack_heads(q_ref[...], r).astype(BF16)
    sel = sel_ref[...]
    row = lax.broadcasted_iota(jnp.int32, (Q_BLOCK, kc), 0)
    col = lax.broadcasted_iota(jnp.int32, (Q_BLOCK, kc), 1)

    def body(kb, carry):
        m, l, acc = carry
        off = pl.multiple_of(kb * kc, kc)
        k = ks_ref[pl.ds(off, kc), :].astype(BF16)
        v = vs_ref[pl.ds(off, kc), :].astype(BF16)
        s = _nt(qs, k) * SCALE
        expand = jnp.where(row == (off + col) // NSA_SEL_LEN, 1.0, 0.0).astype(BF16)
        picked = jnp.dot(sel, expand, preferred_element_type=F32) > 0.5
        mask = _tile_rows(picked & ((off + col) <= (i * Q_BLOCK + row)), r)
        s = jnp.where(mask, s, NEG_INF)
        m_new = jnp.maximum(m, jnp.max(s, axis=-1, keepdims=True))
        m_safe = jnp.where(m_new == NEG_INF, 0.0, m_new)
        p = jnp.exp(s - m_safe)
        alpha = jnp.exp(m - m_safe)
        l = alpha * l + jnp.sum(p, axis=-1, keepdims=True)
        acc = alpha * acc + jnp.dot(p.astype(BF16), v, preferred_element_type=F32)
        return m_new, l, acc

    init = (jnp.full((rq, 1), NEG_INF, F32), jnp.zeros((rq, 1), F32), jnp.zeros((rq, HEAD_DIM), F32))
    n_it = ((i + 1) * Q_BLOCK + kc - 1) // kc
    _, l, acc = lax.fori_loop(0, n_it, body, init)
    oslc = acc / jnp.where(l > 0, l, 1.0)
    for h in range(r):
        oslc_ref[:, h * HEAD_DIM:(h + 1) * HEAD_DIM] = oslc[h * Q_BLOCK:(h + 1) * Q_BLOCK]


def nsa_prompt_global(y, kcmp, vcmp, bsz, t):
    r = N_HEADS // NSA_KV
    nb = t // Q_BLOCK
    rw = r * HEAD_DIM
    n_seg = t // NSA_CMP_STRIDE
    n_cmp = n_seg - NSA_CMP_LEN // NSA_CMP_STRIDE + 1
    n_sel = -(-t // NSA_SEL_LEN)
    assert n_sel <= LANES and t % NSA_SEL_LEN == 0
    n_top = min(NSA_TOP, n_sel)
    y3 = y.reshape(bsz, t, y.shape[1])
    cover = _cover_matrix(n_seg, n_cmp, LANES, n_sel)
    ks_col = N_HEADS + 2 * NSA_KV
    vs_col = N_HEADS + 3 * NSA_KV
    o_shape = jax.ShapeDtypeStruct((bsz, t, N_HEADS * HEAD_DIM), F32)
    q_spec = pl.BlockSpec((None, Q_BLOCK, rw), lambda b, g, i: (b, i, g))
    sel_spec = pl.BlockSpec((None, None, Q_BLOCK, LANES), lambda b, g, i: (b, g, i, 0))
    ocmp, sel = pl.pallas_call(
        functools.partial(_nsa_prompt_cmp_kernel, r=r, n_cmp=n_cmp, n_sel=n_sel, n_top=n_top),
        grid=(bsz, NSA_KV, nb),
        in_specs=[q_spec,
                  pl.BlockSpec((None, n_seg, HEAD_DIM), lambda b, g, i: (b, 0, g)),
                  pl.BlockSpec((None, n_seg, HEAD_DIM), lambda b, g, i: (b, 0, g)),
                  pl.BlockSpec(cover.shape, lambda b, g, i: (0, 0))],
        out_specs=[q_spec, sel_spec],
        out_shape=[o_shape, jax.ShapeDtypeStruct((bsz, NSA_KV, t, LANES), BF16)],
        compiler_params=_cparams(("parallel", "parallel", "arbitrary")),
        name="nsa_prompt_cmp",
    )(y3, kcmp, vcmp, cover)
    oslc = pl.pallas_call(
        functools.partial(_nsa_prompt_slc_kernel, r=r, kc=_pick(t, (512, 256, 128))),
        grid=(bsz, NSA_KV, nb),
        in_specs=[q_spec, sel_spec,
                  pl.BlockSpec((None, t, HEAD_DIM), lambda b, g, i: (b, 0, ks_col + g)),
                  pl.BlockSpec((None, t, HEAD_DIM), lambda b, g, i: (b, 0, vs_col + g))],
        out_specs=q_spec,
        out_shape=o_shape,
        compiler_params=_cparams(("parallel", "parallel", "arbitrary")),
        name="nsa_prompt_slc",
    )(y3, sel, y3, y3)
    return ocmp.reshape(bsz * t, -1), oslc.reshape(bsz * t, -1)


def _combine_kernel(gl_ref, oc_ref, os_ref, ow_ref, o_ref):
    gates = jax.nn.sigmoid(gl_ref[...])
    for h in range(N_HEADS):
        sl = slice(h * HEAD_DIM, (h + 1) * HEAD_DIM)
        acc = gates[:, 3 * h:3 * h + 1] * oc_ref[:, sl]
        acc = acc + gates[:, 3 * h + 1:3 * h + 2] * os_ref[:, sl]
        acc = acc + gates[:, 3 * h + 2:3 * h + 3] * ow_ref[:, sl]
        o_ref[:, sl] = acc.astype(o_ref.dtype)


def nsa_combine(gate_logits, o_cmp, o_slc, o_win, tm):
    m, d = o_cmp.shape
    row = lambda w: pl.BlockSpec((tm, w), lambda i: (i, 0))
    return pl.pallas_call(
        _combine_kernel,
        grid=(m // tm,),
        in_specs=[row(gate_logits.shape[1]), row(d), row(d), row(d)],
        out_specs=row(d),
        out_shape=jax.ShapeDtypeStruct((m, d), BF16),
        compiler_params=_cparams(("parallel",)),
        name="nsa_combine",
    )(gate_logits, o_cmp, o_slc, o_win)


def _block_diag_q(q, g_kv):
    db = q.shape[0]
    r = N_HEADS // g_kv
    qh = q.reshape(db, N_HEADS, 1, HEAD_DIM)
    onehot = (jnp.arange(N_HEADS)[:, None] // r == jnp.arange(g_kv)[None, :]).astype(q.dtype)
    return (qh * onehot[None, :, :, None]).reshape(db, N_HEADS, g_kv * HEAD_DIM).astype(BF16)


def _diag_extract(o_all, g_kv):
    r = N_HEADS // g_kv
    hrow = lax.broadcasted_iota(jnp.int32, (N_HEADS, HEAD_DIM), 0)
    out = jnp.zeros((N_HEADS, HEAD_DIM), F32)
    for g in range(g_kv):
        out = out + jnp.where(hrow // r == g, o_all[:, g * HEAD_DIM:(g + 1) * HEAD_DIM], 0.0)
    return out


def _bf16_round(x):
    return x.astype(BF16).astype(F32)


def _group_rows(x_ref, g_kv):
    r = N_HEADS // g_kv
    hrow = lax.broadcasted_iota(jnp.int32, (N_HEADS, HEAD_DIM), 0)
    out = jnp.zeros((N_HEADS, HEAD_DIM), F32)
    for g in range(g_kv):
        out = out + jnp.where(hrow // r == g, x_ref[g:g + 1, :], 0.0)
    return out


def _win_sample_kernel(*refs, window, past_len, g_kv, has_sink):
    if has_sink:
        sink_ref, q_ref, kb_ref, vb_ref, kn_ref, vn_ref, o_ref = refs
    else:
        q_ref, kb_ref, vb_ref, kn_ref, vn_ref, o_ref = refs
    rows = kb_ref.shape[0]
    wb = rows // g_kv
    r = N_HEADS // g_kv
    q = q_ref[...]
    z = _nt(q, kb_ref[...].astype(BF16)) * SCALE
    zn = jnp.sum(q.astype(F32) * _bf16_round(_group_rows(kn_ref, g_kv)), axis=-1, keepdims=True) * SCALE
    hrow = lax.broadcasted_iota(jnp.int32, (N_HEADS, rows), 0)
    lane = lax.broadcasted_iota(jnp.int32, (N_HEADS, rows), 1)
    k_pos = past_len - wb + lane // g_kv
    rel = past_len - k_pos
    mask = (lane % g_kv == hrow // r) & (rel >= 0) & (rel <= window) & (k_pos >= 0)
    z = jnp.where(mask, z, NEG_INF)
    m = jnp.maximum(jnp.max(z, axis=-1, keepdims=True), zn)
    if has_sink:
        m = jnp.maximum(m, sink_ref[...])
    p = jnp.exp(z - m)
    pn = jnp.exp(zn - m)
    den = jnp.sum(p, axis=-1, keepdims=True) + pn
    if has_sink:
        den = den + jnp.exp(sink_ref[...] - m)
    o = jnp.dot((p / den).astype(BF16), vb_ref[...].astype(BF16), preferred_element_type=F32)
    o = o + _bf16_round(pn / den) * _bf16_round(_group_rows(vn_ref, g_kv))
    o_ref[...] = o.astype(o_ref.dtype)


def window_sample_attn(q, k_new, v_new, k_buf, v_buf, past_len, window, sink, out_dtype):
    db, wb, g_kv, _ = k_buf.shape
    rows = wb * g_kv
    in_specs = [pl.BlockSpec((None, N_HEADS, HEAD_DIM), lambda b: (b, 0, 0)),
                pl.BlockSpec((None, rows, HEAD_DIM), lambda b: (b, 0, 0)),
                pl.BlockSpec((None, rows, HEAD_DIM), lambda b: (b, 0, 0)),
                pl.BlockSpec((None, g_kv, HEAD_DIM), lambda b: (b, 0, 0)),
                pl.BlockSpec((None, g_kv, HEAD_DIM), lambda b: (b, 0, 0))]
    args = [q.reshape(db, N_HEADS, HEAD_DIM).astype(BF16),
            k_buf.reshape(db, rows, HEAD_DIM), v_buf.reshape(db, rows, HEAD_DIM),
            k_new.reshape(db, g_kv, HEAD_DIM), v_new.reshape(db, g_kv, HEAD_DIM)]
    if sink is not None:
        in_specs = [pl.BlockSpec((N_HEADS, 1), lambda b: (0, 0))] + in_specs
        args = [sink.reshape(N_HEADS, 1)] + args
    out = pl.pallas_call(
        functools.partial(_win_sample_kernel, window=window, past_len=past_len, g_kv=g_kv,
                          has_sink=sink is not None),
        grid=(db,),
        in_specs=in_specs,
        out_specs=pl.BlockSpec((None, N_HEADS, HEAD_DIM), lambda b: (b, 0, 0)),
        out_shape=jax.ShapeDtypeStruct((db, N_HEADS, HEAD_DIM), out_dtype),
        compiler_params=_cparams(("parallel",)),
        name="window_sample",
    )(*args)
    return out.reshape(db, N_HEADS * HEAD_DIM)


def _sb_sample_kernel(*refs, pp, n_pages, q_pos, g_kv):
    refs = refs[1:]
    q_ref = refs[0]
    k_refs = refs[1:1 + pp]
    v_refs = refs[1 + pp:1 + 2 * pp]
    o_ref, cs_ref, acc_ref = refs[1 + 2 * pp:]
    p = pl.program_id(1)
    r = N_HEADS // g_kv
    n_ch = g_kv
    cp = LANES // g_kv

    @pl.when(p == 0)
    def _():
        cs_ref[...] = jnp.zeros_like(cs_ref)
        acc_ref[...] = jnp.zeros_like(acc_ref)

    q = q_ref[...]
    row = lax.broadcasted_iota(jnp.int32, (LANES, LANES), 0)
    col = lax.broadcasted_iota(jnp.int32, (LANES, LANES), 1)
    later_sel = (row > col).astype(BF16)
    n_rank = pp * n_ch
    chunks = [(s, c) for s in range(pp) for c in reversed(range(n_ch))]
    srow = lax.broadcasted_iota(jnp.int32, (n_rank * N_HEADS, LANES), 0)
    slane = lax.broadcasted_iota(jnp.int32, (n_rank * N_HEADS, LANES), 1)
    own = (slane % g_kv) == ((srow % N_HEADS) // r)
    rank = srow // N_HEADS
    k_pos = ((n_pages - 1 - p * pp) * n_ch - rank) * cp + slane // g_kv
    causal = own & (k_pos < q_pos)
    z_pages = [_nt(q, k_refs[s][...].astype(BF16)) for s in range(pp)]
    z = jnp.concatenate([z_pages[s][:, c * LANES:(c + 1) * LANES] for s, c in chunks],
                        axis=0) * SCALE
    sp = _softplus_neg_abs(z)
    log_beta = jnp.minimum(z, 0.0) - sp
    log_keep = jnp.where(causal, jnp.minimum(-z, 0.0) - sp, 0.0)
    rs = jnp.sum(log_keep, axis=-1, keepdims=True)
    cs = cs_ref[...]
    carries = []
    for ci in range(n_rank):
        carries.append(cs)
        cs = cs + rs[ci * N_HEADS:(ci + 1) * N_HEADS]
    later = _dot2(log_keep, later_sel) + jnp.concatenate(carries, axis=0)
    a = jnp.where(causal, jnp.exp(log_beta + later), 0.0).astype(BF16)
    acc = acc_ref[...]
    for s in range(pp):
        ranks = [s * n_ch + (n_ch - 1 - c) for c in range(n_ch)]
        a_page = jnp.concatenate([a[k * N_HEADS:(k + 1) * N_HEADS] for k in ranks], axis=1)
        acc = acc + jnp.dot(a_page, v_refs[s][...].astype(BF16), preferred_element_type=F32)
    cs_ref[...] = cs
    acc_ref[...] = acc

    @pl.when(p == pl.num_programs(1) - 1)
    def _():
        o_ref[...] = acc.astype(o_ref.dtype)


def sb_sample_attn(q, pool_k, pool_v, page_table, pp, out_dtype):
    db, n_pages = page_table.shape
    n_pool, page, g_kv, _ = pool_k.shape
    assert page == 128
    q_pos = n_pages * page
    rows = page * g_kv
    pk = pool_k.reshape(n_pool, rows, HEAD_DIM)
    pv = pool_v.reshape(n_pool, rows, HEAD_DIM)
    page_spec = lambda s: pl.BlockSpec(
        (None, rows, HEAD_DIM), lambda b, p, pt, s=s: (pt[b, n_pages - 1 - (p * pp + s)], 0, 0))
    grid_spec = pltpu.PrefetchScalarGridSpec(
        num_scalar_prefetch=1,
        grid=(db, n_pages // pp),
        in_specs=([pl.BlockSpec((None, N_HEADS, HEAD_DIM), lambda b, p, pt: (b, 0, 0))]
                  + [page_spec(s) for s in range(pp)] * 2),
        out_specs=pl.BlockSpec((None, N_HEADS, HEAD_DIM), lambda b, p, pt: (b, 0, 0)),
        scratch_shapes=[pltpu.VMEM((N_HEADS, 1), F32), pltpu.VMEM((N_HEADS, HEAD_DIM), F32)],
    )
    out = pl.pallas_call(
        functools.partial(_sb_sample_kernel, pp=pp, n_pages=n_pages, q_pos=q_pos, g_kv=g_kv),
        grid_spec=grid_spec,
        out_shape=jax.ShapeDtypeStruct((db, N_HEADS, HEAD_DIM), out_dtype),
        compiler_params=_cparams(("parallel", "arbitrary")),
        name="sb_sample",
    )(page_table, q.reshape(db, N_HEADS, HEAD_DIM).astype(BF16), *([pk] * pp), *([pv] * pp))
    return out.reshape(db, N_HEADS * HEAD_DIM)


def _nsa_sample_cmp_kernel(q_ref, kc_ref, vc_ref, cover_ref, ocmp_ref, idx_ref, val_ref,
                           *, n_cmp, n_sel, n_top, q_pos):
    q = q_ref[...]
    nc = kc_ref.shape[0]
    ns = cover_ref.shape[1]
    r = N_HEADS // NSA_KV
    lc = _nt(q, kc_ref[...].astype(BF16)) * SCALE
    n_c = lax.broadcasted_iota(jnp.int32, (N_HEADS, nc), 1)
    cmask = (n_c * NSA_CMP_STRIDE + NSA_CMP_LEN - 1 <= q_pos) & (n_c < n_cmp)
    pc = _masked_softmax(lc, cmask)
    o_all = jnp.dot(pc.astype(BF16), vc_ref[...].astype(BF16), preferred_element_type=F32)
    ocmp_ref[...] = _diag_extract(o_all, NSA_KV)

    grow = lax.broadcasted_iota(jnp.int32, (SUBLANES, N_HEADS), 0)
    hcol = lax.broadcasted_iota(jnp.int32, (SUBLANES, N_HEADS), 1)
    group_sel = (hcol // r == grow).astype(BF16)
    p_sum = _dot3r(group_sel, pc)
    imp = _dot3(p_sum, cover_ref[...])
    jj = lax.broadcasted_iota(jnp.int32, (SUBLANES, ns), 1)
    cur = q_pos // NSA_SEL_LEN
    vis = jj * NSA_SEL_LEN <= q_pos
    forced = (jj == 0) | (jj == cur) | (jj == cur - 1)
    score = jnp.where(vis, jnp.where(forced, SEL_FORCE, imp), -1.0)
    score = jnp.where(jj < n_sel, score, -2.0)
    jf = jj.astype(F32)
    tl = lax.broadcasted_iota(jnp.int32, (SUBLANES, LANES), 1)
    idx_out = jnp.zeros((SUBLANES, LANES), F32)
    val_out = jnp.full((SUBLANES, LANES), -1.0, F32)
    for t in range(n_top):
        mx = jnp.max(score, axis=-1, keepdims=True)
        first = jnp.min(jnp.where(score == mx, jf, float(ns)), axis=-1, keepdims=True)
        idx_out = jnp.where(tl == t, first, idx_out)
        val_out = jnp.where(tl == t, mx, val_out)
        score = jnp.where(jf == first, NEG_INF, score)
    idx_ref[...] = idx_out.astype(jnp.int32)
    val_ref[...] = val_out


def nsa_sample_cmp(q, kcmp, vcmp, n_cmp, n_sel, q_pos):
    db, nc, gw = kcmp.shape
    ns = -(-n_sel // LANES) * LANES
    n_top = min(NSA_TOP, n_sel)
    cover = _cover_matrix(nc, n_cmp, ns, n_sel)
    qbd = _block_diag_q(q, NSA_KV)
    return pl.pallas_call(
        functools.partial(_nsa_sample_cmp_kernel, n_cmp=n_cmp, n_sel=n_sel, n_top=n_top, q_pos=q_pos),
        grid=(db,),
        in_specs=[pl.BlockSpec((None, N_HEADS, gw), lambda b: (b, 0, 0)),
                  pl.BlockSpec((None, nc, gw), lambda b: (b, 0, 0)),
                  pl.BlockSpec((None, nc, gw), lambda b: (b, 0, 0)),
                  pl.BlockSpec(cover.shape, lambda b: (0, 0))],
        out_specs=[pl.BlockSpec((None, N_HEADS, HEAD_DIM), lambda b: (b, 0, 0)),
                   pl.BlockSpec((None, SUBLANES, LANES), lambda b: (b, 0, 0)),
                   pl.BlockSpec((None, SUBLANES, LANES), lambda b: (b, 0, 0))],
        out_shape=[jax.ShapeDtypeStruct((db, N_HEADS, HEAD_DIM), F32),
                   jax.ShapeDtypeStruct((db, SUBLANES, LANES), jnp.int32),
                   jax.ShapeDtypeStruct((db, SUBLANES, LANES), F32)],
        compiler_params=_cparams(("parallel",)),
        name="nsa_sample_cmp",
    )(qbd, kcmp, vcmp, cover)


def _nsa_sample_slc_kernel(*refs, n_top, n_sel, q_pos):
    idx_ref, ok_ref, pt_ref = refs[:3]
    refs = refs[3:]
    q_ref = refs[0]
    k_refs = refs[1:1 + n_top]
    v_refs = refs[1 + n_top:1 + 2 * n_top]
    kn_ref, vn_ref, o_ref = refs[1 + 2 * n_top:]
    b = pl.program_id(0)
    g = pl.program_id(1)
    r = N_HEADS // NSA_KV
    rows = NSA_SEL_LEN * NSA_KV
    base = (b * NSA_KV + g) * n_top

    @pl.when(g == 0)
    def _():
        o_ref[...] = jnp.zeros_like(o_ref)

    q = q_ref[...]
    lane = lax.broadcasted_iota(jnp.int32, (N_HEADS, rows), 1)
    lane_pos = lane // NSA_KV
    lane_own = (lane % NSA_KV) == g
    zs = []
    new_ok = jnp.int32(0)
    for t in range(n_top):
        j = idx_ref[base + t]
        ok = ok_ref[base + t]
        in_pool = j < n_sel - 1
        z = _nt(q, k_refs[t][...].astype(BF16)) * SCALE
        last_pos = jnp.where((ok > 0) & in_pool, q_pos, -1)
        zs.append(jnp.where(lane_own & ((j * NSA_SEL_LEN + lane_pos) <= last_pos), z, NEG_INF))
        new_ok = new_ok | jnp.where((ok > 0) & jnp.logical_not(in_pool), 1, 0)
    new_vis = (new_ok > 0) & ((n_sel - 1) * NSA_SEL_LEN <= q_pos)
    zn = jnp.sum(q.astype(F32) * _bf16_round(kn_ref[...]), axis=-1, keepdims=True) * SCALE
    zn = zn + jnp.where(new_vis, 0.0, NEG_INF)
    m = zn
    for z in zs:
        m = jnp.maximum(m, jnp.max(z, axis=-1, keepdims=True))
    m = jnp.where(m == NEG_INF, 0.0, m)
    pn = jnp.exp(zn - m)
    den = pn
    ps = []
    for z in zs:
        e = jnp.exp(z - m)
        ps.append(e)
        den = den + jnp.sum(e, axis=-1, keepdims=True)
    den = jnp.where(den > 0, den, 1.0)
    o = _bf16_round(pn / den) * _bf16_round(vn_ref[...])
    for t in range(n_top):
        o = o + jnp.dot((ps[t] / den).astype(BF16), v_refs[t][...].astype(BF16),
                        preferred_element_type=F32)
    hrow = lax.broadcasted_iota(jnp.int32, (N_HEADS, HEAD_DIM), 0)
    o_ref[...] = jnp.where(hrow // r == g, o, o_ref[...])


def nsa_sample_slc(q, top_idx, top_val, pool_k, pool_v, page_table, k_new, v_new, n_sel, q_pos):
    db, n_pages = page_table.shape
    n_top = min(NSA_TOP, n_sel)
    n_pool, page, g_kv, _ = pool_k.shape
    assert g_kv == NSA_KV
    half = page // NSA_SEL_LEN
    rows = NSA_SEL_LEN * g_kv
    pk = pool_k.reshape(n_pool * half, rows, HEAD_DIM)
    pv = pool_v.reshape(n_pool * half, rows, HEAD_DIM)
    idx = top_idx[:, :NSA_KV, :n_top].reshape(-1)
    ok = (top_val[:, :NSA_KV, :n_top] >= 0.0).astype(jnp.int32).reshape(-1)

    def blk_spec(t):
        def index_map(b, g, idx_ref, ok_ref, pt_ref):
            j = jnp.minimum(idx_ref[(b * NSA_KV + g) * n_top + t], n_sel - 2)
            return (pt_ref[b, j // half] * half + j % half, 0, 0)
        return pl.BlockSpec((None, rows, HEAD_DIM), index_map)

    row_spec = pl.BlockSpec((None, 1, HEAD_DIM), lambda b, g, *_: (b * NSA_KV + g, 0, 0))
    grid_spec = pltpu.PrefetchScalarGridSpec(
        num_scalar_prefetch=3,
        grid=(db, NSA_KV),
        in_specs=([pl.BlockSpec((None, N_HEADS, HEAD_DIM), lambda b, g, *_: (b, 0, 0))]
                  + [blk_spec(t) for t in range(n_top)] * 2 + [row_spec, row_spec]),
        out_specs=pl.BlockSpec((None, N_HEADS, HEAD_DIM), lambda b, g, *_: (b, 0, 0)),
    )
    return pl.pallas_call(
        functools.partial(_nsa_sample_slc_kernel, n_top=n_top, n_sel=n_sel, q_pos=q_pos),
        grid_spec=grid_spec,
        out_shape=jax.ShapeDtypeStruct((db, N_HEADS, HEAD_DIM), F32),
        compiler_params=_cparams(("parallel", "arbitrary")),
        name="nsa_sample_slc",
    )(idx, ok, page_table, q.reshape(db, N_HEADS, HEAD_DIM).astype(BF16),
      *([pk] * n_top), *([pv] * n_top),
      k_new.reshape(db * NSA_KV, 1, HEAD_DIM), v_new.reshape(db * NSA_KV, 1, HEAD_DIM))


def _rope_tables(pos):
    half = HEAD_DIM // 2
    inv = ROPE_THETA ** (-jnp.arange(half, dtype=F32) / half)
    ang = pos.astype(F32)[:, None] * inv[None, :]
    cos, sin = jnp.cos(ang), jnp.sin(ang)
    return jnp.concatenate([cos, cos], -1), jnp.concatenate([-sin, sin], -1)


def _pick(n, cands):
    for c in cands:
        if n % c == 0:
            return c
    return n


def _ffn_step(x32, x16, w_in, w_out, g, b):
    m = x32.shape[0]
    f = w_out.shape[0]
    h = glu(x16, w_in, _pick(m, (1024, 512, 256, 128)), _pick(f, (512, 256, 128)))
    return res_ln(h, w_out, x32, g, b, 0.5, _pick(m, (512, 256, 128)), _pick(f, (1408, 512, 256, 128)))


def _project(x16, w, flags, tables, tn):
    m = x16.shape[0]
    tm = _pick(min(m, tables[0].shape[0]), (1024, 512, 256, 128))
    return proj(x16, w, jnp.asarray(flags, jnp.int32), tables[0], tables[1], tm, tn)


def _out_step(o16, w_o, x32, g, b):
    m = x32.shape[0]
    return res_ln(o16, w_o, x32, g, b, 1.0, _pick(m, (512, 256, 128)), _pick(w_o.shape[0], (2048, 1024, 512, 256, 128)))


def kernel(x_prompt, x_sample, cache_swa_k, cache_swa_v, cache_nsa_kc, cache_nsa_vc, cache_nsa_ks,
           cache_nsa_vs, cache_nsa_kw, cache_nsa_vw, cache_sb_k, cache_sb_v, page_table, ln_g, ln_b,
           ffn_w_in, ffn_w_out, swa_w_in, swa_sink, swa_w_o, nsa_w_in, nsa_cmp_a, nsa_cmp_pe,
           nsa_cmp_w1, nsa_cmp_w2, nsa_w_o, sb_w_in, sb_w_o):
    bsz, t, d = x_prompt.shape
    db, tn_new, _ = x_sample.shape
    assert tn_new == 1
    depth = ffn_w_in.shape[0]
    n_pages = page_table.shape[1]
    page = cache_nsa_kc.shape[2]
    past_len = n_pages * page
    hq = N_HEADS * HEAD_DIM

    tab_p = _rope_tables(jnp.arange(t, dtype=jnp.int32))
    tab_s = _rope_tables(jnp.full((db,), past_len, jnp.int32))

    xp32 = x_prompt.reshape(bsz * t, d)
    xs32 = x_sample.reshape(db, d)
    xp16, xs16 = xp32.astype(BF16), xs32.astype(BF16)

    outs = {k: [] for k in ("swa_kp", "swa_vp", "swa_ks", "swa_vs", "nkc_p", "nvc_p", "nks_p", "nvs_p",
                            "nkw_p", "nvw_p", "nkc_s", "nvc_s", "nks_s", "nvs_s", "nkw_s", "nvw_s",
                            "sbk_p", "sbv_p", "sbk_s", "sbv_s")}

    for l in range(depth):
        kind, j = l % 3, l // 3
        w_in0 = ffn_w_in[l, 0].astype(BF16)
        w_out0 = ffn_w_out[l, 0].astype(BF16)
        xp32, xp16 = _ffn_step(xp32, xp16, w_in0, w_out0, ln_g[l, 0], ln_b[l, 0])
        xs32, xs16 = _ffn_step(xs32, xs16, w_in0, w_out0, ln_g[l, 0], ln_b[l, 0])

        if kind == 0:
            kd = SWA_KV * HEAD_DIM
            w = swa_w_in[j].astype(BF16)
            flags = [1] * ((hq + kd) // 512) + [0] * (kd // 512)
            yp = _project(xp16, w, flags, tab_p, 512)
            ys = _project(xs16, w, flags, tab_s, 512)
            op = window_prompt_attn(yp, bsz, t, SWA_KV, 0, N_HEADS, N_HEADS + SWA_KV, SWA_WINDOW,
                                    swa_sink[j], BF16)
            keep = min(SWA_WINDOW, t)
            yp3 = yp.reshape(bsz, t, -1)
            outs["swa_kp"].append(yp3[:, t - keep:, hq:hq + kd].reshape(bsz, keep, SWA_KV, HEAD_DIM))
            outs["swa_vp"].append(yp3[:, t - keep:, hq + kd:].reshape(bsz, keep, SWA_KV, HEAD_DIM))
            k_new, v_new = ys[:, hq:hq + kd], ys[:, hq + kd:]
            k_buf, v_buf = cache_swa_k[j], cache_swa_v[j]
            osm = window_sample_attn(ys[:, :hq], k_new, v_new, k_buf, v_buf, past_len, SWA_WINDOW,
                                     swa_sink[j], BF16)
            wb = k_buf.shape[1]
            keep_s = min(SWA_WINDOW, past_len + 1)
            new4 = lambda x: x.reshape(db, 1, SWA_KV, HEAD_DIM)
            outs["swa_ks"].append(jnp.concatenate([k_buf, new4(k_new)], axis=1)[:, wb + 1 - keep_s:])
            outs["swa_vs"].append(jnp.concatenate([v_buf, new4(v_new)], axis=1)[:, wb + 1 - keep_s:])
            w_o = swa_w_o[j]
        elif kind == 1:
            kd = NSA_KV * HEAD_DIM
            main = hq + 6 * kd
            w = nsa_w_in[j][:, :main].astype(BF16)
            w_gate = jnp.pad(nsa_w_in[j][:, main:], ((0, 0), (0, LANES - 3 * N_HEADS))).astype(BF16)
            flags = [1] * (hq // 512) + [1, 0, 1, 0, 1, 0]
            a_k, a_v = nsa_cmp_a[j, 0], nsa_cmp_a[j, 1]
            mlp_k = (a_k, nsa_cmp_pe[j, 0], nsa_cmp_w1[j, 0], nsa_cmp_w2[j, 0])
            mlp_v = (a_v, nsa_cmp_pe[j, 1], nsa_cmp_w1[j, 1], nsa_cmp_w2[j, 1])
            yp = _project(xp16, w, flags, tab_p, 512)
            gp = _project(xp16, w_gate, [0], tab_p, LANES)
            pp = _pick(t // 128, (8, 4, 2, 1))
            yp3 = yp.reshape(bsz, t, -1)
            seg = lambda c: yp3[:, :, hq + c * kd:hq + (c + 1) * kd].reshape(bsz, t, NSA_KV, HEAD_DIM)
            kc_p, vc_p = seg(0), seg(1)
            kcmp = cmp_mlp(*pool_prompt(kc_p, a_k, pp), *mlp_k)
            vcmp = cmp_mlp(*pool_prompt(vc_p, a_v, pp), *mlp_v)
            oc, osl = nsa_prompt_global(yp, kcmp, vcmp, bsz, t)
            ow = window_prompt_attn(yp, bsz, t, NSA_KV, 0, N_HEADS + 4 * NSA_KV, N_HEADS + 5 * NSA_KV,
                                    NSA_WINDOW, None, F32)
            op = nsa_combine(gp, oc, osl, ow, _pick(bsz * t, (512, 256, 128)))
            keep = min(NSA_WINDOW, t)
            outs["nkc_p"].append(kc_p)
            outs["nvc_p"].append(vc_p)
            outs["nks_p"].append(seg(2))
            outs["nvs_p"].append(seg(3))
            outs["nkw_p"].append(seg(4)[:, t - keep:])
            outs["nvw_p"].append(seg(5)[:, t - keep:])
            ys = _project(xs16, w, flags, tab_s, 512)
            gs = _project(xs16, w_gate, [0], tab_s, LANES)
            sseg = lambda c: ys[:, hq + c * kd:hq + (c + 1) * kd]
            assert page == 128
            total = past_len + 1
            n_seg = total // NSA_CMP_STRIDE
            assert n_seg == n_pages * SEG_PER_PAGE
            n_cmp = n_seg - NSA_CMP_LEN // NSA_CMP_STRIDE + 1
            n_sel = -(-total // NSA_SEL_LEN)
            pps = _pick(n_pages, (8, 4, 2, 1))
            kcmp_s = cmp_mlp(*pool_paged(cache_nsa_kc[j], page_table, a_k, pps), *mlp_k)
            vcmp_s = cmp_mlp(*pool_paged(cache_nsa_vc[j], page_table, a_v, pps), *mlp_v)
            oc_s, top_idx, top_val = nsa_sample_cmp(ys[:, :hq], kcmp_s, vcmp_s, n_cmp, n_sel, past_len)
            osl_s = nsa_sample_slc(ys[:, :hq], top_idx, top_val, cache_nsa_ks[j], cache_nsa_vs[j],
                                   page_table, sseg(2), sseg(3), n_sel, past_len)
            kw_buf, vw_buf = cache_nsa_kw[j], cache_nsa_vw[j]
            ow_s = window_sample_attn(ys[:, :hq], sseg(4), sseg(5), kw_buf, vw_buf, past_len, NSA_WINDOW,
                                      None, F32)
            osm = nsa_combine(gs, oc_s.reshape(db, hq), osl_s.reshape(db, hq), ow_s, db)
            wb = kw_buf.shape[1]
            keep_s = min(NSA_WINDOW, past_len + 1)
            new4 = lambda c: sseg(c).reshape(db, 1, NSA_KV, HEAD_DIM)
            for name, c in (("nkc_s", 0), ("nvc_s", 1), ("nks_s", 2), ("nvs_s", 3)):
                outs[name].append(new4(c))
            outs["nkw_s"].append(jnp.concatenate([kw_buf, new4(4)], axis=1)[:, wb + 1 - keep_s:])
            outs["nvw_s"].append(jnp.concatenate([vw_buf, new4(5)], axis=1)[:, wb + 1 - keep_s:])
            w_o = nsa_w_o[j]
        else:
            kd = SB_KV * HEAD_DIM
            w = sb_w_in[j].astype(BF16)
            flags = [0] * (w.shape[1] // 512)
            yp = _project(xp16, w, flags, tab_p, 512)
            ys = _project(xs16, w, flags, tab_s, 512)
            op = sb_prompt_attn(yp, bsz, t, BF16)
            yp3 = yp.reshape(bsz, t, -1)
            outs["sbk_p"].append(yp3[:, :, hq:hq + kd].reshape(bsz, t, SB_KV, HEAD_DIM))
            outs["sbv_p"].append(yp3[:, :, hq + kd:].reshape(bsz, t, SB_KV, HEAD_DIM))
            osm = sb_sample_attn(ys[:, :hq], cache_sb_k[j], cache_sb_v[j], page_table,
                                 _pick(n_pages, (8, 4, 2, 1)), BF16)
            outs["sbk_s"].append(ys[:, hq:hq + kd].reshape(db, 1, SB_KV, HEAD_DIM))
            outs["sbv_s"].append(ys[:, hq + kd:].reshape(db, 1, SB_KV, HEAD_DIM))
            w_o = sb_w_o[j]

        w_o16 = w_o.astype(BF16)
        xp32, xp16 = _out_step(op, w_o16, xp32, ln_g[l, 1], ln_b[l, 1])
        xs32, xs16 = _out_step(osm, w_o16, xs32, ln_g[l, 1], ln_b[l, 1])
        w_in1 = ffn_w_in[l, 1].astype(BF16)
        w_out1 = ffn_w_out[l, 1].astype(BF16)
        xp32, xp16 = _ffn_step(xp32, xp16, w_in1, w_out1, ln_g[l, 2], ln_b[l, 2])
        xs32, xs16 = _ffn_step(xs32, xs16, w_in1, w_out1, ln_g[l, 2], ln_b[l, 2])

    st = lambda name: jnp.stack(outs[name])
    return (xp32.reshape(bsz, t, d), xs32.reshape(db, 1, d),
            st("swa_kp"), st("swa_vp"),
            st("nkc_p"), st("nvc_p"), st("nks_p"), st("nvs_p"), st("nkw_p"), st("nvw_p"),
            st("sbk_p"), st("sbv_p"),
            st("swa_ks"), st("swa_vs"),
            st("nkc_s"), st("nvc_s"), st("nks_s"), st("nvs_s"), st("nkw_s"), st("nvw_s"),
            st("sbk_s"), st("sbv_s"))
```

```python
import functools

import numpy as np
import jax
import jax.numpy as jnp
from jax import lax
from jax.experimental import pallas as pl
from jax.experimental.pallas import tpu as pltpu

F32 = jnp.float32
BF16 = jnp.bfloat16

HEAD_DIM = 128
N_HEADS = 16
Q_BLOCK = 128
SWA_KV = 4
SWA_WINDOW = 128
NSA_KV = 4
NSA_CMP_LEN = 32
NSA_CMP_STRIDE = 16
NSA_SEL_LEN = 64
NSA_TOP = 16
NSA_WINDOW = 512
SEL_FORCE = 1e4
SB_KV = 8
ROPE_THETA = 10000.0
LN_EPS = 1e-5
DEPTH = 4
DN_ALPHA = (2 * DEPTH) ** 0.25
SCALE = HEAD_DIM ** -0.5
NEG_INF = float("-inf")

LANES = 128
SUBLANES = 8
VMEM_LIMIT = 56 * 1024 * 1024


def _cparams(sem):
    return pltpu.CompilerParams(dimension_semantics=sem, vmem_limit_bytes=VMEM_LIMIT)


def _nt(a, b):
    return lax.dot_general(a, b, (((1,), (1,)), ((), ())), preferred_element_type=F32)


def _split3(x):
    hi = x.astype(BF16)
    r = x - hi.astype(F32)
    mid = r.astype(BF16)
    lo = (r - mid.astype(F32)).astype(BF16)
    return hi, mid, lo


def _dot3(x, u):
    hi, mid, lo = _split3(x)
    return (jnp.dot(hi, u, preferred_element_type=F32)
            + jnp.dot(mid, u, preferred_element_type=F32)
            + jnp.dot(lo, u, preferred_element_type=F32))


def _dot2(x, u):
    hi = x.astype(BF16)
    mid = (x - hi.astype(F32)).astype(BF16)
    return jnp.dot(hi, u, preferred_element_type=F32) + jnp.dot(mid, u, preferred_element_type=F32)


def _dot3r(u, x):
    hi, mid, lo = _split3(x)
    return (jnp.dot(u, hi, preferred_element_type=F32)
            + jnp.dot(u, mid, preferred_element_type=F32)
            + jnp.dot(u, lo, preferred_element_type=F32))


def _stack_heads(q, r):
    return jnp.concatenate([q[:, h * HEAD_DIM:(h + 1) * HEAD_DIM] for h in range(r)], axis=0)


def _tile_rows(x, r):
    return jnp.concatenate([x] * r, axis=0) if r > 1 else x


def _silu_mul(g, u):
    return g * jax.nn.sigmoid(g) * u


def _glu_kernel(x_ref, xs_ref, wg_ref, wu_ref, o_ref, os_ref, wg16_ref, wu16_ref):
    @pl.when(pl.program_id(1) == 0)
    def _():
        wg16_ref[...] = wg_ref[...].astype(BF16)
        wu16_ref[...] = wu_ref[...].astype(BF16)
        xs = xs_ref[...]
        gs = jnp.dot(xs, wg16_ref[...], preferred_element_type=F32)
        us = jnp.dot(xs, wu16_ref[...], preferred_element_type=F32)
        os_ref[...] = _silu_mul(gs, us).astype(os_ref.dtype)

    x = x_ref[...]
    g = jnp.dot(x, wg16_ref[...], preferred_element_type=F32)
    u = jnp.dot(x, wu16_ref[...], preferred_element_type=F32)
    o_ref[...] = _silu_mul(g, u).astype(o_ref.dtype)


def glu(xb, xsb, w_in_all, which, tm, tn):
    m, k = xb.shape
    ms = xsb.shape[0]
    f = w_in_all.shape[-1] // 2
    nj = f // tn
    l0, l1 = which
    return pl.pallas_call(
        _glu_kernel,
        grid=(nj, m // tm),
        in_specs=[pl.BlockSpec((tm, k), lambda j, i: (i, 0)),
                  pl.BlockSpec((ms, k), lambda j, i: (0, 0)),
                  pl.BlockSpec((None, None, k, tn), lambda j, i: (l0, l1, 0, j)),
                  pl.BlockSpec((None, None, k, tn), lambda j, i: (l0, l1, 0, j + nj))],
        out_specs=[pl.BlockSpec((tm, tn), lambda j, i: (i, j)),
                   pl.BlockSpec((ms, tn), lambda j, i: (0, j))],
        out_shape=[jax.ShapeDtypeStruct((m, f), BF16), jax.ShapeDtypeStruct((ms, f), BF16)],
        scratch_shapes=[pltpu.VMEM((k, tn), BF16), pltpu.VMEM((k, tn), BF16)],
        compiler_params=_cparams(("parallel", "arbitrary")),
        name="glu",
    )(xb, xsb, w_in_all, w_in_all)


def _resln_kernel(h_ref, w_ref, x_ref, g_ref, b_ref, o32_ref, o16_ref, *, scale, sub):
    for u in range(h_ref.shape[0] // sub):
        rows = slice(u * sub, (u + 1) * sub)
        acc = jnp.dot(h_ref[rows, :], w_ref[...], preferred_element_type=F32)
        y = DN_ALPHA * x_ref[rows, :] + scale * acc
        mu = jnp.mean(y, axis=-1, keepdims=True)
        d = y - mu
        var = jnp.mean(d * d, axis=-1, keepdims=True)
        out = d * lax.rsqrt(var + LN_EPS) * g_ref[...] + b_ref[...]
        o32_ref[rows, :] = out
        o16_ref[rows, :] = out.astype(BF16)


def res_ln(hb, w_all, widx, x, g, b, scale, tm, sub):
    m, kdim = hb.shape
    d = w_all.shape[-1]
    lead = (None,) * len(widx)
    return pl.pallas_call(
        functools.partial(_resln_kernel, scale=scale, sub=sub),
        grid=(m // tm,),
        in_specs=[pl.BlockSpec((tm, kdim), lambda i: (i, 0)),
                  pl.BlockSpec(lead + (kdim, d), lambda i: tuple(widx) + (0, 0), pipeline_mode=pl.Buffered(1)),
                  pl.BlockSpec((tm, d), lambda i: (i, 0)),
                  pl.BlockSpec((1, d), lambda i: (0, 0)),
                  pl.BlockSpec((1, d), lambda i: (0, 0))],
        out_specs=[pl.BlockSpec((tm, d), lambda i: (i, 0)),
                   pl.BlockSpec((tm, d), lambda i: (i, 0))],
        out_shape=[jax.ShapeDtypeStruct((m, d), F32), jax.ShapeDtypeStruct((m, d), BF16)],
        compiler_params=_cparams(("parallel",)),
        name="res_ln",
    )(hb, w_all, x, g.reshape(1, d), b.reshape(1, d))


def _proj_kernel(flag_ref, x_ref, w_ref, c_ref, s_ref, o_ref, *, nh):
    j = pl.program_id(1)
    y = jnp.dot(x_ref[...], w_ref[...], preferred_element_type=F32)

    @pl.when(flag_ref[j] == 0)
    def _():
        o_ref[...] = y

    @pl.when(flag_ref[j] != 0)
    def _():
        c = c_ref[...]
        s = s_ref[...]
        for h in range(nh):
            yh = y[:, h * HEAD_DIM:(h + 1) * HEAD_DIM]
            o_ref[:, h * HEAD_DIM:(h + 1) * HEAD_DIM] = yh * c + pltpu.roll(yh, HEAD_DIM // 2, 1) * s


def proj(xb, w_all, widx, n, rope_flags, cos_t, sin_t, tm, tn):
    m, k = xb.shape
    lead = (None,) * len(widx)
    tbl_blocks = cos_t.shape[0] // tm
    grid_spec = pltpu.PrefetchScalarGridSpec(
        num_scalar_prefetch=1,
        grid=(m // tm, n // tn),
        in_specs=[pl.BlockSpec((tm, k), lambda i, j, f: (i, 0)),
                  pl.BlockSpec(lead + (k, tn), lambda i, j, f: tuple(widx) + (0, j)),
                  pl.BlockSpec((tm, HEAD_DIM), lambda i, j, f: (i % tbl_blocks, 0)),
                  pl.BlockSpec((tm, HEAD_DIM), lambda i, j, f: (i % tbl_blocks, 0))],
        out_specs=pl.BlockSpec((tm, tn), lambda i, j, f: (i, j)),
    )
    return pl.pallas_call(
        functools.partial(_proj_kernel, nh=tn // HEAD_DIM),
        grid_spec=grid_spec,
        out_shape=jax.ShapeDtypeStruct((m, n), F32),
        compiler_params=_cparams(("parallel", "arbitrary")),
        name="proj",
    )(rope_flags, xb, w_all, cos_t, sin_t)


def _win_kernel(*refs, window, r, qb, has_sink):
    if has_sink:
        sink_ref, q_ref, k_ref, v_ref, o_ref = refs
    else:
        q_ref, k_ref, v_ref, o_ref = refs
    g = pl.program_id(1)
    kl = window + Q_BLOCK
    if has_sink:
        s = jnp.concatenate([jnp.full((Q_BLOCK, 1), sink_ref[g * r + h], F32) for h in range(r)], axis=0)
    for u in range(qb):
        i = pl.program_id(2) * qb + u
        rows = slice(u * Q_BLOCK, (u + 1) * Q_BLOCK)
        start = pl.multiple_of(jnp.maximum(i * Q_BLOCK - window, 0), Q_BLOCK)
        kb = k_ref[pl.ds(start, kl), :].astype(BF16)
        vb = v_ref[pl.ds(start, kl), :].astype(BF16)
        qs = _stack_heads(q_ref[rows, :], r).astype(BF16)
        logits = _nt(qs, kb) * SCALE
        q_pos = i * Q_BLOCK + lax.broadcasted_iota(jnp.int32, (Q_BLOCK, kl), 0)
        k_pos = start + lax.broadcasted_iota(jnp.int32, (Q_BLOCK, kl), 1)
        rel = q_pos - k_pos
        mask = _tile_rows((rel >= 0) & (rel <= window), r)
        logits = jnp.where(mask, logits, NEG_INF)
        m = jnp.max(logits, axis=-1, keepdims=True)
        if has_sink:
            m = jnp.maximum(m, s)
        p = jnp.exp(logits - m)
        den = jnp.sum(p, axis=-1, keepdims=True)
        if has_sink:
            den = den + jnp.exp(s - m)
        o = jnp.dot((p / den).astype(BF16), vb, preferred_element_type=F32)
        for h in range(r):
            o_ref[rows, h * HEAD_DIM:(h + 1) * HEAD_DIM] = o[h * Q_BLOCK:(h + 1) * Q_BLOCK].astype(o_ref.dtype)


def window_prompt_attn(y, bsz, t, g_kv, q_col, k_col, v_col, window, sink, out_dtype):
    r = N_HEADS // g_kv
    assert window + Q_BLOCK <= t
    rw = r * HEAD_DIM
    qb = _pick(t // Q_BLOCK, (2, 1))
    nb = t // (Q_BLOCK * qb)
    y3 = y.reshape(bsz, t, y.shape[1])
    in_specs = [pl.BlockSpec((None, Q_BLOCK * qb, rw), lambda b, g, i: (b, i, q_col // r + g)),
                pl.BlockSpec((None, t, HEAD_DIM), lambda b, g, i: (b, 0, k_col + g)),
                pl.BlockSpec((None, t, HEAD_DIM), lambda b, g, i: (b, 0, v_col + g))]
    args = [y3, y3, y3]
    if sink is not None:
        in_specs = [pl.BlockSpec(memory_space=pltpu.SMEM)] + in_specs
        args = [sink] + args
    out = pl.pallas_call(
        functools.partial(_win_kernel, window=window, r=r, qb=qb, has_sink=sink is not None),
        grid=(bsz, g_kv, nb),
        in_specs=in_specs,
        out_specs=pl.BlockSpec((None, Q_BLOCK * qb, rw), lambda b, g, i: (b, i, g)),
        out_shape=jax.ShapeDtypeStruct((bsz, t, N_HEADS * HEAD_DIM), out_dtype),
        compiler_params=_cparams(("parallel", "parallel", "arbitrary")),
        name="window_prompt",
    )(*args)
    return out.reshape(bsz * t, N_HEADS * HEAD_DIM)


SB_UNROLL = 4


def _softplus_neg_abs(z):
    return jnp.log(1.0 + jnp.exp(-jnp.abs(z)))


def _sb_kernel(q_ref, k_ref, v_ref, o_ref, *, r):
    i = pl.program_id(2)
    rq = r * Q_BLOCK
    qs = _stack_heads(q_ref[...], r).astype(BF16)
    row = lax.broadcasted_iota(jnp.int32, (Q_BLOCK, Q_BLOCK), 0)
    col = lax.broadcasted_iota(jnp.int32, (Q_BLOCK, Q_BLOCK), 1)
    later_sel = (row > col).astype(BF16)

    def body(t, carry):
        acc, cs = carry
        for u in range(SB_UNROLL):
            j = i - (t * SB_UNROLL + u)
            q_lim = jnp.where(j >= 0, i * Q_BLOCK, -Q_BLOCK)
            off = pl.multiple_of(jnp.maximum(j, 0) * Q_BLOCK, Q_BLOCK)
            kb = k_ref[pl.ds(off, Q_BLOCK), :].astype(BF16)
            vb = v_ref[pl.ds(off, Q_BLOCK), :].astype(BF16)
            z = _nt(qs, kb) * SCALE
            causal = _tile_rows((off + col) < (q_lim + row), r)
            sp = _softplus_neg_abs(z)
            log_beta = jnp.minimum(z, 0.0) - sp
            log_keep = jnp.where(causal, jnp.minimum(-z, 0.0) - sp, 0.0)
            later = _dot2(log_keep, later_sel) + cs
            a = jnp.where(causal, jnp.exp(log_beta + later), 0.0)
            acc = acc + jnp.dot(a.astype(BF16), vb, preferred_element_type=F32)
            cs = cs + jnp.sum(log_keep, axis=-1, keepdims=True)
        return acc, cs

    n_it = (i + SB_UNROLL) // SB_UNROLL
    acc, _ = lax.fori_loop(0, n_it, body, (jnp.zeros((rq, HEAD_DIM), F32), jnp.zeros((rq, 1), F32)))
    for h in range(r):
        o_ref[:, h * HEAD_DIM:(h + 1) * HEAD_DIM] = acc[h * Q_BLOCK:(h + 1) * Q_BLOCK].astype(o_ref.dtype)


def sb_prompt_attn(y, bsz, t, out_dtype):
    r = N_HEADS // SB_KV
    nb = t // Q_BLOCK
    rw = r * HEAD_DIM
    y3 = y.reshape(bsz, t, y.shape[1])
    out = pl.pallas_call(
        functools.partial(_sb_kernel, r=r),
        grid=(bsz, SB_KV, nb),
        in_specs=[pl.BlockSpec((None, Q_BLOCK, rw), lambda b, g, i: (b, i, g)),
                  pl.BlockSpec((None, t, HEAD_DIM), lambda b, g, i: (b, 0, N_HEADS + g)),
                  pl.BlockSpec((None, t, HEAD_DIM), lambda b, g, i: (b, 0, N_HEADS + SB_KV + g))],
        out_specs=pl.BlockSpec((None, Q_BLOCK, rw), lambda b, g, i: (b, i, g)),
        out_shape=jax.ShapeDtypeStruct((bsz, t, N_HEADS * HEAD_DIM), out_dtype),
        compiler_params=_cparams(("parallel", "parallel", "arbitrary")),
        name="sb_prompt",
    )(y3, y3, y3)
    return out.reshape(bsz * t, N_HEADS * HEAD_DIM)


SEG_PER_PAGE = 128 // NSA_CMP_STRIDE


def _pool_kernel(*refs, pp, g_kv, prefetch):
    refs = refs[prefetch:]
    x_refs = refs[:pp]
    w0_ref, w1_ref, u0_ref, u1_ref = refs[pp:]
    vps = NSA_CMP_STRIDE * g_kv // SUBLANES
    sub = lax.broadcasted_iota(jnp.int32, (SUBLANES, HEAD_DIM), 0)
    for s_i, x_ref in enumerate(x_refs):
        rows = slice(s_i * SEG_PER_PAGE, (s_i + 1) * SEG_PER_PAGE)
        tiles = [[jnp.zeros((SEG_PER_PAGE, HEAD_DIM), F32) for _ in range(g_kv)] for _ in range(2)]
        for n in range(SEG_PER_PAGE):
            accs = [jnp.zeros((SUBLANES, HEAD_DIM), F32), jnp.zeros((SUBLANES, HEAD_DIM), F32)]
            for v in range(vps):
                xv = x_ref[(n * vps + v) * SUBLANES:(n * vps + v + 1) * SUBLANES, :]
                accs[0] = accs[0] + xv * w0_ref[v]
                accs[1] = accs[1] + xv * w1_ref[v]
            for half in range(2):
                acc = accs[half]
                sh = g_kv
                while sh < SUBLANES:
                    acc = acc + pltpu.roll(acc, sh, 0)
                    sh *= 2
                for g in range(g_kv):
                    k = (n - g) % SUBLANES
                    moved = pltpu.roll(acc, k, 0) if k else acc
                    tiles[half][g] = jnp.where(sub == n, moved, tiles[half][g])
        for g in range(g_kv):
            u0_ref[rows, g * HEAD_DIM:(g + 1) * HEAD_DIM] = tiles[0][g]
            u1_ref[rows, g * HEAD_DIM:(g + 1) * HEAD_DIM] = tiles[1][g]


def _pool_weights(a, g_kv):
    ppv = SUBLANES // g_kv
    a_r = a.reshape(NSA_CMP_LEN // NSA_CMP_STRIDE, NSA_CMP_STRIDE // ppv, ppv, 1, HEAD_DIM)
    a_r = jnp.broadcast_to(a_r, a_r.shape[:3] + (g_kv, HEAD_DIM))
    a_r = a_r.reshape(2, NSA_CMP_STRIDE // ppv, SUBLANES, HEAD_DIM)
    return a_r[0], a_r[1]


def pool_prompt(x, a, pp):
    bsz, t, g_kv, _ = x.shape
    gw = g_kv * HEAD_DIM
    n_pages = t // 128
    xr = x.reshape(bsz, t * g_kv, HEAD_DIM)
    a0, a1 = _pool_weights(a, g_kv)
    in_specs = [pl.BlockSpec((None, 128 * g_kv, HEAD_DIM), lambda b, p, s=s: (b, p * pp + s, 0))
                for s in range(pp)]
    in_specs += [pl.BlockSpec(a0.shape, lambda b, p: (0, 0, 0))] * 2
    seg = pp * SEG_PER_PAGE
    return pl.pallas_call(
        functools.partial(_pool_kernel, pp=pp, g_kv=g_kv, prefetch=0),
        grid=(bsz, n_pages // pp),
        in_specs=in_specs,
        out_specs=[pl.BlockSpec((None, seg, gw), lambda b, p: (b, p, 0))] * 2,
        out_shape=[jax.ShapeDtypeStruct((bsz, t // NSA_CMP_STRIDE, gw), F32)] * 2,
        compiler_params=_cparams(("parallel", "arbitrary")),
        name="pool_prompt",
    )(*([xr] * pp), a0, a1)


def pool_paged(pool, page_table, a, pp):
    db, n_pages = page_table.shape
    n_pool, page, g_kv, _ = pool.shape
    assert page == 128
    gw = g_kv * HEAD_DIM
    pr = pool.reshape(n_pool, page * g_kv, HEAD_DIM)
    a0, a1 = _pool_weights(a, g_kv)
    in_specs = [pl.BlockSpec((None, page * g_kv, HEAD_DIM), lambda b, p, pt, s=s: (pt[b, p * pp + s], 0, 0))
                for s in range(pp)]
    in_specs += [pl.BlockSpec(a0.shape, lambda b, p, pt: (0, 0, 0))] * 2
    seg = pp * SEG_PER_PAGE
    grid_spec = pltpu.PrefetchScalarGridSpec(
        num_scalar_prefetch=1,
        grid=(db, n_pages // pp),
        in_specs=in_specs,
        out_specs=[pl.BlockSpec((None, seg, gw), lambda b, p, pt: (b, p, 0))] * 2,
    )
    return pl.pallas_call(
        functools.partial(_pool_kernel, pp=pp, g_kv=g_kv, prefetch=1),
        grid_spec=grid_spec,
        out_shape=[jax.ShapeDtypeStruct((db, n_pages * SEG_PER_PAGE, gw), F32)] * 2,
        compiler_params=_cparams(("parallel", "arbitrary")),
        name="pool_paged",
    )(page_table, *([pr] * pp), a0, a1)


def _cmp_mlp_kernel(u0_ref, u1_ref, a_ref, pe_ref, w1_ref, w2_ref, o_ref):
    n_seg = u0_ref.shape[0]
    c = jnp.sum(a_ref[...] * pe_ref[...], axis=0, keepdims=True)
    u = u0_ref[...] + pltpu.roll(u1_ref[...], n_seg - 1, 0)
    u = u + c
    h = jnp.dot(u.astype(BF16), w1_ref[...].astype(BF16), preferred_element_type=F32)
    h = jax.nn.gelu(h)
    o_ref[...] = jnp.dot(h.astype(BF16), w2_ref[...].astype(BF16), preferred_element_type=F32)


def cmp_mlp(u0, u1, a, pe, w1, w2):
    bx, n_seg, gw = u0.shape
    g_kv = gw // HEAD_DIM
    full = lambda arr: pl.BlockSpec(arr.shape, lambda b, g: (0,) * arr.ndim)
    return pl.pallas_call(
        _cmp_mlp_kernel,
        grid=(bx, g_kv),
        in_specs=[pl.BlockSpec((None, n_seg, HEAD_DIM), lambda b, g: (b, 0, g)),
                  pl.BlockSpec((None, n_seg, HEAD_DIM), lambda b, g: (b, 0, g)),
                  full(a), full(pe), full(w1), full(w2)],
        out_specs=pl.BlockSpec((None, n_seg, HEAD_DIM), lambda b, g: (b, 0, g)),
        out_shape=jax.ShapeDtypeStruct((bx, n_seg, gw), F32),
        compiler_params=_cparams(("parallel", "arbitrary")),
        name="cmp_mlp",
    )(u0, u1, a, pe, w1, w2)


def _cover_matrix(n_rows, n_cmp, n_cols, n_sel):
    cs = np.arange(n_rows) * NSA_CMP_STRIDE
    ss = np.arange(n_cols) * NSA_SEL_LEN
    m = (cs[:, None] < ss[None, :] + NSA_SEL_LEN) & (cs[:, None] + NSA_CMP_LEN > ss[None, :])
    m = m & (np.arange(n_rows)[:, None] < n_cmp) & (np.arange(n_cols)[None, :] < n_sel)
    return jnp.asarray(m, BF16)


def _masked_softmax(logits, mask):
    logits = jnp.where(mask, logits, NEG_INF)
    m = jnp.max(logits, axis=-1, keepdims=True)
    m = jnp.where(m == NEG_INF, 0.0, m)
    e = jnp.exp(logits - m)
    s = jnp.sum(e, axis=-1, keepdims=True)
    return e / jnp.where(s > 0, s, 1.0)


def _nsa_prompt_cmp_kernel(q_ref, kc_ref, vc_ref, cover_ref, ocmp_ref, sel_ref, *, r, n_cmp, n_sel, n_top):
    i = pl.program_id(2)
    qs = _stack_heads(q_ref[...], r).astype(BF16)
    nc = kc_ref.shape[0]

    lc = _nt(qs, kc_ref[...].astype(BF16)) * SCALE
    qp_c = i * Q_BLOCK + lax.broadcasted_iota(jnp.int32, (Q_BLOCK, nc), 0)
    n_c = lax.broadcasted_iota(jnp.int32, (Q_BLOCK, nc), 1)
    cmask = _tile_rows((n_c * NSA_CMP_STRIDE + NSA_CMP_LEN - 1 <= qp_c) & (n_c < n_cmp), r)
    pc = _masked_softmax(lc, cmask)
    ocmp = jnp.dot(pc.astype(BF16), vc_ref[...].astype(BF16), preferred_element_type=F32)
    for h in range(r):
        ocmp_ref[:, h * HEAD_DIM:(h + 1) * HEAD_DIM] = ocmp[h * Q_BLOCK:(h + 1) * Q_BLOCK]

    p_sum = pc[0:Q_BLOCK]
    for h in range(1, r):
        p_sum = p_sum + pc[h * Q_BLOCK:(h + 1) * Q_BLOCK]
    imp = _dot3(p_sum, cover_ref[...])
    qp = i * Q_BLOCK + lax.broadcasted_iota(jnp.int32, (Q_BLOCK, LANES), 0)
    jj = lax.broadcasted_iota(jnp.int32, (Q_BLOCK, LANES), 1)
    cur = qp // NSA_SEL_LEN
    vis = jj * NSA_SEL_LEN <= qp
    forced = (jj == 0) | (jj == cur) | (jj == cur - 1)
    score = jnp.where(vis, jnp.where(forced, SEL_FORCE, imp), -1.0)
    score = jnp.where(jj < n_sel, score, -2.0)
    rank = jnp.zeros((Q_BLOCK, LANES), F32)
    for t in range(n_sel):
        st = score[:, t:t + 1]
        beats = (st > score) | ((st == score) & (jj > t))
        rank = rank + jnp.where(beats, 1.0, 0.0)
    sel_ref[...] = jnp.where((rank < n_top) & (score >= 0.0), 1.0, 0.0).astype(sel_ref.dtype)


def _nsa_prompt_slc_kernel(q_ref, sel_ref, ks_ref, vs_ref, oslc_ref, *, r, kc):
    i = pl.program_id(2)
    rq = r * Q_BLOCK
    qs = _stack_heads(q_ref[...], r).astype(BF16)
    sel = sel_ref[...]
    row = lax.broadcasted_iota(jnp.int32, (Q_BLOCK, kc), 0)
    col = lax.broadcasted_iota(jnp.int32, (Q_BLOCK, kc), 1)

    def body(kb, carry):
        m, l, acc = carry
        off = pl.multiple_of(kb * kc, kc)
        k = ks_ref[pl.ds(off, kc), :].astype(BF16)
        v = vs_ref[pl.ds(off, kc), :].astype(BF16)
        s = _nt(qs, k) * SCALE
        expand = jnp.where(row == (off + col) // NSA_SEL_LEN, 1.0, 0.0).astype(BF16)
        picked = jnp.dot(sel, expand, preferred_element_type=F32) > 0.5
        mask = _tile_rows(picked & ((off + col) <= (i * Q_BLOCK + row)), r)
        s = jnp.where(mask, s, NEG_INF)
        m_new = jnp.maximum(m, jnp.max(s, axis=-1, keepdims=True))
        m_safe = jnp.where(m_new == NEG_INF, 0.0, m_new)
        p = jnp.exp(s - m_safe)
        alpha = jnp.exp(m - m_safe)
        l = alpha * l + jnp.sum(p, axis=-1, keepdims=True)
        acc = alpha * acc + jnp.dot(p.astype(BF16), v, preferred_element_type=F32)
        return m_new, l, acc

    init = (jnp.full((rq, 1), NEG_INF, F32), jnp.zeros((rq, 1), F32), jnp.zeros((rq, HEAD_DIM), F32))
    n_it = ((i + 1) * Q_BLOCK + kc - 1) // kc
    _, l, acc = lax.fori_loop(0, n_it, body, init)
    oslc = acc / jnp.where(l > 0, l, 1.0)
    for h in range(r):
        oslc_ref[:, h * HEAD_DIM:(h + 1) * HEAD_DIM] = oslc[h * Q_BLOCK:(h + 1) * Q_BLOCK]


def nsa_prompt_global(y, kcmp, vcmp, bsz, t):
    r = N_HEADS // NSA_KV
    nb = t // Q_BLOCK
    rw = r * HEAD_DIM
    n_seg = t // NSA_CMP_STRIDE
    n_cmp = n_seg - NSA_CMP_LEN // NSA_CMP_STRIDE + 1
    n_sel = -(-t // NSA_SEL_LEN)
    assert n_sel <= LANES and t % NSA_SEL_LEN == 0
    n_top = min(NSA_TOP, n_sel)
    y3 = y.reshape(bsz, t, y.shape[1])
    cover = _cover_matrix(n_seg, n_cmp, LANES, n_sel)
    ks_col = N_HEADS + 2 * NSA_KV
    vs_col = N_HEADS + 3 * NSA_KV
    o_shape = jax.ShapeDtypeStruct((bsz, t, N_HEADS * HEAD_DIM), F32)
    q_spec = pl.BlockSpec((None, Q_BLOCK, rw), lambda b, g, i: (b, i, g))
    sel_spec = pl.BlockSpec((None, None, Q_BLOCK, LANES), lambda b, g, i: (b, g, i, 0))
    ocmp, sel = pl.pallas_call(
        functools.partial(_nsa_prompt_cmp_kernel, r=r, n_cmp=n_cmp, n_sel=n_sel, n_top=n_top),
        grid=(bsz, NSA_KV, nb),
        in_specs=[q_spec,
                  pl.BlockSpec((None, n_seg, HEAD_DIM), lambda b, g, i: (b, 0, g)),
                  pl.BlockSpec((None, n_seg, HEAD_DIM), lambda b, g, i: (b, 0, g)),
                  pl.BlockSpec(cover.shape, lambda b, g, i: (0, 0))],
        out_specs=[q_spec, sel_spec],
        out_shape=[o_shape, jax.ShapeDtypeStruct((bsz, NSA_KV, t, LANES), BF16)],
        compiler_params=_cparams(("parallel", "parallel", "arbitrary")),
        name="nsa_prompt_cmp",
    )(y3, kcmp, vcmp, cover)
    oslc = pl.pallas_call(
        functools.partial(_nsa_prompt_slc_kernel, r=r, kc=_pick(t, (512, 256, 128))),
        grid=(bsz, NSA_KV, nb),
        in_specs=[q_spec, sel_spec,
                  pl.BlockSpec((None, t, HEAD_DIM), lambda b, g, i: (b, 0, ks_col + g)),
                  pl.BlockSpec((None, t, HEAD_DIM), lambda b, g, i: (b, 0, vs_col + g))],
        out_specs=q_spec,
        out_shape=o_shape,
        compiler_params=_cparams(("parallel", "parallel", "arbitrary")),
        name="nsa_prompt_slc",
    )(y3, sel, y3, y3)
    return ocmp.reshape(bsz * t, -1), oslc.reshape(bsz * t, -1)


def _combine_kernel(gl_ref, oc_ref, os_ref, ow_ref, o_ref):
    gates = jax.nn.sigmoid(gl_ref[...])
    for h in range(N_HEADS):
        sl = slice(h * HEAD_DIM, (h + 1) * HEAD_DIM)
        acc = gates[:, 3 * h:3 * h + 1] * oc_ref[:, sl]
        acc = acc + gates[:, 3 * h + 1:3 * h + 2] * os_ref[:, sl]
        acc = acc + gates[:, 3 * h + 2:3 * h + 3] * ow_ref[:, sl]
        o_ref[:, sl] = acc.astype(o_ref.dtype)


def nsa_combine(gate_logits, o_cmp, o_slc, o_win, tm):
    m, d = o_cmp.shape
    row = lambda w: pl.BlockSpec((tm, w), lambda i: (i, 0))
    return pl.pallas_call(
        _combine_kernel,
        grid=(m // tm,),
        in_specs=[row(gate_logits.shape[1]), row(d), row(d), row(d)],
        out_specs=row(d),
        out_shape=jax.ShapeDtypeStruct((m, d), BF16),
        compiler_params=_cparams(("parallel",)),
        name="nsa_combine",
    )(gate_logits, o_cmp, o_slc, o_win)


def _block_diag_q(q, g_kv):
    db = q.shape[0]
    r = N_HEADS // g_kv
    qh = q.reshape(db, N_HEADS, 1, HEAD_DIM)
    onehot = (jnp.arange(N_HEADS)[:, None] // r == jnp.arange(g_kv)[None, :]).astype(q.dtype)
    return (qh * onehot[None, :, :, None]).reshape(db, N_HEADS, g_kv * HEAD_DIM).astype(BF16)


def _diag_extract(o_all, g_kv):
    r = N_HEADS // g_kv
    hrow = lax.broadcasted_iota(jnp.int32, (N_HEADS, HEAD_DIM), 0)
    out = jnp.zeros((N_HEADS, HEAD_DIM), F32)
    for g in range(g_kv):
        out = out + jnp.where(hrow // r == g, o_all[:, g * HEAD_DIM:(g + 1) * HEAD_DIM], 0.0)
    return out


def _bf16_round(x):
    return x.astype(BF16).astype(F32)


def _group_rows(x_ref, g_kv):
    r = N_HEADS // g_kv
    hrow = lax.broadcasted_iota(jnp.int32, (N_HEADS, HEAD_DIM), 0)
    out = jnp.zeros((N_HEADS, HEAD_DIM), F32)
    for g in range(g_kv):
        out = out + jnp.where(hrow // r == g, x_ref[g:g + 1, :], 0.0)
    return out


def _win_sample_kernel(*refs, window, past_len, g_kv, has_sink):
    if has_sink:
        sink_ref, q_ref, kb_ref, vb_ref, kn_ref, vn_ref, o_ref = refs
    else:
        q_ref, kb_ref, vb_ref, kn_ref, vn_ref, o_ref = refs
    rows = kb_ref.shape[0]
    wb = rows // g_kv
    r = N_HEADS // g_kv
    q = q_ref[...]
    z = _nt(q, kb_ref[...].astype(BF16)) * SCALE
    zn = jnp.sum(q.astype(F32) * _bf16_round(_group_rows(kn_ref, g_kv)), axis=-1, keepdims=True) * SCALE
    hrow = lax.broadcasted_iota(jnp.int32, (N_HEADS, rows), 0)
    lane = lax.broadcasted_iota(jnp.int32, (N_HEADS, rows), 1)
    k_pos = past_len - wb + lane // g_kv
    rel = past_len - k_pos
    mask = (lane % g_kv == hrow // r) & (rel >= 0) & (rel <= window) & (k_pos >= 0)
    z = jnp.where(mask, z, NEG_INF)
    m = jnp.maximum(jnp.max(z, axis=-1, keepdims=True), zn)
    if has_sink:
        m = jnp.maximum(m, sink_ref[...])
    p = jnp.exp(z - m)
    pn = jnp.exp(zn - m)
    den = jnp.sum(p, axis=-1, keepdims=True) + pn
    if has_sink:
        den = den + jnp.exp(sink_ref[...] - m)
    o = jnp.dot((p / den).astype(BF16), vb_ref[...].astype(BF16), preferred_element_type=F32)
    o = o + _bf16_round(pn / den) * _bf16_round(_group_rows(vn_ref, g_kv))
    o_ref[...] = o.astype(o_ref.dtype)


def window_sample_attn(q, k_new, v_new, k_buf, v_buf, past_len, window, sink, out_dtype):
    db, wb, g_kv, _ = k_buf.shape
    rows = wb * g_kv
    in_specs = [pl.BlockSpec((None, N_HEADS, HEAD_DIM), lambda b: (b, 0, 0)),
                pl.BlockSpec((None, rows, HEAD_DIM), lambda b: (b, 0, 0)),
                pl.BlockSpec((None, rows, HEAD_DIM), lambda b: (b, 0, 0)),
                pl.BlockSpec((None, g_kv, HEAD_DIM), lambda b: (b, 0, 0)),
                pl.BlockSpec((None, g_kv, HEAD_DIM), lambda b: (b, 0, 0))]
    args = [q.reshape(db, N_HEADS, HEAD_DIM).astype(BF16),
            k_buf.reshape(db, rows, HEAD_DIM), v_buf.reshape(db, rows, HEAD_DIM),
            k_new.reshape(db, g_kv, HEAD_DIM), v_new.reshape(db, g_kv, HEAD_DIM)]
    if sink is not None:
        in_specs = [pl.BlockSpec((N_HEADS, 1), lambda b: (0, 0))] + in_specs
        args = [sink.reshape(N_HEADS, 1)] + args
    out = pl.pallas_call(
        functools.partial(_win_sample_kernel, window=window, past_len=past_len, g_kv=g_kv,
                          has_sink=sink is not None),
        grid=(db,),
        in_specs=in_specs,
        out_specs=pl.BlockSpec((None, N_HEADS, HEAD_DIM), lambda b: (b, 0, 0)),
        out_shape=jax.ShapeDtypeStruct((db, N_HEADS, HEAD_DIM), out_dtype),
        compiler_params=_cparams(("parallel",)),
        name="window_sample",
    )(*args)
    return out.reshape(db, N_HEADS * HEAD_DIM)


def _sb_sample_kernel(*refs, pp, n_pages, q_pos, g_kv):
    refs = refs[1:]
    q_ref = refs[0]
    k_refs = refs[1:1 + pp]
    v_refs = refs[1 + pp:1 + 2 * pp]
    o_ref, cs_ref, acc_ref = refs[1 + 2 * pp:]
    p = pl.program_id(1)
    r = N_HEADS // g_kv
    n_ch = g_kv
    cp = LANES // g_kv

    @pl.when(p == 0)
    def _():
        cs_ref[...] = jnp.zeros_like(cs_ref)
        acc_ref[...] = jnp.zeros_like(acc_ref)

    q = q_ref[...]
    row = lax.broadcasted_iota(jnp.int32, (LANES, LANES), 0)
    col = lax.broadcasted_iota(jnp.int32, (LANES, LANES), 1)
    later_sel = (row > col).astype(BF16)
    n_rank = pp * n_ch
    chunks = [(s, c) for s in range(pp) for c in reversed(range(n_ch))]
    srow = lax.broadcasted_iota(jnp.int32, (n_rank * N_HEADS, LANES), 0)
    slane = lax.broadcasted_iota(jnp.int32, (n_rank * N_HEADS, LANES), 1)
    own = (slane % g_kv) == ((srow % N_HEADS) // r)
    rank = srow // N_HEADS
    k_pos = ((n_pages - 1 - p * pp) * n_ch - rank) * cp + slane // g_kv
    causal = own & (k_pos < q_pos)
    z_pages = [_nt(q, k_refs[s][...].astype(BF16)) for s in range(pp)]
    z = jnp.concatenate([z_pages[s][:, c * LANES:(c + 1) * LANES] for s, c in chunks],
                        axis=0) * SCALE
    sp = _softplus_neg_abs(z)
    log_beta = jnp.minimum(z, 0.0) - sp
    log_keep = jnp.where(causal, jnp.minimum(-z, 0.0) - sp, 0.0)
    rs = jnp.sum(log_keep, axis=-1, keepdims=True)
    cs = cs_ref[...]
    carries = []
    for ci in range(n_rank):
        carries.append(cs)
        cs = cs + rs[ci * N_HEADS:(ci + 1) * N_HEADS]
    later = _dot2(log_keep, later_sel) + jnp.concatenate(carries, axis=0)
    a = jnp.where(causal, jnp.exp(log_beta + later), 0.0).astype(BF16)
    acc = acc_ref[...]
    for s in range(pp):
        ranks = [s * n_ch + (n_ch - 1 - c) for c in range(n_ch)]
        a_page = jnp.concatenate([a[k * N_HEADS:(k + 1) * N_HEADS] for k in ranks], axis=1)
        acc = acc + jnp.dot(a_page, v_refs[s][...].astype(BF16), preferred_element_type=F32)
    cs_ref[...] = cs
    acc_ref[...] = acc

    @pl.when(p == pl.num_programs(1) - 1)
    def _():
        o_ref[...] = acc.astype(o_ref.dtype)


def sb_sample_attn(q, pool_k, pool_v, page_table, pp, out_dtype):
    db, n_pages = page_table.shape
    n_pool, page, g_kv, _ = pool_k.shape
    assert page == 128
    q_pos = n_pages * page
    rows = page * g_kv
    pk = pool_k.reshape(n_pool, rows, HEAD_DIM)
    pv = pool_v.reshape(n_pool, rows, HEAD_DIM)
    page_spec = lambda s: pl.BlockSpec(
        (None, rows, HEAD_DIM), lambda b, p, pt, s=s: (pt[b, n_pages - 1 - (p * pp + s)], 0, 0))
    grid_spec = pltpu.PrefetchScalarGridSpec(
        num_scalar_prefetch=1,
        grid=(db, n_pages // pp),
        in_specs=([pl.BlockSpec((None, N_HEADS, HEAD_DIM), lambda b, p, pt: (b, 0, 0))]
                  + [page_spec(s) for s in range(pp)] * 2),
        out_specs=pl.BlockSpec((None, N_HEADS, HEAD_DIM), lambda b, p, pt: (b, 0, 0)),
        scratch_shapes=[pltpu.VMEM((N_HEADS, 1), F32), pltpu.VMEM((N_HEADS, HEAD_DIM), F32)],
    )
    out = pl.pallas_call(
        functools.partial(_sb_sample_kernel, pp=pp, n_pages=n_pages, q_pos=q_pos, g_kv=g_kv),
        grid_spec=grid_spec,
        out_shape=jax.ShapeDtypeStruct((db, N_HEADS, HEAD_DIM), out_dtype),
        compiler_params=_cparams(("parallel", "arbitrary")),
        name="sb_sample",
    )(page_table, q.reshape(db, N_HEADS, HEAD_DIM).astype(BF16), *([pk] * pp), *([pv] * pp))
    return out.reshape(db, N_HEADS * HEAD_DIM)


def _nsa_sample_cmp_kernel(q_ref, kc_ref, vc_ref, cover_ref, ocmp_ref, idx_ref, val_ref,
                           *, n_cmp, n_sel, n_top, q_pos):
    q = q_ref[...]
    nc = kc_ref.shape[0]
    ns = cover_ref.shape[1]
    r = N_HEADS // NSA_KV
    lc = _nt(q, kc_ref[...].astype(BF16)) * SCALE
    n_c = lax.broadcasted_iota(jnp.int32, (N_HEADS, nc), 1)
    cmask = (n_c * NSA_CMP_STRIDE + NSA_CMP_LEN - 1 <= q_pos) & (n_c < n_cmp)
    pc = _masked_softmax(lc, cmask)
    o_all = jnp.dot(pc.astype(BF16), vc_ref[...].astype(BF16), preferred_element_type=F32)
    ocmp_ref[...] = _diag_extract(o_all, NSA_KV)

    grow = lax.broadcasted_iota(jnp.int32, (SUBLANES, N_HEADS), 0)
    hcol = lax.broadcasted_iota(jnp.int32, (SUBLANES, N_HEADS), 1)
    group_sel = (hcol // r == grow).astype(BF16)
    p_sum = _dot3r(group_sel, pc)
    imp = _dot3(p_sum, cover_ref[...])
    jj = lax.broadcasted_iota(jnp.int32, (SUBLANES, ns), 1)
    cur = q_pos // NSA_SEL_LEN
    vis = jj * NSA_SEL_LEN <= q_pos
    forced = (jj == 0) | (jj == cur) | (jj == cur - 1)
    score = jnp.where(vis, jnp.where(forced, SEL_FORCE, imp), -1.0)
    score = jnp.where(jj < n_sel, score, -2.0)
    jf = jj.astype(F32)
    tl = lax.broadcasted_iota(jnp.int32, (SUBLANES, LANES), 1)
    idx_out = jnp.zeros((SUBLANES, LANES), F32)
    val_out = jnp.full((SUBLANES, LANES), -1.0, F32)
    for t in range(n_top):
        mx = jnp.max(score, axis=-1, keepdims=True)
        first = jnp.min(jnp.where(score == mx, jf, float(ns)), axis=-1, keepdims=True)
        idx_out = jnp.where(tl == t, first, idx_out)
        val_out = jnp.where(tl == t, mx, val_out)
        score = jnp.where(jf == first, NEG_INF, score)
    idx_ref[...] = idx_out.astype(jnp.int32)
    val_ref[...] = val_out


def nsa_sample_cmp(q, kcmp, vcmp, n_cmp, n_sel, q_pos):
    db, nc, gw = kcmp.shape
    ns = -(-n_sel // LANES) * LANES
    n_top = min(NSA_TOP, n_sel)
    cover = _cover_matrix(nc, n_cmp, ns, n_sel)
    qbd = _block_diag_q(q, NSA_KV)
    return pl.pallas_call(
        functools.partial(_nsa_sample_cmp_kernel, n_cmp=n_cmp, n_sel=n_sel, n_top=n_top, q_pos=q_pos),
        grid=(db,),
        in_specs=[pl.BlockSpec((None, N_HEADS, gw), lambda b: (b, 0, 0)),
                  pl.BlockSpec((None, nc, gw), lambda b: (b, 0, 0)),
                  pl.BlockSpec((None, nc, gw), lambda b: (b, 0, 0)),
                  pl.BlockSpec(cover.shape, lambda b: (0, 0))],
        out_specs=[pl.BlockSpec((None, N_HEADS, HEAD_DIM), lambda b: (b, 0, 0)),
                   pl.BlockSpec((None, SUBLANES, LANES), lambda b: (b, 0, 0)),
                   pl.BlockSpec((None, SUBLANES, LANES), lambda b: (b, 0, 0))],
        out_shape=[jax.ShapeDtypeStruct((db, N_HEADS, HEAD_DIM), F32),
                   jax.ShapeDtypeStruct((db, SUBLANES, LANES), jnp.int32),
                   jax.ShapeDtypeStruct((db, SUBLANES, LANES), F32)],
        compiler_params=_cparams(("parallel",)),
        name="nsa_sample_cmp",
    )(qbd, kcmp, vcmp, cover)


def _nsa_sample_slc_kernel(*refs, n_top, n_sel, q_pos):
    idx_ref, ok_ref, pt_ref = refs[:3]
    refs = refs[3:]
    q_ref = refs[0]
    k_refs = refs[1:1 + n_top]
    v_refs = refs[1 + n_top:1 + 2 * n_top]
    kn_ref, vn_ref, o_ref = refs[1 + 2 * n_top:]
    b = pl.program_id(0)
    g = pl.program_id(1)
    r = N_HEADS // NSA_KV
    rows = NSA_SEL_LEN * NSA_KV
    base = (b * NSA_KV + g) * n_top

    @pl.when(g == 0)
    def _():
        o_ref[...] = jnp.zeros_like(o_ref)

    q = q_ref[...]
    lane = lax.broadcasted_iota(jnp.int32, (N_HEADS, rows), 1)
    lane_pos = lane // NSA_KV
    lane_own = (lane % NSA_KV) == g
    zs = []
    new_ok = jnp.int32(0)
    for t in range(n_top):
        j = idx_ref[base + t]
        ok = ok_ref[base + t]
        in_pool = j < n_sel - 1
        z = _nt(q, k_refs[t][...].astype(BF16)) * SCALE
        last_pos = jnp.where((ok > 0) & in_pool, q_pos, -1)
        zs.append(jnp.where(lane_own & ((j * NSA_SEL_LEN + lane_pos) <= last_pos), z, NEG_INF))
        new_ok = new_ok | jnp.where((ok > 0) & jnp.logical_not(in_pool), 1, 0)
    new_vis = (new_ok > 0) & ((n_sel - 1) * NSA_SEL_LEN <= q_pos)
    zn = jnp.sum(q.astype(F32) * _bf16_round(kn_ref[...]), axis=-1, keepdims=True) * SCALE
    zn = zn + jnp.where(new_vis, 0.0, NEG_INF)
    m = zn
    for z in zs:
        m = jnp.maximum(m, jnp.max(z, axis=-1, keepdims=True))
    m = jnp.where(m == NEG_INF, 0.0, m)
    pn = jnp.exp(zn - m)
    den = pn
    ps = []
    for z in zs:
        e = jnp.exp(z - m)
        ps.append(e)
        den = den + jnp.sum(e, axis=-1, keepdims=True)
    den = jnp.where(den > 0, den, 1.0)
    o = _bf16_round(pn / den) * _bf16_round(vn_ref[...])
    for t in range(n_top):
        o = o + jnp.dot((ps[t] / den).astype(BF16), v_refs[t][...].astype(BF16),
                        preferred_element_type=F32)
    hrow = lax.broadcasted_iota(jnp.int32, (N_HEADS, HEAD_DIM), 0)
    o_ref[...] = jnp.where(hrow // r == g, o, o_ref[...])


def nsa_sample_slc(q, top_idx, top_val, pool_k, pool_v, page_table, k_new, v_new, n_sel, q_pos):
    db, n_pages = page_table.shape
    n_top = min(NSA_TOP, n_sel)
    n_pool, page, g_kv, _ = pool_k.shape
    assert g_kv == NSA_KV
    half = page // NSA_SEL_LEN
    rows = NSA_SEL_LEN * g_kv
    pk = pool_k.reshape(n_pool * half, rows, HEAD_DIM)
    pv = pool_v.reshape(n_pool * half, rows, HEAD_DIM)
    idx = top_idx[:, :NSA_KV, :n_top].reshape(-1)
    ok = (top_val[:, :NSA_KV, :n_top] >= 0.0).astype(jnp.int32).reshape(-1)

    def blk_spec(t):
        def index_map(b, g, idx_ref, ok_ref, pt_ref):
            j = jnp.minimum(idx_ref[(b * NSA_KV + g) * n_top + t], n_sel - 2)
            return (pt_ref[b, j // half] * half + j % half, 0, 0)
        return pl.BlockSpec((None, rows, HEAD_DIM), index_map)

    row_spec = pl.BlockSpec((None, 1, HEAD_DIM), lambda b, g, *_: (b * NSA_KV + g, 0, 0))
    grid_spec = pltpu.PrefetchScalarGridSpec(
        num_scalar_prefetch=3,
        grid=(db, NSA_KV),
        in_specs=([pl.BlockSpec((None, N_HEADS, HEAD_DIM), lambda b, g, *_: (b, 0, 0))]
                  + [blk_spec(t) for t in range(n_top)] * 2 + [row_spec, row_spec]),
        out_specs=pl.BlockSpec((None, N_HEADS, HEAD_DIM), lambda b, g, *_: (b, 0, 0)),
    )
    return pl.pallas_call(
        functools.partial(_nsa_sample_slc_kernel, n_top=n_top, n_sel=n_sel, q_pos=q_pos),
        grid_spec=grid_spec,
        out_shape=jax.ShapeDtypeStruct((db, N_HEADS, HEAD_DIM), F32),
        compiler_params=_cparams(("parallel", "arbitrary")),
        name="nsa_sample_slc",
    )(idx, ok, page_table, q.reshape(db, N_HEADS, HEAD_DIM).astype(BF16),
      *([pk] * n_top), *([pv] * n_top),
      k_new.reshape(db * NSA_KV, 1, HEAD_DIM), v_new.reshape(db * NSA_KV, 1, HEAD_DIM))


def _rope_tables(pos):
    half = HEAD_DIM // 2
    inv = ROPE_THETA ** (-jnp.arange(half, dtype=F32) / half)
    ang = pos.astype(F32)[:, None] * inv[None, :]
    cos, sin = jnp.cos(ang), jnp.sin(ang)
    return jnp.concatenate([cos, cos], -1), jnp.concatenate([-sin, sin], -1)


def _pick(n, cands):
    for c in cands:
        if n % c == 0:
            return c
    return n


def _res_ln_rows(h16, w16_all, widx, x32, g, b, scale):
    tm = _pick(x32.shape[0], (256, 128))
    return res_ln(h16, w16_all, widx, x32, g, b, scale, tm, _pick(tm, (128,)))


def _ffn_step(xp, xs, w_in_all, w_out16_all, which, g, b):
    m = xp[0].shape[0]
    f = w_out16_all.shape[-2]
    hp, hs = glu(xp[1], xs[1], w_in_all, which, _pick(m, (1024, 512, 256, 128)), _pick(f, (512, 256, 128)))
    return (_res_ln_rows(hp, w_out16_all, which, xp[0], g, b, 0.5),
            _res_ln_rows(hs, w_out16_all, which, xs[0], g, b, 0.5))


def _project(x16, w16_all, widx, flags, tables, tn):
    m = x16.shape[0]
    tm = _pick(min(m, tables[0].shape[0]), (1024, 512, 256, 128))
    return proj(x16, w16_all, widx, len(flags) * tn, jnp.asarray(flags, jnp.int32), tables[0], tables[1], tm, tn)


def kernel(x_prompt, x_sample, cache_swa_k, cache_swa_v, cache_nsa_kc, cache_nsa_vc, cache_nsa_ks,
           cache_nsa_vs, cache_nsa_kw, cache_nsa_vw, cache_sb_k, cache_sb_v, page_table, ln_g, ln_b,
           ffn_w_in, ffn_w_out, swa_w_in, swa_sink, swa_w_o, nsa_w_in, nsa_cmp_a, nsa_cmp_pe,
           nsa_cmp_w1, nsa_cmp_w2, nsa_w_o, sb_w_in, sb_w_o):
    bsz, t, d = x_prompt.shape
    db, tn_new, _ = x_sample.shape
    assert tn_new == 1
    depth = ffn_w_in.shape[0]
    n_pages = page_table.shape[1]
    page = cache_nsa_kc.shape[2]
    past_len = n_pages * page
    hq = N_HEADS * HEAD_DIM

    tab_p = _rope_tables(jnp.arange(t, dtype=jnp.int32))
    tab_s = _rope_tables(jnp.full((db,), past_len, jnp.int32))

    xp32 = x_prompt.reshape(bsz * t, d)
    xs32 = x_sample.reshape(db, d)
    xp16, xs16 = xp32.astype(BF16), xs32.astype(BF16)

    ffn_w_out16 = ffn_w_out.astype(BF16)
    swa_w_in16, swa_w_o16 = swa_w_in.astype(BF16), swa_w_o.astype(BF16)
    nsa_w_in16, nsa_w_o16 = nsa_w_in.astype(BF16), nsa_w_o.astype(BF16)
    sb_w_in16, sb_w_o16 = sb_w_in.astype(BF16), sb_w_o.astype(BF16)

    outs = {k: [] for k in ("swa_kp", "swa_vp", "swa_ks", "swa_vs", "nkc_p", "nvc_p", "nks_p", "nvs_p",
                            "nkw_p", "nvw_p", "nkc_s", "nvc_s", "nks_s", "nvs_s", "nkw_s", "nvw_s",
                            "sbk_p", "sbv_p", "sbk_s", "sbv_s")}

    for l in range(depth):
        kind, j = l % 3, l // 3
        (xp32, xp16), (xs32, xs16) = _ffn_step((xp32, xp16), (xs32, xs16), ffn_w_in, ffn_w_out16, (l, 0),
                                               ln_g[l, 0], ln_b[l, 0])

        if kind == 0:
            kd = SWA_KV * HEAD_DIM
            flags = [1] * ((hq + kd) // 512) + [0] * (kd // 512)
            yp = _project(xp16, swa_w_in16, (j,), flags, tab_p, 512)
            ys = _project(xs16, swa_w_in16, (j,), flags, tab_s, 512)
            op = window_prompt_attn(yp, bsz, t, SWA_KV, 0, N_HEADS, N_HEADS + SWA_KV, SWA_WINDOW,
                                    swa_sink[j], BF16)
            keep = min(SWA_WINDOW, t)
            yp3 = yp.reshape(bsz, t, -1)
            outs["swa_kp"].append(yp3[:, t - keep:, hq:hq + kd].reshape(bsz, keep, SWA_KV, HEAD_DIM))
            outs["swa_vp"].append(yp3[:, t - keep:, hq + kd:].reshape(bsz, keep, SWA_KV, HEAD_DIM))
            k_new, v_new = ys[:, hq:hq + kd], ys[:, hq + kd:]
            k_buf, v_buf = cache_swa_k[j], cache_swa_v[j]
            osm = window_sample_attn(ys[:, :hq], k_new, v_new, k_buf, v_buf, past_len, SWA_WINDOW,
                                     swa_sink[j], BF16)
            wb = k_buf.shape[1]
            keep_s = min(SWA_WINDOW, past_len + 1)
            new4 = lambda x: x.reshape(db, 1, SWA_KV, HEAD_DIM)
            outs["swa_ks"].append(jnp.concatenate([k_buf, new4(k_new)], axis=1)[:, wb + 1 - keep_s:])
            outs["swa_vs"].append(jnp.concatenate([v_buf, new4(v_new)], axis=1)[:, wb + 1 - keep_s:])
            w_o16 = swa_w_o16
        elif kind == 1:
            kd = NSA_KV * HEAD_DIM
            main = hq + 6 * kd
            w_gate = jnp.pad(nsa_w_in16[j][:, main:], ((0, 0), (0, LANES - 3 * N_HEADS)))
            flags = [1] * (hq // 512) + [1, 0, 1, 0, 1, 0]
            a_k, a_v = nsa_cmp_a[j, 0], nsa_cmp_a[j, 1]
            mlp_k = (a_k, nsa_cmp_pe[j, 0], nsa_cmp_w1[j, 0], nsa_cmp_w2[j, 0])
            mlp_v = (a_v, nsa_cmp_pe[j, 1], nsa_cmp_w1[j, 1], nsa_cmp_w2[j, 1])
            yp = _project(xp16, nsa_w_in16, (j,), flags, tab_p, 512)
            gp = _project(xp16, w_gate, (), [0], tab_p, LANES)
            pp = _pick(t // 128, (8, 4, 2, 1))
            yp3 = yp.reshape(bsz, t, -1)
            seg = lambda c: yp3[:, :, hq + c * kd:hq + (c + 1) * kd].reshape(bsz, t, NSA_KV, HEAD_DIM)
            kc_p, vc_p = seg(0), seg(1)
            kcmp = cmp_mlp(*pool_prompt(kc_p, a_k, pp), *mlp_k)
            vcmp = cmp_mlp(*pool_prompt(vc_p, a_v, pp), *mlp_v)
            oc, osl = nsa_prompt_global(yp, kcmp, vcmp, bsz, t)
            ow = window_prompt_attn(yp, bsz, t, NSA_KV, 0, N_HEADS + 4 * NSA_KV, N_HEADS + 5 * NSA_KV,
                                    NSA_WINDOW, None, F32)
            op = nsa_combine(gp, oc, osl, ow, _pick(bsz * t, (512, 256, 128)))
            keep = min(NSA_WINDOW, t)
            outs["nkc_p"].append(kc_p)
            outs["nvc_p"].append(vc_p)
            outs["nks_p"].append(seg(2))
            outs["nvs_p"].append(seg(3))
            outs["nkw_p"].append(seg(4)[:, t - keep:])
            outs["nvw_p"].append(seg(5)[:, t - keep:])
            ys = _project(xs16, nsa_w_in16, (j,), flags, tab_s, 512)
            gs = _project(xs16, w_gate, (), [0], tab_s, LANES)
            sseg = lambda c: ys[:, hq + c * kd:hq + (c + 1) * kd]
            assert page == 128
            total = past_len + 1
            n_seg = total // NSA_CMP_STRIDE
            assert n_seg == n_pages * SEG_PER_PAGE
            n_cmp = n_seg - NSA_CMP_LEN // NSA_CMP_STRIDE + 1
            n_sel = -(-total // NSA_SEL_LEN)
            pps = _pick(n_pages, (16, 8, 4, 2, 1))
            kcmp_s = cmp_mlp(*pool_paged(cache_nsa_kc[j], page_table, a_k, pps), *mlp_k)
            vcmp_s = cmp_mlp(*pool_paged(cache_nsa_vc[j], page_table, a_v, pps), *mlp_v)
            oc_s, top_idx, top_val = nsa_sample_cmp(ys[:, :hq], kcmp_s, vcmp_s, n_cmp, n_sel, past_len)
            osl_s = nsa_sample_slc(ys[:, :hq], top_idx, top_val, cache_nsa_ks[j], cache_nsa_vs[j],
                                   page_table, sseg(2), sseg(3), n_sel, past_len)
            kw_buf, vw_buf = cache_nsa_kw[j], cache_nsa_vw[j]
            ow_s = window_sample_attn(ys[:, :hq], sseg(4), sseg(5), kw_buf, vw_buf, past_len, NSA_WINDOW,
                                      None, F32)
            osm = nsa_combine(gs, oc_s.reshape(db, hq), osl_s.reshape(db, hq), ow_s, db)
            wb = kw_buf.shape[1]
            keep_s = min(NSA_WINDOW, past_len + 1)
            new4 = lambda c: sseg(c).reshape(db, 1, NSA_KV, HEAD_DIM)
            for name, c in (("nkc_s", 0), ("nvc_s", 1), ("nks_s", 2), ("nvs_s", 3)):
                outs[name].append(new4(c))
            outs["nkw_s"].append(jnp.concatenate([kw_buf, new4(4)], axis=1)[:, wb + 1 - keep_s:])
            outs["nvw_s"].append(jnp.concatenate([vw_buf, new4(5)], axis=1)[:, wb + 1 - keep_s:])
            w_o16 = nsa_w_o16
        else:
            kd = SB_KV * HEAD_DIM
            flags = [0] * (sb_w_in.shape[-1] // 512)
            yp = _project(xp16, sb_w_in16, (j,), flags, tab_p, 512)
            ys = _project(xs16, sb_w_in16, (j,), flags, tab_s, 512)
            op = sb_prompt_attn(yp, bsz, t, BF16)
            yp3 = yp.reshape(bsz, t, -1)
            outs["sbk_p"].append(yp3[:, :, hq:hq + kd].reshape(bsz, t, SB_KV, HEAD_DIM))
            outs["sbv_p"].append(yp3[:, :, hq + kd:].reshape(bsz, t, SB_KV, HEAD_DIM))
            osm = sb_sample_attn(ys[:, :hq], cache_sb_k[j], cache_sb_v[j], page_table,
                                 _pick(n_pages, (8, 4, 2, 1)), BF16)
            outs["sbk_s"].append(ys[:, hq:hq + kd].reshape(db, 1, SB_KV, HEAD_DIM))
            outs["sbv_s"].append(ys[:, hq + kd:].reshape(db, 1, SB_KV, HEAD_DIM))
            w_o16 = sb_w_o16

        xp32, xp16 = _res_ln_rows(op, w_o16, (j,), xp32, ln_g[l, 1], ln_b[l, 1], 1.0)
        xs32, xs16 = _res_ln_rows(osm, w_o16, (j,), xs32, ln_g[l, 1], ln_b[l, 1], 1.0)
        (xp32, xp16), (xs32, xs16) = _ffn_step((xp32, xp16), (xs32, xs16), ffn_w_in, ffn_w_out16, (l, 1),
                                               ln_g[l, 2], ln_b[l, 2])

    st = lambda name: jnp.stack(outs[name])
    return (xp32.reshape(bsz, t, d), xs32.reshape(db, 1, d),
            st("swa_kp"), st("swa_vp"),
            st("nkc_p"), st("nvc_p"), st("nks_p"), st("nvs_p"), st("nkw_p"), st("nvw_p"),
            st("sbk_p"), st("sbv_p"),
            st("swa_ks"), st("swa_vs"),
            st("nkc_s"), st("nvc_s"), st("nks_s"), st("nvs_s"), st("nkw_s"), st("nvw_s"),
            st("sbk_s"), st("sbv_s"))
```

```python
import functools

import numpy as np
import jax
import jax.numpy as jnp
from jax import lax
from jax.experimental import pallas as pl
from jax.experimental.pallas import tpu as pltpu

F32 = jnp.float32
BF16 = jnp.bfloat16

HEAD_DIM = 128
N_HEADS = 16
Q_BLOCK = 128
SWA_KV = 4
SWA_WINDOW = 128
NSA_KV = 4
NSA_CMP_LEN = 32
NSA_CMP_STRIDE = 16
NSA_SEL_LEN = 64
NSA_TOP = 16
NSA_WINDOW = 512
SEL_FORCE = 1e4
SB_KV = 8
ROPE_THETA = 10000.0
LN_EPS = 1e-5
DEPTH = 4
DN_ALPHA = (2 * DEPTH) ** 0.25
SCALE = HEAD_DIM ** -0.5
NEG_INF = float("-inf")

LANES = 128
SUBLANES = 8
VMEM_LIMIT = 56 * 1024 * 1024


def _cparams(sem):
    return pltpu.CompilerParams(dimension_semantics=sem, vmem_limit_bytes=VMEM_LIMIT)


def _nt(a, b):
    return lax.dot_general(a, b, (((1,), (1,)), ((), ())), preferred_element_type=F32)


def _split3(x):
    hi = x.astype(BF16)
    r = x - hi.astype(F32)
    mid = r.astype(BF16)
    lo = (r - mid.astype(F32)).astype(BF16)
    return hi, mid, lo


def _dot3(x, u):
    hi, mid, lo = _split3(x)
    return (jnp.dot(hi, u, preferred_element_type=F32)
            + jnp.dot(mid, u, preferred_element_type=F32)
            + jnp.dot(lo, u, preferred_element_type=F32))


def _dot2(x, u):
    hi = x.astype(BF16)
    mid = (x - hi.astype(F32)).astype(BF16)
    return jnp.dot(hi, u, preferred_element_type=F32) + jnp.dot(mid, u, preferred_element_type=F32)


def _dot3r(u, x):
    hi, mid, lo = _split3(x)
    return (jnp.dot(u, hi, preferred_element_type=F32)
            + jnp.dot(u, mid, preferred_element_type=F32)
            + jnp.dot(u, lo, preferred_element_type=F32))


def _stack_heads(q, r):
    return jnp.concatenate([q[:, h * HEAD_DIM:(h + 1) * HEAD_DIM] for h in range(r)], axis=0)


def _tile_rows(x, r):
    return jnp.concatenate([x] * r, axis=0) if r > 1 else x


def _silu_mul(g, u):
    return g * jax.nn.sigmoid(g) * u


def _glu_kernel(x_ref, xs_ref, wg_ref, wu_ref, o_ref, os_ref, wg16_ref, wu16_ref):
    @pl.when(pl.program_id(1) == 0)
    def _():
        wg16_ref[...] = wg_ref[...].astype(BF16)
        wu16_ref[...] = wu_ref[...].astype(BF16)
        xs = xs_ref[...]
        gs = jnp.dot(xs, wg16_ref[...], preferred_element_type=F32)
        us = jnp.dot(xs, wu16_ref[...], preferred_element_type=F32)
        os_ref[...] = _silu_mul(gs, us).astype(os_ref.dtype)

    x = x_ref[...]
    g = jnp.dot(x, wg16_ref[...], preferred_element_type=F32)
    u = jnp.dot(x, wu16_ref[...], preferred_element_type=F32)
    o_ref[...] = _silu_mul(g, u).astype(o_ref.dtype)


def glu(xb, xsb, w_in_all, which, tm, tn):
    m, k = xb.shape
    ms = xsb.shape[0]
    f = w_in_all.shape[-1] // 2
    nj = f // tn
    l0, l1 = which
    return pl.pallas_call(
        _glu_kernel,
        grid=(nj, m // tm),
        in_specs=[pl.BlockSpec((tm, k), lambda j, i: (i, 0)),
                  pl.BlockSpec((ms, k), lambda j, i: (0, 0)),
                  pl.BlockSpec((None, None, k, tn), lambda j, i: (l0, l1, 0, j)),
                  pl.BlockSpec((None, None, k, tn), lambda j, i: (l0, l1, 0, j + nj))],
        out_specs=[pl.BlockSpec((tm, tn), lambda j, i: (i, j)),
                   pl.BlockSpec((ms, tn), lambda j, i: (0, j))],
        out_shape=[jax.ShapeDtypeStruct((m, f), BF16), jax.ShapeDtypeStruct((ms, f), BF16)],
        scratch_shapes=[pltpu.VMEM((k, tn), BF16), pltpu.VMEM((k, tn), BF16)],
        compiler_params=_cparams(("parallel", "arbitrary")),
        name="glu",
    )(xb, xsb, w_in_all, w_in_all)


def _resln_kernel(h_ref, w_ref, x_ref, g_ref, b_ref, o32_ref, o16_ref, *, scale, sub):
    for u in range(h_ref.shape[0] // sub):
        rows = slice(u * sub, (u + 1) * sub)
        acc = jnp.dot(h_ref[rows, :], w_ref[...], preferred_element_type=F32)
        y = DN_ALPHA * x_ref[rows, :] + scale * acc
        mu = jnp.mean(y, axis=-1, keepdims=True)
        d = y - mu
        var = jnp.mean(d * d, axis=-1, keepdims=True)
        out = d * lax.rsqrt(var + LN_EPS) * g_ref[...] + b_ref[...]
        o32_ref[rows, :] = out
        o16_ref[rows, :] = out.astype(BF16)


def res_ln(hb, w_all, widx, x, g, b, scale, tm, sub):
    m, kdim = hb.shape
    d = w_all.shape[-1]
    lead = (None,) * len(widx)
    return pl.pallas_call(
        functools.partial(_resln_kernel, scale=scale, sub=sub),
        grid=(m // tm,),
        in_specs=[pl.BlockSpec((tm, kdim), lambda i: (i, 0)),
                  pl.BlockSpec(lead + (kdim, d), lambda i: tuple(widx) + (0, 0), pipeline_mode=pl.Buffered(1)),
                  pl.BlockSpec((tm, d), lambda i: (i, 0)),
                  pl.BlockSpec((1, d), lambda i: (0, 0)),
                  pl.BlockSpec((1, d), lambda i: (0, 0))],
        out_specs=[pl.BlockSpec((tm, d), lambda i: (i, 0)),
                   pl.BlockSpec((tm, d), lambda i: (i, 0))],
        out_shape=[jax.ShapeDtypeStruct((m, d), F32), jax.ShapeDtypeStruct((m, d), BF16)],
        compiler_params=_cparams(("parallel",)),
        name="res_ln",
    )(hb, w_all, x, g.reshape(1, d), b.reshape(1, d))


def _proj_kernel(flag_ref, x_ref, w_ref, c_ref, s_ref, o_ref, *, nh):
    j = pl.program_id(1)
    y = jnp.dot(x_ref[...], w_ref[...], preferred_element_type=F32)

    @pl.when(flag_ref[j] == 0)
    def _():
        o_ref[...] = y

    @pl.when(flag_ref[j] != 0)
    def _():
        c = c_ref[...]
        s = s_ref[...]
        for h in range(nh):
            yh = y[:, h * HEAD_DIM:(h + 1) * HEAD_DIM]
            o_ref[:, h * HEAD_DIM:(h + 1) * HEAD_DIM] = yh * c + pltpu.roll(yh, HEAD_DIM // 2, 1) * s


def proj(xb, w_all, widx, n, rope_flags, cos_t, sin_t, tm, tn):
    m, k = xb.shape
    lead = (None,) * len(widx)
    tbl_blocks = cos_t.shape[0] // tm
    grid_spec = pltpu.PrefetchScalarGridSpec(
        num_scalar_prefetch=1,
        grid=(m // tm, n // tn),
        in_specs=[pl.BlockSpec((tm, k), lambda i, j, f: (i, 0)),
                  pl.BlockSpec(lead + (k, tn), lambda i, j, f: tuple(widx) + (0, j)),
                  pl.BlockSpec((tm, HEAD_DIM), lambda i, j, f: (i % tbl_blocks, 0)),
                  pl.BlockSpec((tm, HEAD_DIM), lambda i, j, f: (i % tbl_blocks, 0))],
        out_specs=pl.BlockSpec((tm, tn), lambda i, j, f: (i, j)),
    )
    return pl.pallas_call(
        functools.partial(_proj_kernel, nh=tn // HEAD_DIM),
        grid_spec=grid_spec,
        out_shape=jax.ShapeDtypeStruct((m, n), F32),
        compiler_params=_cparams(("parallel", "arbitrary")),
        name="proj",
    )(rope_flags, xb, w_all, cos_t, sin_t)


def _win_kernel(*refs, window, r, qb, has_sink):
    if has_sink:
        sink_ref, q_ref, k_ref, v_ref, o_ref = refs
    else:
        q_ref, k_ref, v_ref, o_ref = refs
    g = pl.program_id(1)
    kl = window + Q_BLOCK
    if has_sink:
        s = jnp.concatenate([jnp.full((Q_BLOCK, 1), sink_ref[g * r + h], F32) for h in range(r)], axis=0)
    for u in range(qb):
        i = pl.program_id(2) * qb + u
        rows = slice(u * Q_BLOCK, (u + 1) * Q_BLOCK)
        start = pl.multiple_of(jnp.maximum(i * Q_BLOCK - window, 0), Q_BLOCK)
        kb = k_ref[pl.ds(start, kl), :].astype(BF16)
        vb = v_ref[pl.ds(start, kl), :].astype(BF16)
        qs = _stack_heads(q_ref[rows, :], r).astype(BF16)
        logits = _nt(qs, kb) * SCALE
        q_pos = i * Q_BLOCK + lax.broadcasted_iota(jnp.int32, (Q_BLOCK, kl), 0)
        k_pos = start + lax.broadcasted_iota(jnp.int32, (Q_BLOCK, kl), 1)
        rel = q_pos - k_pos
        mask = _tile_rows((rel >= 0) & (rel <= window), r)
        logits = jnp.where(mask, logits, NEG_INF)
        m = jnp.max(logits, axis=-1, keepdims=True)
        if has_sink:
            m = jnp.maximum(m, s)
        p = jnp.exp(logits - m)
        den = jnp.sum(p, axis=-1, keepdims=True)
        if has_sink:
            den = den + jnp.exp(s - m)
        o = jnp.dot((p / den).astype(BF16), vb, preferred_element_type=F32)
        for h in range(r):
            o_ref[rows, h * HEAD_DIM:(h + 1) * HEAD_DIM] = o[h * Q_BLOCK:(h + 1) * Q_BLOCK].astype(o_ref.dtype)


def window_prompt_attn(y, bsz, t, g_kv, q_col, k_col, v_col, window, sink, out_dtype):
    r = N_HEADS // g_kv
    assert window + Q_BLOCK <= t
    rw = r * HEAD_DIM
    qb = _pick(t // Q_BLOCK, (8, 4, 2, 1))
    nb = t // (Q_BLOCK * qb)
    y3 = y.reshape(bsz, t, y.shape[1])
    in_specs = [pl.BlockSpec((None, Q_BLOCK * qb, rw), lambda b, g, i: (b, i, q_col // r + g)),
                pl.BlockSpec((None, t, HEAD_DIM), lambda b, g, i: (b, 0, k_col + g)),
                pl.BlockSpec((None, t, HEAD_DIM), lambda b, g, i: (b, 0, v_col + g))]
    args = [y3, y3, y3]
    if sink is not None:
        in_specs = [pl.BlockSpec(memory_space=pltpu.SMEM)] + in_specs
        args = [sink] + args
    out = pl.pallas_call(
        functools.partial(_win_kernel, window=window, r=r, qb=qb, has_sink=sink is not None),
        grid=(bsz, g_kv, nb),
        in_specs=in_specs,
        out_specs=pl.BlockSpec((None, Q_BLOCK * qb, rw), lambda b, g, i: (b, i, g)),
        out_shape=jax.ShapeDtypeStruct((bsz, t, N_HEADS * HEAD_DIM), out_dtype),
        compiler_params=_cparams(("parallel", "parallel", "arbitrary")),
        name="window_prompt",
    )(*args)
    return out.reshape(bsz * t, N_HEADS * HEAD_DIM)


SB_UNROLL = 3
SB_CUTOFF = -104.0


def _softplus_neg_abs(z):
    return jnp.log(1.0 + jnp.exp(-jnp.abs(z)))


def _sb_kernel(q_ref, k_ref, v_ref, o_ref, *, r):
    i = pl.program_id(2)
    rq = r * Q_BLOCK
    qs = _stack_heads(q_ref[...], r).astype(BF16)
    row = lax.broadcasted_iota(jnp.int32, (Q_BLOCK, Q_BLOCK), 0)
    col = lax.broadcasted_iota(jnp.int32, (Q_BLOCK, Q_BLOCK), 1)
    later_sel = (row > col).astype(BF16)

    n_it = (i + SB_UNROLL) // SB_UNROLL

    def cond(carry):
        t, live, _, _ = carry
        return jnp.logical_and(t < n_it, live > 0)

    def body(carry):
        t, _, acc, cs = carry
        for u in range(SB_UNROLL):
            j = i - (t * SB_UNROLL + u)
            q_lim = jnp.where(j >= 0, i * Q_BLOCK, -Q_BLOCK)
            off = pl.multiple_of(jnp.maximum(j, 0) * Q_BLOCK, Q_BLOCK)
            kb = k_ref[pl.ds(off, Q_BLOCK), :].astype(BF16)
            vb = v_ref[pl.ds(off, Q_BLOCK), :].astype(BF16)
            z = _nt(qs, kb) * SCALE
            causal = _tile_rows((off + col) < (q_lim + row), r)
            sp = _softplus_neg_abs(z)
            log_beta = jnp.minimum(z, 0.0) - sp
            log_keep = jnp.where(causal, jnp.minimum(-z, 0.0) - sp, 0.0)
            later = _dot2(log_keep, later_sel) + cs
            a = jnp.where(causal, jnp.exp(log_beta + later), 0.0)
            acc = acc + jnp.dot(a.astype(BF16), vb, preferred_element_type=F32)
            cs = cs + jnp.sum(log_keep, axis=-1, keepdims=True)
        live = jnp.where(jnp.max(cs) > SB_CUTOFF, 1, 0)
        return t + 1, live, acc, cs

    init = (jnp.int32(0), jnp.int32(1), jnp.zeros((rq, HEAD_DIM), F32), jnp.zeros((rq, 1), F32))
    _, _, acc, _ = lax.while_loop(cond, body, init)
    for h in range(r):
        o_ref[:, h * HEAD_DIM:(h + 1) * HEAD_DIM] = acc[h * Q_BLOCK:(h + 1) * Q_BLOCK].astype(o_ref.dtype)


def sb_prompt_attn(y, bsz, t, out_dtype):
    r = N_HEADS // SB_KV
    nb = t // Q_BLOCK
    rw = r * HEAD_DIM
    y3 = y.reshape(bsz, t, y.shape[1])
    out = pl.pallas_call(
        functools.partial(_sb_kernel, r=r),
        grid=(bsz, SB_KV, nb),
        in_specs=[pl.BlockSpec((None, Q_BLOCK, rw), lambda b, g, i: (b, i, g)),
                  pl.BlockSpec((None, t, HEAD_DIM), lambda b, g, i: (b, 0, N_HEADS + g)),
                  pl.BlockSpec((None, t, HEAD_DIM), lambda b, g, i: (b, 0, N_HEADS + SB_KV + g))],
        out_specs=pl.BlockSpec((None, Q_BLOCK, rw), lambda b, g, i: (b, i, g)),
        out_shape=jax.ShapeDtypeStruct((bsz, t, N_HEADS * HEAD_DIM), out_dtype),
        compiler_params=_cparams(("parallel", "parallel", "arbitrary")),
        name="sb_prompt",
    )(y3, y3, y3)
    return out.reshape(bsz * t, N_HEADS * HEAD_DIM)


SEG_PER_PAGE = 128 // NSA_CMP_STRIDE


def _pool_kernel(*refs, pp, g_kv, prefetch):
    refs = refs[prefetch:]
    x_refs = refs[:pp]
    w0_ref, w1_ref, u0_ref, u1_ref = refs[pp:]
    vps = NSA_CMP_STRIDE * g_kv // SUBLANES
    sub = lax.broadcasted_iota(jnp.int32, (SUBLANES, HEAD_DIM), 0)
    for s_i, x_ref in enumerate(x_refs):
        rows = slice(s_i * SEG_PER_PAGE, (s_i + 1) * SEG_PER_PAGE)
        tiles = [[jnp.zeros((SEG_PER_PAGE, HEAD_DIM), F32) for _ in range(g_kv)] for _ in range(2)]
        for n in range(SEG_PER_PAGE):
            accs = [jnp.zeros((SUBLANES, HEAD_DIM), F32), jnp.zeros((SUBLANES, HEAD_DIM), F32)]
            for v in range(vps):
                xv = x_ref[(n * vps + v) * SUBLANES:(n * vps + v + 1) * SUBLANES, :]
                accs[0] = accs[0] + xv * w0_ref[v]
                accs[1] = accs[1] + xv * w1_ref[v]
            for half in range(2):
                acc = accs[half]
                sh = g_kv
                while sh < SUBLANES:
                    acc = acc + pltpu.roll(acc, sh, 0)
                    sh *= 2
                for g in range(g_kv):
                    k = (n - g) % SUBLANES
                    moved = pltpu.roll(acc, k, 0) if k else acc
                    tiles[half][g] = jnp.where(sub == n, moved, tiles[half][g])
        for g in range(g_kv):
            u0_ref[rows, g * HEAD_DIM:(g + 1) * HEAD_DIM] = tiles[0][g]
            u1_ref[rows, g * HEAD_DIM:(g + 1) * HEAD_DIM] = tiles[1][g]


def _pool_weights(a, g_kv):
    ppv = SUBLANES // g_kv
    a_r = a.reshape(NSA_CMP_LEN // NSA_CMP_STRIDE, NSA_CMP_STRIDE // ppv, ppv, 1, HEAD_DIM)
    a_r = jnp.broadcast_to(a_r, a_r.shape[:3] + (g_kv, HEAD_DIM))
    a_r = a_r.reshape(2, NSA_CMP_STRIDE // ppv, SUBLANES, HEAD_DIM)
    return a_r[0], a_r[1]


def pool_prompt(x, a, pp):
    bsz, t, g_kv, _ = x.shape
    gw = g_kv * HEAD_DIM
    n_pages = t // 128
    xr = x.reshape(bsz, t * g_kv, HEAD_DIM)
    a0, a1 = _pool_weights(a, g_kv)
    in_specs = [pl.BlockSpec((None, 128 * g_kv, HEAD_DIM), lambda b, p, s=s: (b, p * pp + s, 0))
                for s in range(pp)]
    in_specs += [pl.BlockSpec(a0.shape, lambda b, p: (0, 0, 0))] * 2
    seg = pp * SEG_PER_PAGE
    return pl.pallas_call(
        functools.partial(_pool_kernel, pp=pp, g_kv=g_kv, prefetch=0),
        grid=(bsz, n_pages // pp),
        in_specs=in_specs,
        out_specs=[pl.BlockSpec((None, seg, gw), lambda b, p: (b, p, 0))] * 2,
        out_shape=[jax.ShapeDtypeStruct((bsz, t // NSA_CMP_STRIDE, gw), F32)] * 2,
        compiler_params=_cparams(("parallel", "arbitrary")),
        name="pool_prompt",
    )(*([xr] * pp), a0, a1)


def pool_paged(pool, page_table, a, pp):
    db, n_pages = page_table.shape
    n_pool, page, g_kv, _ = pool.shape
    assert page == 128
    gw = g_kv * HEAD_DIM
    pr = pool.reshape(n_pool, page * g_kv, HEAD_DIM)
    a0, a1 = _pool_weights(a, g_kv)
    in_specs = [pl.BlockSpec((None, page * g_kv, HEAD_DIM), lambda b, p, pt, s=s: (pt[b, p * pp + s], 0, 0))
                for s in range(pp)]
    in_specs += [pl.BlockSpec(a0.shape, lambda b, p, pt: (0, 0, 0))] * 2
    seg = pp * SEG_PER_PAGE
    grid_spec = pltpu.PrefetchScalarGridSpec(
        num_scalar_prefetch=1,
        grid=(db, n_pages // pp),
        in_specs=in_specs,
        out_specs=[pl.BlockSpec((None, seg, gw), lambda b, p, pt: (b, p, 0))] * 2,
    )
    return pl.pallas_call(
        functools.partial(_pool_kernel, pp=pp, g_kv=g_kv, prefetch=1),
        grid_spec=grid_spec,
        out_shape=[jax.ShapeDtypeStruct((db, n_pages * SEG_PER_PAGE, gw), F32)] * 2,
        compiler_params=_cparams(("parallel", "arbitrary")),
        name="pool_paged",
    )(page_table, *([pr] * pp), a0, a1)


def _cmp_mlp_kernel(u0_ref, u1_ref, a_ref, pe_ref, w1_ref, w2_ref, o_ref):
    n_seg = u0_ref.shape[0]
    c = jnp.sum(a_ref[...] * pe_ref[...], axis=0, keepdims=True)
    w1 = w1_ref[...].astype(BF16)
    w2 = w2_ref[...].astype(BF16)
    for g in range(u0_ref.shape[1] // HEAD_DIM):
        cols = slice(g * HEAD_DIM, (g + 1) * HEAD_DIM)
        u = u0_ref[:, cols] + pltpu.roll(u1_ref[:, cols], n_seg - 1, 0)
        u = u + c
        h = jax.nn.gelu(jnp.dot(u.astype(BF16), w1, preferred_element_type=F32))
        o_ref[:, cols] = jnp.dot(h.astype(BF16), w2, preferred_element_type=F32)


def cmp_mlp(u0, u1, a, pe, w1, w2):
    bx, n_seg, gw = u0.shape
    full = lambda arr: pl.BlockSpec(arr.shape, lambda b: (0,) * arr.ndim)
    rows = pl.BlockSpec((None, n_seg, gw), lambda b: (b, 0, 0))
    return pl.pallas_call(
        _cmp_mlp_kernel,
        grid=(bx,),
        in_specs=[rows, rows, full(a), full(pe), full(w1), full(w2)],
        out_specs=rows,
        out_shape=jax.ShapeDtypeStruct((bx, n_seg, gw), F32),
        compiler_params=_cparams(("parallel",)),
        name="cmp_mlp",
    )(u0, u1, a, pe, w1, w2)


def _cover_matrix(n_rows, n_cmp, n_cols, n_sel):
    cs = np.arange(n_rows) * NSA_CMP_STRIDE
    ss = np.arange(n_cols) * NSA_SEL_LEN
    m = (cs[:, None] < ss[None, :] + NSA_SEL_LEN) & (cs[:, None] + NSA_CMP_LEN > ss[None, :])
    m = m & (np.arange(n_rows)[:, None] < n_cmp) & (np.arange(n_cols)[None, :] < n_sel)
    return jnp.asarray(m, BF16)


def _masked_softmax(logits, mask):
    logits = jnp.where(mask, logits, NEG_INF)
    m = jnp.max(logits, axis=-1, keepdims=True)
    m = jnp.where(m == NEG_INF, 0.0, m)
    e = jnp.exp(logits - m)
    s = jnp.sum(e, axis=-1, keepdims=True)
    return e / jnp.where(s > 0, s, 1.0)


def _nsa_prompt_cmp_kernel(q_ref, kc_ref, vc_ref, cover_ref, ocmp_ref, sel_ref, *, r, n_cmp, n_sel, n_top):
    i = pl.program_id(2)
    qs = _stack_heads(q_ref[...], r).astype(BF16)
    nc = kc_ref.shape[0]

    lc = _nt(qs, kc_ref[...].astype(BF16)) * SCALE
    qp_c = i * Q_BLOCK + lax.broadcasted_iota(jnp.int32, (Q_BLOCK, nc), 0)
    n_c = lax.broadcasted_iota(jnp.int32, (Q_BLOCK, nc), 1)
    cmask = _tile_rows((n_c * NSA_CMP_STRIDE + NSA_CMP_LEN - 1 <= qp_c) & (n_c < n_cmp), r)
    pc = _masked_softmax(lc, cmask)
    ocmp = jnp.dot(pc.astype(BF16), vc_ref[...].astype(BF16), preferred_element_type=F32)
    for h in range(r):
        ocmp_ref[:, h * HEAD_DIM:(h + 1) * HEAD_DIM] = ocmp[h * Q_BLOCK:(h + 1) * Q_BLOCK]

    p_sum = pc[0:Q_BLOCK]
    for h in range(1, r):
        p_sum = p_sum + pc[h * Q_BLOCK:(h + 1) * Q_BLOCK]
    nj = -(-n_sel // SUBLANES) * SUBLANES
    cover_t = cover_ref[...]
    hi, mid, lo = _split3(p_sum)
    imp = (_nt(cover_t, hi) + _nt(cover_t, mid) + _nt(cover_t, lo))[:nj]
    qp = i * Q_BLOCK + lax.broadcasted_iota(jnp.int32, (nj, Q_BLOCK), 1)
    jj = lax.broadcasted_iota(jnp.int32, (nj, Q_BLOCK), 0)
    cur = qp // NSA_SEL_LEN
    vis = jj * NSA_SEL_LEN <= qp
    forced = (jj == 0) | (jj == cur) | (jj == cur - 1)
    score = jnp.where(vis, jnp.where(forced, SEL_FORCE, imp), -1.0)
    score = jnp.where(jj < n_sel, score, -2.0)
    rank = jnp.zeros((nj, Q_BLOCK), F32)
    for t in range(n_sel):
        st = score[t:t + 1, :]
        beats = (st > score) | ((st == score) & (jj > t))
        rank = rank + jnp.where(beats, 1.0, 0.0)
    sel_t = jnp.where((rank < n_top) & (score >= 0.0), 1.0, 0.0)
    if nj < LANES:
        sel_t = jnp.concatenate([sel_t, jnp.zeros((LANES - nj, Q_BLOCK), F32)], axis=0)
    sel_ref[...] = sel_t.T.astype(sel_ref.dtype)


def _nsa_prompt_slc_kernel(q_ref, sel_ref, ks_ref, vs_ref, oslc_ref, *, r, kc):
    i = pl.program_id(2)
    rq = r * Q_BLOCK
    qs = _stack_heads(q_ref[...], r).astype(BF16)
    sel = sel_ref[...]
    row = lax.broadcasted_iota(jnp.int32, (Q_BLOCK, kc), 0)
    col = lax.broadcasted_iota(jnp.int32, (Q_BLOCK, kc), 1)

    def body(kb, carry):
        m, l, acc = carry
        off = pl.multiple_of(kb * kc, kc)
        k = ks_ref[pl.ds(off, kc), :].astype(BF16)
        v = vs_ref[pl.ds(off, kc), :].astype(BF16)
        s = _nt(qs, k) * SCALE
        expand = jnp.where(row == (off + col) // NSA_SEL_LEN, 1.0, 0.0).astype(BF16)
        picked = jnp.dot(sel, expand, preferred_element_type=F32) > 0.5
        mask = _tile_rows(picked & ((off + col) <= (i * Q_BLOCK + row)), r)
        s = jnp.where(mask, s, NEG_INF)
        m_new = jnp.maximum(m, jnp.max(s, axis=-1, keepdims=True))
        m_safe = jnp.where(m_new == NEG_INF, 0.0, m_new)
        p = jnp.exp(s - m_safe)
        alpha = jnp.exp(m - m_safe)
        l = alpha * l + jnp.sum(p, axis=-1, keepdims=True)
        acc = alpha * acc + jnp.dot(p.astype(BF16), v, preferred_element_type=F32)
        return m_new, l, acc

    init = (jnp.full((rq, 1), NEG_INF, F32), jnp.zeros((rq, 1), F32), jnp.zeros((rq, HEAD_DIM), F32))
    n_it = ((i + 1) * Q_BLOCK + kc - 1) // kc
    _, l, acc = lax.fori_loop(0, n_it, body, init)
    oslc = acc / jnp.where(l > 0, l, 1.0)
    for h in range(r):
        oslc_ref[:, h * HEAD_DIM:(h + 1) * HEAD_DIM] = oslc[h * Q_BLOCK:(h + 1) * Q_BLOCK]


def nsa_prompt_global(y, kcmp, vcmp, bsz, t):
    r = N_HEADS // NSA_KV
    nb = t // Q_BLOCK
    rw = r * HEAD_DIM
    n_seg = t // NSA_CMP_STRIDE
    n_cmp = n_seg - NSA_CMP_LEN // NSA_CMP_STRIDE + 1
    n_sel = -(-t // NSA_SEL_LEN)
    assert n_sel <= LANES and t % NSA_SEL_LEN == 0
    n_top = min(NSA_TOP, n_sel)
    y3 = y.reshape(bsz, t, y.shape[1])
    cover = _cover_matrix(n_seg, n_cmp, LANES, n_sel).T
    ks_col = N_HEADS + 2 * NSA_KV
    vs_col = N_HEADS + 3 * NSA_KV
    o_shape = jax.ShapeDtypeStruct((bsz, t, N_HEADS * HEAD_DIM), F32)
    q_spec = pl.BlockSpec((None, Q_BLOCK, rw), lambda b, g, i: (b, i, g))
    sel_spec = pl.BlockSpec((None, None, Q_BLOCK, LANES), lambda b, g, i: (b, g, i, 0))
    ocmp, sel = pl.pallas_call(
        functools.partial(_nsa_prompt_cmp_kernel, r=r, n_cmp=n_cmp, n_sel=n_sel, n_top=n_top),
        grid=(bsz, NSA_KV, nb),
        in_specs=[q_spec,
                  pl.BlockSpec((None, n_seg, HEAD_DIM), lambda b, g, i: (b, 0, g)),
                  pl.BlockSpec((None, n_seg, HEAD_DIM), lambda b, g, i: (b, 0, g)),
                  pl.BlockSpec(cover.shape, lambda b, g, i: (0, 0))],
        out_specs=[q_spec, sel_spec],
        out_shape=[o_shape, jax.ShapeDtypeStruct((bsz, NSA_KV, t, LANES), BF16)],
        compiler_params=_cparams(("parallel", "parallel", "arbitrary")),
        name="nsa_prompt_cmp",
    )(y3, kcmp, vcmp, cover)
    oslc = pl.pallas_call(
        functools.partial(_nsa_prompt_slc_kernel, r=r, kc=_pick(t, (512, 256, 128))),
        grid=(bsz, NSA_KV, nb),
        in_specs=[q_spec, sel_spec,
                  pl.BlockSpec((None, t, HEAD_DIM), lambda b, g, i: (b, 0, ks_col + g)),
                  pl.BlockSpec((None, t, HEAD_DIM), lambda b, g, i: (b, 0, vs_col + g))],
        out_specs=q_spec,
        out_shape=o_shape,
        compiler_params=_cparams(("parallel", "parallel", "arbitrary")),
        name="nsa_prompt_slc",
    )(y3, sel, y3, y3)
    return ocmp.reshape(bsz * t, -1), oslc.reshape(bsz * t, -1)


def _combine_kernel(gl_ref, oc_ref, os_ref, ow_ref, o_ref):
    gates = jax.nn.sigmoid(gl_ref[...])
    for h in range(N_HEADS):
        sl = slice(h * HEAD_DIM, (h + 1) * HEAD_DIM)
        acc = gates[:, 3 * h:3 * h + 1] * oc_ref[:, sl]
        acc = acc + gates[:, 3 * h + 1:3 * h + 2] * os_ref[:, sl]
        acc = acc + gates[:, 3 * h + 2:3 * h + 3] * ow_ref[:, sl]
        o_ref[:, sl] = acc.astype(o_ref.dtype)


def nsa_combine(gate_logits, o_cmp, o_slc, o_win, tm):
    m, d = o_cmp.shape
    row = lambda w: pl.BlockSpec((tm, w), lambda i: (i, 0))
    return pl.pallas_call(
        _combine_kernel,
        grid=(m // tm,),
        in_specs=[row(gate_logits.shape[1]), row(d), row(d), row(d)],
        out_specs=row(d),
        out_shape=jax.ShapeDtypeStruct((m, d), BF16),
        compiler_params=_cparams(("parallel",)),
        name="nsa_combine",
    )(gate_logits, o_cmp, o_slc, o_win)


def _block_diag_q(q, g_kv):
    db = q.shape[0]
    r = N_HEADS // g_kv
    qh = q.reshape(db, N_HEADS, 1, HEAD_DIM)
    onehot = (jnp.arange(N_HEADS)[:, None] // r == jnp.arange(g_kv)[None, :]).astype(q.dtype)
    return (qh * onehot[None, :, :, None]).reshape(db, N_HEADS, g_kv * HEAD_DIM).astype(BF16)


def _diag_extract(o_all, g_kv):
    r = N_HEADS // g_kv
    hrow = lax.broadcasted_iota(jnp.int32, (N_HEADS, HEAD_DIM), 0)
    out = jnp.zeros((N_HEADS, HEAD_DIM), F32)
    for g in range(g_kv):
        out = out + jnp.where(hrow // r == g, o_all[:, g * HEAD_DIM:(g + 1) * HEAD_DIM], 0.0)
    return out


def _bf16_round(x):
    return x.astype(BF16).astype(F32)


def _group_rows(x_ref, g_kv):
    r = N_HEADS // g_kv
    hrow = lax.broadcasted_iota(jnp.int32, (N_HEADS, HEAD_DIM), 0)
    out = jnp.zeros((N_HEADS, HEAD_DIM), F32)
    for g in range(g_kv):
        out = out + jnp.where(hrow // r == g, x_ref[g:g + 1, :], 0.0)
    return out


def _win_sample_kernel(*refs, window, past_len, g_kv, has_sink):
    if has_sink:
        sink_ref, q_ref, kb_ref, vb_ref, kn_ref, vn_ref, o_ref = refs
    else:
        q_ref, kb_ref, vb_ref, kn_ref, vn_ref, o_ref = refs
    rows = kb_ref.shape[0]
    wb = rows // g_kv
    r = N_HEADS // g_kv
    q = q_ref[...]
    z = _nt(q, kb_ref[...].astype(BF16)) * SCALE
    zn = jnp.sum(q.astype(F32) * _bf16_round(_group_rows(kn_ref, g_kv)), axis=-1, keepdims=True) * SCALE
    hrow = lax.broadcasted_iota(jnp.int32, (N_HEADS, rows), 0)
    lane = lax.broadcasted_iota(jnp.int32, (N_HEADS, rows), 1)
    k_pos = past_len - wb + lane // g_kv
    rel = past_len - k_pos
    mask = (lane % g_kv == hrow // r) & (rel >= 0) & (rel <= window) & (k_pos >= 0)
    z = jnp.where(mask, z, NEG_INF)
    m = jnp.maximum(jnp.max(z, axis=-1, keepdims=True), zn)
    if has_sink:
        m = jnp.maximum(m, sink_ref[...])
    p = jnp.exp(z - m)
    pn = jnp.exp(zn - m)
    den = jnp.sum(p, axis=-1, keepdims=True) + pn
    if has_sink:
        den = den + jnp.exp(sink_ref[...] - m)
    o = jnp.dot((p / den).astype(BF16), vb_ref[...].astype(BF16), preferred_element_type=F32)
    o = o + _bf16_round(pn / den) * _bf16_round(_group_rows(vn_ref, g_kv))
    o_ref[...] = o.astype(o_ref.dtype)


def window_sample_attn(q, k_new, v_new, k_buf, v_buf, past_len, window, sink, out_dtype):
    db, wb, g_kv, _ = k_buf.shape
    rows = wb * g_kv
    in_specs = [pl.BlockSpec((None, N_HEADS, HEAD_DIM), lambda b: (b, 0, 0)),
                pl.BlockSpec((None, rows, HEAD_DIM), lambda b: (b, 0, 0)),
                pl.BlockSpec((None, rows, HEAD_DIM), lambda b: (b, 0, 0)),
                pl.BlockSpec((None, g_kv, HEAD_DIM), lambda b: (b, 0, 0)),
                pl.BlockSpec((None, g_kv, HEAD_DIM), lambda b: (b, 0, 0))]
    args = [q.reshape(db, N_HEADS, HEAD_DIM).astype(BF16),
            k_buf.reshape(db, rows, HEAD_DIM), v_buf.reshape(db, rows, HEAD_DIM),
            k_new.reshape(db, g_kv, HEAD_DIM), v_new.reshape(db, g_kv, HEAD_DIM)]
    if sink is not None:
        in_specs = [pl.BlockSpec((N_HEADS, 1), lambda b: (0, 0))] + in_specs
        args = [sink.reshape(N_HEADS, 1)] + args
    out = pl.pallas_call(
        functools.partial(_win_sample_kernel, window=window, past_len=past_len, g_kv=g_kv,
                          has_sink=sink is not None),
        grid=(db,),
        in_specs=in_specs,
        out_specs=pl.BlockSpec((None, N_HEADS, HEAD_DIM), lambda b: (b, 0, 0)),
        out_shape=jax.ShapeDtypeStruct((db, N_HEADS, HEAD_DIM), out_dtype),
        compiler_params=_cparams(("parallel",)),
        name="window_sample",
    )(*args)
    return out.reshape(db, N_HEADS * HEAD_DIM)


def _sb_sample_kernel(*refs, pp, n_pages, first_rank, q_pos, g_kv):
    refs = refs[1:]
    q_ref, cs0_ref, acc0_ref = refs[:3]
    k_refs = refs[3:3 + pp]
    v_refs = refs[3 + pp:3 + 2 * pp]
    cs_out_ref, acc_out_ref, cs_ref, acc_ref = refs[3 + 2 * pp:]
    p = pl.program_id(1)
    r = N_HEADS // g_kv
    n_ch = g_kv
    cp = LANES // g_kv

    @pl.when(p == 0)
    def _():
        cs_ref[...] = cs0_ref[:, 0:1]
        acc_ref[...] = acc0_ref[...]

    q = q_ref[...]
    row = lax.broadcasted_iota(jnp.int32, (LANES, LANES), 0)
    col = lax.broadcasted_iota(jnp.int32, (LANES, LANES), 1)
    later_sel = (row > col).astype(BF16)
    n_rank = pp * n_ch
    chunks = [(s, c) for s in range(pp) for c in reversed(range(n_ch))]
    srow = lax.broadcasted_iota(jnp.int32, (n_rank * N_HEADS, LANES), 0)
    slane = lax.broadcasted_iota(jnp.int32, (n_rank * N_HEADS, LANES), 1)
    own = (slane % g_kv) == ((srow % N_HEADS) // r)
    rank = srow // N_HEADS
    k_pos = ((n_pages - 1 - first_rank - p * pp) * n_ch - rank) * cp + slane // g_kv
    causal = own & (k_pos < q_pos)
    z_pages = [_nt(q, k_refs[s][...].astype(BF16)) for s in range(pp)]
    z = jnp.concatenate([z_pages[s][:, c * LANES:(c + 1) * LANES] for s, c in chunks],
                        axis=0) * SCALE
    sp = _softplus_neg_abs(z)
    log_beta = jnp.minimum(z, 0.0) - sp
    log_keep = jnp.where(causal, jnp.minimum(-z, 0.0) - sp, 0.0)
    rs = jnp.sum(log_keep, axis=-1, keepdims=True)
    cs = cs_ref[...]
    carries = []
    for ci in range(n_rank):
        carries.append(cs)
        cs = cs + rs[ci * N_HEADS:(ci + 1) * N_HEADS]
    later = _dot2(log_keep, later_sel) + jnp.concatenate(carries, axis=0)
    a = jnp.where(causal, jnp.exp(log_beta + later), 0.0).astype(BF16)
    acc = acc_ref[...]
    for s in range(pp):
        ranks = [s * n_ch + (n_ch - 1 - c) for c in range(n_ch)]
        a_page = jnp.concatenate([a[k * N_HEADS:(k + 1) * N_HEADS] for k in ranks], axis=1)
        acc = acc + jnp.dot(a_page, v_refs[s][...].astype(BF16), preferred_element_type=F32)
    cs_ref[...] = cs
    acc_ref[...] = acc

    @pl.when(p == pl.num_programs(1) - 1)
    def _():
        cs_out_ref[...] = jnp.broadcast_to(cs, cs_out_ref.shape)
        acc_out_ref[...] = acc


def _sb_sample_sweep(q16, pk, pv, page_table, cs0, acc0, first_rank, n_steps, pp, g_kv):
    db, n_pages = page_table.shape
    rows = pk.shape[1]
    q_pos = n_pages * (rows // g_kv)
    page_spec = lambda s: pl.BlockSpec(
        (None, rows, HEAD_DIM),
        lambda b, p, pt, s=s: (pt[b, n_pages - 1 - first_rank - (p * pp + s)], 0, 0))
    state_spec = pl.BlockSpec((None, N_HEADS, HEAD_DIM), lambda b, p, pt: (b, 0, 0))
    grid_spec = pltpu.PrefetchScalarGridSpec(
        num_scalar_prefetch=1,
        grid=(db, n_steps),
        in_specs=[state_spec, state_spec, state_spec] + [page_spec(s) for s in range(pp)] * 2,
        out_specs=[state_spec, state_spec],
        scratch_shapes=[pltpu.VMEM((N_HEADS, 1), F32), pltpu.VMEM((N_HEADS, HEAD_DIM), F32)],
    )
    state = jax.ShapeDtypeStruct((db, N_HEADS, HEAD_DIM), F32)
    return pl.pallas_call(
        functools.partial(_sb_sample_kernel, pp=pp, n_pages=n_pages, first_rank=first_rank, q_pos=q_pos,
                          g_kv=g_kv),
        grid_spec=grid_spec,
        out_shape=[state, state],
        compiler_params=_cparams(("parallel", "arbitrary")),
        name="sb_sample",
    )(page_table, q16, cs0, acc0, *([pk] * pp), *([pv] * pp))


def sb_sample_attn(q, pool_k, pool_v, page_table, pp, out_dtype):
    db, n_pages = page_table.shape
    n_pool, page, g_kv, _ = pool_k.shape
    assert page == 128 and n_pages % pp == 0
    rows = page * g_kv
    pk = pool_k.reshape(n_pool, rows, HEAD_DIM)
    pv = pool_v.reshape(n_pool, rows, HEAD_DIM)
    q16 = q.reshape(db, N_HEADS, HEAD_DIM).astype(BF16)
    zero = jnp.zeros((db, N_HEADS, HEAD_DIM), F32)
    cs, acc = _sb_sample_sweep(q16, pk, pv, page_table, zero, zero, 0, 1, pp, g_kv)
    if n_pages > pp:
        rest = lambda st: _sb_sample_sweep(q16, pk, pv, page_table, st[0], st[1], pp, n_pages // pp - 1,
                                           pp, g_kv)[1]
        acc = lax.cond(jnp.max(cs) > SB_CUTOFF, rest, lambda st: st[1], (cs, acc))
    return acc.astype(out_dtype).reshape(db, N_HEADS * HEAD_DIM)


def _nsa_sample_cmp_kernel(q_ref, kc_ref, vc_ref, cover_ref, ocmp_ref, idx_ref, val_ref,
                           *, n_cmp, n_sel, n_top, q_pos):
    q = q_ref[...]
    nc = kc_ref.shape[0]
    ns = cover_ref.shape[1]
    r = N_HEADS // NSA_KV
    lc = _nt(q, kc_ref[...].astype(BF16)) * SCALE
    n_c = lax.broadcasted_iota(jnp.int32, (N_HEADS, nc), 1)
    cmask = (n_c * NSA_CMP_STRIDE + NSA_CMP_LEN - 1 <= q_pos) & (n_c < n_cmp)
    pc = _masked_softmax(lc, cmask)
    o_all = jnp.dot(pc.astype(BF16), vc_ref[...].astype(BF16), preferred_element_type=F32)
    ocmp_ref[...] = _diag_extract(o_all, NSA_KV)

    grow = lax.broadcasted_iota(jnp.int32, (SUBLANES, N_HEADS), 0)
    hcol = lax.broadcasted_iota(jnp.int32, (SUBLANES, N_HEADS), 1)
    group_sel = (hcol // r == grow).astype(BF16)
    p_sum = _dot3r(group_sel, pc)
    imp = _dot3(p_sum, cover_ref[...])
    jj = lax.broadcasted_iota(jnp.int32, (SUBLANES, ns), 1)
    cur = q_pos // NSA_SEL_LEN
    vis = jj * NSA_SEL_LEN <= q_pos
    forced = (jj == 0) | (jj == cur) | (jj == cur - 1)
    score = jnp.where(vis, jnp.where(forced, SEL_FORCE, imp), -1.0)
    score = jnp.where(jj < n_sel, score, -2.0)
    jf = jj.astype(F32)
    tl = lax.broadcasted_iota(jnp.int32, (SUBLANES, LANES), 1)
    idx_out = jnp.zeros((SUBLANES, LANES), F32)
    val_out = jnp.full((SUBLANES, LANES), -1.0, F32)
    for t in range(n_top):
        mx = jnp.max(score, axis=-1, keepdims=True)
        first = jnp.min(jnp.where(score == mx, jf, float(ns)), axis=-1, keepdims=True)
        idx_out = jnp.where(tl == t, first, idx_out)
        val_out = jnp.where(tl == t, mx, val_out)
        score = jnp.where(jf == first, NEG_INF, score)
    idx_ref[...] = idx_out.astype(jnp.int32)
    val_ref[...] = val_out


def nsa_sample_cmp(q, kcmp, vcmp, n_cmp, n_sel, q_pos):
    db, nc, gw = kcmp.shape
    ns = -(-n_sel // LANES) * LANES
    n_top = min(NSA_TOP, n_sel)
    cover = _cover_matrix(nc, n_cmp, ns, n_sel)
    qbd = _block_diag_q(q, NSA_KV)
    return pl.pallas_call(
        functools.partial(_nsa_sample_cmp_kernel, n_cmp=n_cmp, n_sel=n_sel, n_top=n_top, q_pos=q_pos),
        grid=(db,),
        in_specs=[pl.BlockSpec((None, N_HEADS, gw), lambda b: (b, 0, 0)),
                  pl.BlockSpec((None, nc, gw), lambda b: (b, 0, 0)),
                  pl.BlockSpec((None, nc, gw), lambda b: (b, 0, 0)),
                  pl.BlockSpec(cover.shape, lambda b: (0, 0))],
        out_specs=[pl.BlockSpec((None, N_HEADS, HEAD_DIM), lambda b: (b, 0, 0)),
                   pl.BlockSpec((None, SUBLANES, LANES), lambda b: (b, 0, 0)),
                   pl.BlockSpec((None, SUBLANES, LANES), lambda b: (b, 0, 0))],
        out_shape=[jax.ShapeDtypeStruct((db, N_HEADS, HEAD_DIM), F32),
                   jax.ShapeDtypeStruct((db, SUBLANES, LANES), jnp.int32),
                   jax.ShapeDtypeStruct((db, SUBLANES, LANES), F32)],
        compiler_params=_cparams(("parallel",)),
        name="nsa_sample_cmp",
    )(qbd, kcmp, vcmp, cover)


def _nsa_sample_slc_kernel(*refs, n_top, n_sel, q_pos):
    idx_ref, ok_ref, pt_ref = refs[:3]
    refs = refs[3:]
    q_ref = refs[0]
    k_refs = refs[1:1 + n_top]
    v_refs = refs[1 + n_top:1 + 2 * n_top]
    kn_ref, vn_ref, o_ref = refs[1 + 2 * n_top:]
    b = pl.program_id(0)
    g = pl.program_id(1)
    r = N_HEADS // NSA_KV
    rows = NSA_SEL_LEN * NSA_KV
    base = (b * NSA_KV + g) * n_top

    @pl.when(g == 0)
    def _():
        o_ref[...] = jnp.zeros_like(o_ref)

    q = q_ref[...]
    lane = lax.broadcasted_iota(jnp.int32, (N_HEADS, rows), 1)
    lane_pos = lane // NSA_KV
    lane_own = (lane % NSA_KV) == g
    zs = []
    new_ok = jnp.int32(0)
    for t in range(n_top):
        j = idx_ref[base + t]
        ok = ok_ref[base + t]
        in_pool = j < n_sel - 1
        z = _nt(q, k_refs[t][...].astype(BF16)) * SCALE
        last_pos = jnp.where((ok > 0) & in_pool, q_pos, -1)
        zs.append(jnp.where(lane_own & ((j * NSA_SEL_LEN + lane_pos) <= last_pos), z, NEG_INF))
        new_ok = new_ok | jnp.where((ok > 0) & jnp.logical_not(in_pool), 1, 0)
    new_vis = (new_ok > 0) & ((n_sel - 1) * NSA_SEL_LEN <= q_pos)
    zn = jnp.sum(q.astype(F32) * _bf16_round(kn_ref[...]), axis=-1, keepdims=True) * SCALE
    zn = zn + jnp.where(new_vis, 0.0, NEG_INF)
    m = zn
    for z in zs:
        m = jnp.maximum(m, jnp.max(z, axis=-1, keepdims=True))
    m = jnp.where(m == NEG_INF, 0.0, m)
    pn = jnp.exp(zn - m)
    den = pn
    ps = []
    for z in zs:
        e = jnp.exp(z - m)
        ps.append(e)
        den = den + jnp.sum(e, axis=-1, keepdims=True)
    den = jnp.where(den > 0, den, 1.0)
    o = _bf16_round(pn / den) * _bf16_round(vn_ref[...])
    for t in range(n_top):
        o = o + jnp.dot((ps[t] / den).astype(BF16), v_refs[t][...].astype(BF16),
                        preferred_element_type=F32)
    hrow = lax.broadcasted_iota(jnp.int32, (N_HEADS, HEAD_DIM), 0)
    o_ref[...] = jnp.where(hrow // r == g, o, o_ref[...])


def nsa_sample_slc(q, top_idx, top_val, pool_k, pool_v, page_table, k_new, v_new, n_sel, q_pos):
    db, n_pages = page_table.shape
    n_top = min(NSA_TOP, n_sel)
    n_pool, page, g_kv, _ = pool_k.shape
    assert g_kv == NSA_KV
    half = page // NSA_SEL_LEN
    rows = NSA_SEL_LEN * g_kv
    pk = pool_k.reshape(n_pool * half, rows, HEAD_DIM)
    pv = pool_v.reshape(n_pool * half, rows, HEAD_DIM)
    idx = top_idx[:, :NSA_KV, :n_top].reshape(-1)
    ok = (top_val[:, :NSA_KV, :n_top] >= 0.0).astype(jnp.int32).reshape(-1)

    def blk_spec(t):
        def index_map(b, g, idx_ref, ok_ref, pt_ref):
            j = jnp.minimum(idx_ref[(b * NSA_KV + g) * n_top + t], n_sel - 2)
            return (pt_ref[b, j // half] * half + j % half, 0, 0)
        return pl.BlockSpec((None, rows, HEAD_DIM), index_map)

    row_spec = pl.BlockSpec((None, 1, HEAD_DIM), lambda b, g, *_: (b * NSA_KV + g, 0, 0))
    grid_spec = pltpu.PrefetchScalarGridSpec(
        num_scalar_prefetch=3,
        grid=(db, NSA_KV),
        in_specs=([pl.BlockSpec((None, N_HEADS, HEAD_DIM), lambda b, g, *_: (b, 0, 0))]
                  + [blk_spec(t) for t in range(n_top)] * 2 + [row_spec, row_spec]),
        out_specs=pl.BlockSpec((None, N_HEADS, HEAD_DIM), lambda b, g, *_: (b, 0, 0)),
    )
    return pl.pallas_call(
        functools.partial(_nsa_sample_slc_kernel, n_top=n_top, n_sel=n_sel, q_pos=q_pos),
        grid_spec=grid_spec,
        out_shape=jax.ShapeDtypeStruct((db, N_HEADS, HEAD_DIM), F32),
        compiler_params=_cparams(("parallel", "arbitrary")),
        name="nsa_sample_slc",
    )(idx, ok, page_table, q.reshape(db, N_HEADS, HEAD_DIM).astype(BF16),
      *([pk] * n_top), *([pv] * n_top),
      k_new.reshape(db * NSA_KV, 1, HEAD_DIM), v_new.reshape(db * NSA_KV, 1, HEAD_DIM))


def _rope_tables(pos):
    half = HEAD_DIM // 2
    inv = ROPE_THETA ** (-jnp.arange(half, dtype=F32) / half)
    ang = pos.astype(F32)[:, None] * inv[None, :]
    cos, sin = jnp.cos(ang), jnp.sin(ang)
    return jnp.concatenate([cos, cos], -1), jnp.concatenate([-sin, sin], -1)


def _pick(n, cands):
    for c in cands:
        if n % c == 0:
            return c
    return n


def _res_ln_rows(h16, w16_all, widx, x32, g, b, scale):
    tm = _pick(x32.shape[0], (256, 128))
    return res_ln(h16, w16_all, widx, x32, g, b, scale, tm, _pick(tm, (128,)))


def _ffn_step(xp, xs, w_in_all, w_out16_all, which, g, b):
    m = xp[0].shape[0]
    f = w_out16_all.shape[-2]
    hp, hs = glu(xp[1], xs[1], w_in_all, which, _pick(m, (1024, 512, 256, 128)), _pick(f, (512, 256, 128)))
    return (_res_ln_rows(hp, w_out16_all, which, xp[0], g, b, 0.5),
            _res_ln_rows(hs, w_out16_all, which, xs[0], g, b, 0.5))


def _project(x16, w16_all, widx, flags, tables, tn):
    m = x16.shape[0]
    tm = _pick(min(m, tables[0].shape[0]), (1024, 512, 256, 128))
    return proj(x16, w16_all, widx, len(flags) * tn, jnp.asarray(flags, jnp.int32), tables[0], tables[1], tm, tn)


def kernel(x_prompt, x_sample, cache_swa_k, cache_swa_v, cache_nsa_kc, cache_nsa_vc, cache_nsa_ks,
           cache_nsa_vs, cache_nsa_kw, cache_nsa_vw, cache_sb_k, cache_sb_v, page_table, ln_g, ln_b,
           ffn_w_in, ffn_w_out, swa_w_in, swa_sink, swa_w_o, nsa_w_in, nsa_cmp_a, nsa_cmp_pe,
           nsa_cmp_w1, nsa_cmp_w2, nsa_w_o, sb_w_in, sb_w_o):
    bsz, t, d = x_prompt.shape
    db, tn_new, _ = x_sample.shape
    assert tn_new == 1
    depth = ffn_w_in.shape[0]
    n_pages = page_table.shape[1]
    page = cache_nsa_kc.shape[2]
    past_len = n_pages * page
    hq = N_HEADS * HEAD_DIM

    tab_p = _rope_tables(jnp.arange(t, dtype=jnp.int32))
    tab_s = _rope_tables(jnp.full((db,), past_len, jnp.int32))

    xp32 = x_prompt.reshape(bsz * t, d)
    xs32 = x_sample.reshape(db, d)
    xp16, xs16 = xp32.astype(BF16), xs32.astype(BF16)

    ffn_w_out16 = ffn_w_out.astype(BF16)
    swa_w_in16, swa_w_o16 = swa_w_in.astype(BF16), swa_w_o.astype(BF16)
    nsa_w_in16, nsa_w_o16 = nsa_w_in.astype(BF16), nsa_w_o.astype(BF16)
    sb_w_in16, sb_w_o16 = sb_w_in.astype(BF16), sb_w_o.astype(BF16)

    outs = {k: [] for k in ("swa_kp", "swa_vp", "swa_ks", "swa_vs", "nkc_p", "nvc_p", "nks_p", "nvs_p",
                            "nkw_p", "nvw_p", "nkc_s", "nvc_s", "nks_s", "nvs_s", "nkw_s", "nvw_s",
                            "sbk_p", "sbv_p", "sbk_s", "sbv_s")}

    for l in range(depth):
        kind, j = l % 3, l // 3
        (xp32, xp16), (xs32, xs16) = _ffn_step((xp32, xp16), (xs32, xs16), ffn_w_in, ffn_w_out16, (l, 0),
                                               ln_g[l, 0], ln_b[l, 0])

        if kind == 0:
            kd = SWA_KV * HEAD_DIM
            flags = [1] * ((hq + kd) // 512) + [0] * (kd // 512)
            yp = _project(xp16, swa_w_in16, (j,), flags, tab_p, 512)
            ys = _project(xs16, swa_w_in16, (j,), flags, tab_s, 512)
            op = window_prompt_attn(yp, bsz, t, SWA_KV, 0, N_HEADS, N_HEADS + SWA_KV, SWA_WINDOW,
                                    swa_sink[j], BF16)
            keep = min(SWA_WINDOW, t)
            yp3 = yp.reshape(bsz, t, -1)
            outs["swa_kp"].append(yp3[:, t - keep:, hq:hq + kd].reshape(bsz, keep, SWA_KV, HEAD_DIM))
            outs["swa_vp"].append(yp3[:, t - keep:, hq + kd:].reshape(bsz, keep, SWA_KV, HEAD_DIM))
            k_new, v_new = ys[:, hq:hq + kd], ys[:, hq + kd:]
            k_buf, v_buf = cache_swa_k[j], cache_swa_v[j]
            osm = window_sample_attn(ys[:, :hq], k_new, v_new, k_buf, v_buf, past_len, SWA_WINDOW,
                                     swa_sink[j], BF16)
            wb = k_buf.shape[1]
            keep_s = min(SWA_WINDOW, past_len + 1)
            new4 = lambda x: x.reshape(db, 1, SWA_KV, HEAD_DIM)
            outs["swa_ks"].append(jnp.concatenate([k_buf, new4(k_new)], axis=1)[:, wb + 1 - keep_s:])
            outs["swa_vs"].append(jnp.concatenate([v_buf, new4(v_new)], axis=1)[:, wb + 1 - keep_s:])
            w_o16 = swa_w_o16
        elif kind == 1:
            kd = NSA_KV * HEAD_DIM
            main = hq + 6 * kd
            w_gate = jnp.pad(nsa_w_in16[j][:, main:], ((0, 0), (0, LANES - 3 * N_HEADS)))
            flags = [1] * (hq // 512) + [1, 0, 1, 0, 1, 0]
            a_k, a_v = nsa_cmp_a[j, 0], nsa_cmp_a[j, 1]
            mlp_k = (a_k, nsa_cmp_pe[j, 0], nsa_cmp_w1[j, 0], nsa_cmp_w2[j, 0])
            mlp_v = (a_v, nsa_cmp_pe[j, 1], nsa_cmp_w1[j, 1], nsa_cmp_w2[j, 1])
            yp = _project(xp16, nsa_w_in16, (j,), flags, tab_p, 512)
            gp = _project(xp16, w_gate, (), [0], tab_p, LANES)
            pp = _pick(t // 128, (8, 4, 2, 1))
            yp3 = yp.reshape(bsz, t, -1)
            seg = lambda c: yp3[:, :, hq + c * kd:hq + (c + 1) * kd].reshape(bsz, t, NSA_KV, HEAD_DIM)
            kc_p, vc_p = seg(0), seg(1)
            kcmp = cmp_mlp(*pool_prompt(kc_p, a_k, pp), *mlp_k)
            vcmp = cmp_mlp(*pool_prompt(vc_p, a_v, pp), *mlp_v)
            oc, osl = nsa_prompt_global(yp, kcmp, vcmp, bsz, t)
            ow = window_prompt_attn(yp, bsz, t, NSA_KV, 0, N_HEADS + 4 * NSA_KV, N_HEADS + 5 * NSA_KV,
                                    NSA_WINDOW, None, F32)
            op = nsa_combine(gp, oc, osl, ow, _pick(bsz * t, (512, 256, 128)))
            keep = min(NSA_WINDOW, t)
            outs["nkc_p"].append(kc_p)
            outs["nvc_p"].append(vc_p)
            outs["nks_p"].append(seg(2))
            outs["nvs_p"].append(seg(3))
            outs["nkw_p"].append(seg(4)[:, t - keep:])
            outs["nvw_p"].append(seg(5)[:, t - keep:])
            ys = _project(xs16, nsa_w_in16, (j,), flags, tab_s, 512)
            gs = _project(xs16, w_gate, (), [0], tab_s, LANES)
            sseg = lambda c: ys[:, hq + c * kd:hq + (c + 1) * kd]
            assert page == 128
            total = past_len + 1
            n_seg = total // NSA_CMP_STRIDE
            assert n_seg == n_pages * SEG_PER_PAGE
            n_cmp = n_seg - NSA_CMP_LEN // NSA_CMP_STRIDE + 1
            n_sel = -(-total // NSA_SEL_LEN)
            pps = _pick(n_pages, (16, 8, 4, 2, 1))
            kcmp_s = cmp_mlp(*pool_paged(cache_nsa_kc[j], page_table, a_k, pps), *mlp_k)
            vcmp_s = cmp_mlp(*pool_paged(cache_nsa_vc[j], page_table, a_v, pps), *mlp_v)
            oc_s, top_idx, top_val = nsa_sample_cmp(ys[:, :hq], kcmp_s, vcmp_s, n_cmp, n_sel, past_len)
            osl_s = nsa_sample_slc(ys[:, :hq], top_idx, top_val, cache_nsa_ks[j], cache_nsa_vs[j],
                                   page_table, sseg(2), sseg(3), n_sel, past_len)
            kw_buf, vw_buf = cache_nsa_kw[j], cache_nsa_vw[j]
            ow_s = window_sample_attn(ys[:, :hq], sseg(4), sseg(5), kw_buf, vw_buf, past_len, NSA_WINDOW,
                                      None, F32)
            osm = nsa_combine(gs, oc_s.reshape(db, hq), osl_s.reshape(db, hq), ow_s, db)
            wb = kw_buf.shape[1]
            keep_s = min(NSA_WINDOW, past_len + 1)
            new4 = lambda c: sseg(c).reshape(db, 1, NSA_KV, HEAD_DIM)
            for name, c in (("nkc_s", 0), ("nvc_s", 1), ("nks_s", 2), ("nvs_s", 3)):
                outs[name].append(new4(c))
            outs["nkw_s"].append(jnp.concatenate([kw_buf, new4(4)], axis=1)[:, wb + 1 - keep_s:])
            outs["nvw_s"].append(jnp.concatenate([vw_buf, new4(5)], axis=1)[:, wb + 1 - keep_s:])
            w_o16 = nsa_w_o16
        else:
            kd = SB_KV * HEAD_DIM
            flags = [0] * (sb_w_in.shape[-1] // 512)
            yp = _project(xp16, sb_w_in16, (j,), flags, tab_p, 512)
            ys = _project(xs16, sb_w_in16, (j,), flags, tab_s, 512)
            op = sb_prompt_attn(yp, bsz, t, BF16)
            yp3 = yp.reshape(bsz, t, -1)
            outs["sbk_p"].append(yp3[:, :, hq:hq + kd].reshape(bsz, t, SB_KV, HEAD_DIM))
            outs["sbv_p"].append(yp3[:, :, hq + kd:].reshape(bsz, t, SB_KV, HEAD_DIM))
            osm = sb_sample_attn(ys[:, :hq], cache_sb_k[j], cache_sb_v[j], page_table,
                                 _pick(n_pages, (8, 4, 2, 1)), BF16)
            outs["sbk_s"].append(ys[:, hq:hq + kd].reshape(db, 1, SB_KV, HEAD_DIM))
            outs["sbv_s"].append(ys[:, hq + kd:].reshape(db, 1, SB_KV, HEAD_DIM))
            w_o16 = sb_w_o16

        xp32, xp16 = _res_ln_rows(op, w_o16, (j,), xp32, ln_g[l, 1], ln_b[l, 1], 1.0)
        xs32, xs16 = _res_ln_rows(osm, w_o16, (j,), xs32, ln_g[l, 1], ln_b[l, 1], 1.0)
        (xp32, xp16), (xs32, xs16) = _ffn_step((xp32, xp16), (xs32, xs16), ffn_w_in, ffn_w_out16, (l, 1),
                                               ln_g[l, 2], ln_b[l, 2])

    st = lambda name: jnp.stack(outs[name])
    return (xp32.reshape(bsz, t, d), xs32.reshape(db, 1, d),
            st("swa_kp"), st("swa_vp"),
            st("nkc_p"), st("nvc_p"), st("nks_p"), st("nvs_p"), st("nkw_p"), st("nvw_p"),
            st("sbk_p"), st("sbv_p"),
            st("swa_ks"), st("swa_vs"),
            st("nkc_s"), st("nvc_s"), st("nks_s"), st("nvs_s"), st("nkw_s"), st("nvw_s"),
            st("sbk_s"), st("sbv_s"))
```

```python
import functools

import numpy as np
import jax
import jax.numpy as jnp
from jax import lax
from jax.experimental import pallas as pl
from jax.experimental.pallas import tpu as pltpu

F32 = jnp.float32
BF16 = jnp.bfloat16

HEAD_DIM = 128
N_HEADS = 16
Q_BLOCK = 128
SWA_KV = 4
SWA_WINDOW = 128
NSA_KV = 4
NSA_CMP_LEN = 32
NSA_CMP_STRIDE = 16
NSA_SEL_LEN = 64
NSA_TOP = 16
NSA_WINDOW = 512
SEL_FORCE = 1e4
SB_KV = 8
ROPE_THETA = 10000.0
LN_EPS = 1e-5
DEPTH = 4
DN_ALPHA = (2 * DEPTH) ** 0.25
SCALE = HEAD_DIM ** -0.5
NEG_INF = float("-inf")

LANES = 128
SUBLANES = 8
VMEM_LIMIT = 56 * 1024 * 1024


def _cparams(sem):
    return pltpu.CompilerParams(dimension_semantics=sem, vmem_limit_bytes=VMEM_LIMIT)


def _nt(a, b):
    return lax.dot_general(a, b, (((1,), (1,)), ((), ())), preferred_element_type=F32)


def _split3(x):
    hi = x.astype(BF16)
    r = x - hi.astype(F32)
    mid = r.astype(BF16)
    lo = (r - mid.astype(F32)).astype(BF16)
    return hi, mid, lo


def _dot3(x, u):
    hi, mid, lo = _split3(x)
    return (jnp.dot(hi, u, preferred_element_type=F32)
            + jnp.dot(mid, u, preferred_element_type=F32)
            + jnp.dot(lo, u, preferred_element_type=F32))


def _dot2(x, u):
    hi = x.astype(BF16)
    mid = (x - hi.astype(F32)).astype(BF16)
    return jnp.dot(hi, u, preferred_element_type=F32) + jnp.dot(mid, u, preferred_element_type=F32)


def _dot3r(u, x):
    hi, mid, lo = _split3(x)
    return (jnp.dot(u, hi, preferred_element_type=F32)
            + jnp.dot(u, mid, preferred_element_type=F32)
            + jnp.dot(u, lo, preferred_element_type=F32))


def _stack_heads(q, r):
    return jnp.concatenate([q[:, h * HEAD_DIM:(h + 1) * HEAD_DIM] for h in range(r)], axis=0)


def _tile_rows(x, r):
    return jnp.concatenate([x] * r, axis=0) if r > 1 else x


def _silu_mul(g, u):
    return g * jax.nn.sigmoid(g) * u


def _glu_kernel(x_ref, xs_ref, wg_ref, wu_ref, o_ref, os_ref, wg16_ref, wu16_ref):
    @pl.when(pl.program_id(1) == 0)
    def _():
        wg16_ref[...] = wg_ref[...].astype(BF16)
        wu16_ref[...] = wu_ref[...].astype(BF16)
        xs = xs_ref[...]
        gs = jnp.dot(xs, wg16_ref[...], preferred_element_type=F32)
        us = jnp.dot(xs, wu16_ref[...], preferred_element_type=F32)
        os_ref[...] = _silu_mul(gs, us).astype(os_ref.dtype)

    x = x_ref[...]
    g = jnp.dot(x, wg16_ref[...], preferred_element_type=F32)
    u = jnp.dot(x, wu16_ref[...], preferred_element_type=F32)
    o_ref[...] = _silu_mul(g, u).astype(o_ref.dtype)


def glu(xb, xsb, w_in_all, which, tm, tn):
    m, k = xb.shape
    ms = xsb.shape[0]
    f = w_in_all.shape[-1] // 2
    nj = f // tn
    l0, l1 = which
    return pl.pallas_call(
        _glu_kernel,
        grid=(nj, m // tm),
        in_specs=[pl.BlockSpec((tm, k), lambda j, i: (i, 0)),
                  pl.BlockSpec((ms, k), lambda j, i: (0, 0)),
                  pl.BlockSpec((None, None, k, tn), lambda j, i: (l0, l1, 0, j)),
                  pl.BlockSpec((None, None, k, tn), lambda j, i: (l0, l1, 0, j + nj))],
        out_specs=[pl.BlockSpec((tm, tn), lambda j, i: (i, j)),
                   pl.BlockSpec((ms, tn), lambda j, i: (0, j))],
        out_shape=[jax.ShapeDtypeStruct((m, f), BF16), jax.ShapeDtypeStruct((ms, f), BF16)],
        scratch_shapes=[pltpu.VMEM((k, tn), BF16), pltpu.VMEM((k, tn), BF16)],
        compiler_params=_cparams(("parallel", "arbitrary")),
        name="glu",
    )(xb, xsb, w_in_all, w_in_all)


def _resln_kernel(h_ref, w_ref, x_ref, g_ref, b_ref, o32_ref, o16_ref, *, scale, sub):
    for u in range(h_ref.shape[0] // sub):
        rows = slice(u * sub, (u + 1) * sub)
        acc = jnp.dot(h_ref[rows, :], w_ref[...], preferred_element_type=F32)
        y = DN_ALPHA * x_ref[rows, :] + scale * acc
        mu = jnp.mean(y, axis=-1, keepdims=True)
        d = y - mu
        var = jnp.mean(d * d, axis=-1, keepdims=True)
        out = d * lax.rsqrt(var + LN_EPS) * g_ref[...] + b_ref[...]
        o32_ref[rows, :] = out
        o16_ref[rows, :] = out.astype(BF16)


def res_ln(hb, w_all, widx, x, g, b, scale, tm, sub):
    m, kdim = hb.shape
    d = w_all.shape[-1]
    lead = (None,) * len(widx)
    return pl.pallas_call(
        functools.partial(_resln_kernel, scale=scale, sub=sub),
        grid=(m // tm,),
        in_specs=[pl.BlockSpec((tm, kdim), lambda i: (i, 0)),
                  pl.BlockSpec(lead + (kdim, d), lambda i: tuple(widx) + (0, 0), pipeline_mode=pl.Buffered(1)),
                  pl.BlockSpec((tm, d), lambda i: (i, 0)),
                  pl.BlockSpec((1, d), lambda i: (0, 0)),
                  pl.BlockSpec((1, d), lambda i: (0, 0))],
        out_specs=[pl.BlockSpec((tm, d), lambda i: (i, 0)),
                   pl.BlockSpec((tm, d), lambda i: (i, 0))],
        out_shape=[jax.ShapeDtypeStruct((m, d), F32), jax.ShapeDtypeStruct((m, d), BF16)],
        compiler_params=_cparams(("parallel",)),
        name="res_ln",
    )(hb, w_all, x, g.reshape(1, d), b.reshape(1, d))


def _proj_kernel(flag_ref, x_ref, w_ref, c_ref, s_ref, o_ref, *, nh):
    j = pl.program_id(1)
    y = jnp.dot(x_ref[...], w_ref[...], preferred_element_type=F32)

    @pl.when(flag_ref[j] == 0)
    def _():
        o_ref[...] = y

    @pl.when(flag_ref[j] != 0)
    def _():
        c = c_ref[...]
        s = s_ref[...]
        for h in range(nh):
            yh = y[:, h * HEAD_DIM:(h + 1) * HEAD_DIM]
            o_ref[:, h * HEAD_DIM:(h + 1) * HEAD_DIM] = yh * c + pltpu.roll(yh, HEAD_DIM // 2, 1) * s


def proj(xb, w_all, widx, n, rope_flags, cos_t, sin_t, tm, tn):
    m, k = xb.shape
    lead = (None,) * len(widx)
    tbl_blocks = cos_t.shape[0] // tm
    grid_spec = pltpu.PrefetchScalarGridSpec(
        num_scalar_prefetch=1,
        grid=(m // tm, n // tn),
        in_specs=[pl.BlockSpec((tm, k), lambda i, j, f: (i, 0)),
                  pl.BlockSpec(lead + (k, tn), lambda i, j, f: tuple(widx) + (0, j)),
                  pl.BlockSpec((tm, HEAD_DIM), lambda i, j, f: (i % tbl_blocks, 0)),
                  pl.BlockSpec((tm, HEAD_DIM), lambda i, j, f: (i % tbl_blocks, 0))],
        out_specs=pl.BlockSpec((tm, tn), lambda i, j, f: (i, j)),
    )
    return pl.pallas_call(
        functools.partial(_proj_kernel, nh=tn // HEAD_DIM),
        grid_spec=grid_spec,
        out_shape=jax.ShapeDtypeStruct((m, n), F32),
        compiler_params=_cparams(("parallel", "arbitrary")),
        name="proj",
    )(rope_flags, xb, w_all, cos_t, sin_t)


def _win_kernel(*refs, window, r, qb, has_sink):
    if has_sink:
        sink_ref, q_ref, k_ref, v_ref, o_ref = refs
    else:
        q_ref, k_ref, v_ref, o_ref = refs
    g = pl.program_id(1)
    kl = window + Q_BLOCK
    if has_sink:
        s = jnp.concatenate([jnp.full((Q_BLOCK, 1), sink_ref[g * r + h], F32) for h in range(r)], axis=0)
    for u in range(qb):
        i = pl.program_id(2) * qb + u
        rows = slice(u * Q_BLOCK, (u + 1) * Q_BLOCK)
        start = pl.multiple_of(jnp.maximum(i * Q_BLOCK - window, 0), Q_BLOCK)
        kb = k_ref[pl.ds(start, kl), :].astype(BF16)
        vb = v_ref[pl.ds(start, kl), :].astype(BF16)
        qs = _stack_heads(q_ref[rows, :], r).astype(BF16)
        logits = _nt(qs, kb) * SCALE
        q_pos = i * Q_BLOCK + lax.broadcasted_iota(jnp.int32, (Q_BLOCK, kl), 0)
        k_pos = start + lax.broadcasted_iota(jnp.int32, (Q_BLOCK, kl), 1)
        rel = q_pos - k_pos
        mask = _tile_rows((rel >= 0) & (rel <= window), r)
        logits = jnp.where(mask, logits, NEG_INF)
        m = jnp.max(logits, axis=-1, keepdims=True)
        if has_sink:
            m = jnp.maximum(m, s)
        p = jnp.exp(logits - m)
        den = jnp.sum(p, axis=-1, keepdims=True)
        if has_sink:
            den = den + jnp.exp(s - m)
        o = jnp.dot((p / den).astype(BF16), vb, preferred_element_type=F32)
        for h in range(r):
            o_ref[rows, h * HEAD_DIM:(h + 1) * HEAD_DIM] = o[h * Q_BLOCK:(h + 1) * Q_BLOCK].astype(o_ref.dtype)


def window_prompt_attn(y, bsz, t, g_kv, q_col, k_col, v_col, window, sink, out_dtype):
    r = N_HEADS // g_kv
    assert window + Q_BLOCK <= t
    rw = r * HEAD_DIM
    qb = _pick(t // Q_BLOCK, (8, 4, 2, 1))
    nb = t // (Q_BLOCK * qb)
    y3 = y.reshape(bsz, t, y.shape[1])
    in_specs = [pl.BlockSpec((None, Q_BLOCK * qb, rw), lambda b, g, i: (b, i, q_col // r + g)),
                pl.BlockSpec((None, t, HEAD_DIM), lambda b, g, i: (b, 0, k_col + g)),
                pl.BlockSpec((None, t, HEAD_DIM), lambda b, g, i: (b, 0, v_col + g))]
    args = [y3, y3, y3]
    if sink is not None:
        in_specs = [pl.BlockSpec(memory_space=pltpu.SMEM)] + in_specs
        args = [sink] + args
    out = pl.pallas_call(
        functools.partial(_win_kernel, window=window, r=r, qb=qb, has_sink=sink is not None),
        grid=(bsz, g_kv, nb),
        in_specs=in_specs,
        out_specs=pl.BlockSpec((None, Q_BLOCK * qb, rw), lambda b, g, i: (b, i, g)),
        out_shape=jax.ShapeDtypeStruct((bsz, t, N_HEADS * HEAD_DIM), out_dtype),
        compiler_params=_cparams(("parallel", "parallel", "arbitrary")),
        name="window_prompt",
    )(*args)
    return out.reshape(bsz * t, N_HEADS * HEAD_DIM)


SB_UNROLL = 3
SB_CUTOFF = -104.0


def _softplus_neg_abs(z):
    return jnp.log(1.0 + jnp.exp(-jnp.abs(z)))


def _sb_kernel(q_ref, k_ref, v_ref, o_ref, *, r, qb):
    rq = r * Q_BLOCK
    row = lax.broadcasted_iota(jnp.int32, (Q_BLOCK, Q_BLOCK), 0)
    col = lax.broadcasted_iota(jnp.int32, (Q_BLOCK, Q_BLOCK), 1)
    later_sel = (row > col).astype(BF16)
    for blk in range(qb):
        i = pl.program_id(2) * qb + blk
        rows = slice(blk * Q_BLOCK, (blk + 1) * Q_BLOCK)
        qs = _stack_heads(q_ref[rows, :], r).astype(BF16)
        n_it = (i + SB_UNROLL) // SB_UNROLL

        def cond(carry, n_it=n_it):
            t, live, _, _ = carry
            return jnp.logical_and(t < n_it, live > 0)

        def body(carry, i=i, qs=qs):
            t, _, acc, cs = carry
            for u in range(SB_UNROLL):
                j = i - (t * SB_UNROLL + u)
                q_lim = jnp.where(j >= 0, i * Q_BLOCK, -Q_BLOCK)
                off = pl.multiple_of(jnp.maximum(j, 0) * Q_BLOCK, Q_BLOCK)
                kb = k_ref[pl.ds(off, Q_BLOCK), :].astype(BF16)
                vb = v_ref[pl.ds(off, Q_BLOCK), :].astype(BF16)
                z = _nt(qs, kb) * SCALE
                causal = _tile_rows((off + col) < (q_lim + row), r)
                sp = _softplus_neg_abs(z)
                log_beta = jnp.minimum(z, 0.0) - sp
                log_keep = jnp.where(causal, jnp.minimum(-z, 0.0) - sp, 0.0)
                later = _dot2(log_keep, later_sel) + cs
                a = jnp.where(causal, jnp.exp(log_beta + later), 0.0)
                acc = acc + jnp.dot(a.astype(BF16), vb, preferred_element_type=F32)
                cs = cs + jnp.sum(log_keep, axis=-1, keepdims=True)
            live = jnp.where(jnp.max(cs) > SB_CUTOFF, 1, 0)
            return t + 1, live, acc, cs

        init = (jnp.int32(0), jnp.int32(1), jnp.zeros((rq, HEAD_DIM), F32), jnp.zeros((rq, 1), F32))
        _, _, acc, _ = lax.while_loop(cond, body, init)
        for h in range(r):
            o_ref[rows, h * HEAD_DIM:(h + 1) * HEAD_DIM] = acc[h * Q_BLOCK:(h + 1) * Q_BLOCK].astype(o_ref.dtype)


def sb_prompt_attn(y, bsz, t, out_dtype):
    r = N_HEADS // SB_KV
    qb = _pick(t // Q_BLOCK, (2, 1))
    nb = t // (Q_BLOCK * qb)
    rw = r * HEAD_DIM
    y3 = y.reshape(bsz, t, y.shape[1])
    out = pl.pallas_call(
        functools.partial(_sb_kernel, r=r, qb=qb),
        grid=(bsz, SB_KV, nb),
        in_specs=[pl.BlockSpec((None, Q_BLOCK * qb, rw), lambda b, g, i: (b, i, g)),
                  pl.BlockSpec((None, t, HEAD_DIM), lambda b, g, i: (b, 0, N_HEADS + g)),
                  pl.BlockSpec((None, t, HEAD_DIM), lambda b, g, i: (b, 0, N_HEADS + SB_KV + g))],
        out_specs=pl.BlockSpec((None, Q_BLOCK * qb, rw), lambda b, g, i: (b, i, g)),
        out_shape=jax.ShapeDtypeStruct((bsz, t, N_HEADS * HEAD_DIM), out_dtype),
        compiler_params=_cparams(("parallel", "parallel", "arbitrary")),
        name="sb_prompt",
    )(y3, y3, y3)
    return out.reshape(bsz * t, N_HEADS * HEAD_DIM)


SEG_PER_PAGE = 128 // NSA_CMP_STRIDE


def _pool_kernel(*refs, pp, g_kv, prefetch):
    refs = refs[prefetch:]
    x_refs = refs[:pp]
    w0_ref, w1_ref, u0_ref, u1_ref = refs[pp:]
    vps = NSA_CMP_STRIDE * g_kv // SUBLANES
    sub = lax.broadcasted_iota(jnp.int32, (SUBLANES, HEAD_DIM), 0)
    for s_i, x_ref in enumerate(x_refs):
        rows = slice(s_i * SEG_PER_PAGE, (s_i + 1) * SEG_PER_PAGE)
        tiles = [[jnp.zeros((SEG_PER_PAGE, HEAD_DIM), F32) for _ in range(g_kv)] for _ in range(2)]
        for n in range(SEG_PER_PAGE):
            accs = [jnp.zeros((SUBLANES, HEAD_DIM), F32), jnp.zeros((SUBLANES, HEAD_DIM), F32)]
            for v in range(vps):
                xv = x_ref[(n * vps + v) * SUBLANES:(n * vps + v + 1) * SUBLANES, :]
                accs[0] = accs[0] + xv * w0_ref[v]
                accs[1] = accs[1] + xv * w1_ref[v]
            for half in range(2):
                acc = accs[half]
                sh = g_kv
                while sh < SUBLANES:
                    acc = acc + pltpu.roll(acc, sh, 0)
                    sh *= 2
                for g in range(g_kv):
                    k = (n - g) % SUBLANES
                    moved = pltpu.roll(acc, k, 0) if k else acc
                    tiles[half][g] = jnp.where(sub == n, moved, tiles[half][g])
        for g in range(g_kv):
            u0_ref[rows, g * HEAD_DIM:(g + 1) * HEAD_DIM] = tiles[0][g]
            u1_ref[rows, g * HEAD_DIM:(g + 1) * HEAD_DIM] = tiles[1][g]


def _pool_weights(a, g_kv):
    ppv = SUBLANES // g_kv
    a_r = a.reshape(NSA_CMP_LEN // NSA_CMP_STRIDE, NSA_CMP_STRIDE // ppv, ppv, 1, HEAD_DIM)
    a_r = jnp.broadcast_to(a_r, a_r.shape[:3] + (g_kv, HEAD_DIM))
    a_r = a_r.reshape(2, NSA_CMP_STRIDE // ppv, SUBLANES, HEAD_DIM)
    return a_r[0], a_r[1]


def pool_prompt(x, a, pp):
    bsz, t, g_kv, _ = x.shape
    gw = g_kv * HEAD_DIM
    n_pages = t // 128
    xr = x.reshape(bsz, t * g_kv, HEAD_DIM)
    a0, a1 = _pool_weights(a, g_kv)
    in_specs = [pl.BlockSpec((None, 128 * g_kv, HEAD_DIM), lambda b, p, s=s: (b, p * pp + s, 0))
                for s in range(pp)]
    in_specs += [pl.BlockSpec(a0.shape, lambda b, p: (0, 0, 0))] * 2
    seg = pp * SEG_PER_PAGE
    return pl.pallas_call(
        functools.partial(_pool_kernel, pp=pp, g_kv=g_kv, prefetch=0),
        grid=(bsz, n_pages // pp),
        in_specs=in_specs,
        out_specs=[pl.BlockSpec((None, seg, gw), lambda b, p: (b, p, 0))] * 2,
        out_shape=[jax.ShapeDtypeStruct((bsz, t // NSA_CMP_STRIDE, gw), F32)] * 2,
        compiler_params=_cparams(("parallel", "arbitrary")),
        name="pool_prompt",
    )(*([xr] * pp), a0, a1)


def pool_paged(pool, page_table, a, pp):
    db, n_pages = page_table.shape
    n_pool, page, g_kv, _ = pool.shape
    assert page == 128
    gw = g_kv * HEAD_DIM
    pr = pool.reshape(n_pool, page * g_kv, HEAD_DIM)
    a0, a1 = _pool_weights(a, g_kv)
    in_specs = [pl.BlockSpec((None, page * g_kv, HEAD_DIM), lambda b, p, pt, s=s: (pt[b, p * pp + s], 0, 0))
                for s in range(pp)]
    in_specs += [pl.BlockSpec(a0.shape, lambda b, p, pt: (0, 0, 0))] * 2
    seg = pp * SEG_PER_PAGE
    grid_spec = pltpu.PrefetchScalarGridSpec(
        num_scalar_prefetch=1,
        grid=(db, n_pages // pp),
        in_specs=in_specs,
        out_specs=[pl.BlockSpec((None, seg, gw), lambda b, p, pt: (b, p, 0))] * 2,
    )
    return pl.pallas_call(
        functools.partial(_pool_kernel, pp=pp, g_kv=g_kv, prefetch=1),
        grid_spec=grid_spec,
        out_shape=[jax.ShapeDtypeStruct((db, n_pages * SEG_PER_PAGE, gw), F32)] * 2,
        compiler_params=_cparams(("parallel", "arbitrary")),
        name="pool_paged",
    )(page_table, *([pr] * pp), a0, a1)


def _cmp_mlp_kernel(u0_ref, u1_ref, a_ref, pe_ref, w1_ref, w2_ref, o_ref):
    n_seg = u0_ref.shape[0]
    c = jnp.sum(a_ref[...] * pe_ref[...], axis=0, keepdims=True)
    w1 = w1_ref[...].astype(BF16)
    w2 = w2_ref[...].astype(BF16)
    for g in range(u0_ref.shape[1] // HEAD_DIM):
        cols = slice(g * HEAD_DIM, (g + 1) * HEAD_DIM)
        u = u0_ref[:, cols] + pltpu.roll(u1_ref[:, cols], n_seg - 1, 0)
        u = u + c
        h = jax.nn.gelu(jnp.dot(u.astype(BF16), w1, preferred_element_type=F32))
        o_ref[:, cols] = jnp.dot(h.astype(BF16), w2, preferred_element_type=F32)


def cmp_mlp(u0, u1, a, pe, w1, w2):
    bx, n_seg, gw = u0.shape
    full = lambda arr: pl.BlockSpec(arr.shape, lambda b: (0,) * arr.ndim)
    rows = pl.BlockSpec((None, n_seg, gw), lambda b: (b, 0, 0))
    return pl.pallas_call(
        _cmp_mlp_kernel,
        grid=(bx,),
        in_specs=[rows, rows, full(a), full(pe), full(w1), full(w2)],
        out_specs=rows,
        out_shape=jax.ShapeDtypeStruct((bx, n_seg, gw), F32),
        compiler_params=_cparams(("parallel",)),
        name="cmp_mlp",
    )(u0, u1, a, pe, w1, w2)


def _cover_matrix(n_rows, n_cmp, n_cols, n_sel):
    cs = np.arange(n_rows) * NSA_CMP_STRIDE
    ss = np.arange(n_cols) * NSA_SEL_LEN
    m = (cs[:, None] < ss[None, :] + NSA_SEL_LEN) & (cs[:, None] + NSA_CMP_LEN > ss[None, :])
    m = m & (np.arange(n_rows)[:, None] < n_cmp) & (np.arange(n_cols)[None, :] < n_sel)
    return jnp.asarray(m, BF16)


def _masked_softmax(logits, mask):
    logits = jnp.where(mask, logits, NEG_INF)
    m = jnp.max(logits, axis=-1, keepdims=True)
    m = jnp.where(m == NEG_INF, 0.0, m)
    e = jnp.exp(logits - m)
    s = jnp.sum(e, axis=-1, keepdims=True)
    return e / jnp.where(s > 0, s, 1.0)


def _nsa_prompt_cmp_kernel(q_ref, kc_ref, vc_ref, cover_ref, ocmp_ref, sel_ref, *, qb, **kw):
    for blk in range(qb):
        rows = slice(blk * Q_BLOCK, (blk + 1) * Q_BLOCK)
        _nsa_prompt_cmp_block(pl.program_id(2) * qb + blk, q_ref.at[rows, :], kc_ref, vc_ref, cover_ref,
                              ocmp_ref.at[rows, :], sel_ref.at[rows, :], **kw)


def _nsa_prompt_cmp_block(i, q_ref, kc_ref, vc_ref, cover_ref, ocmp_ref, sel_ref, *, r, n_cmp, n_sel, n_top):
    qs = _stack_heads(q_ref[...], r).astype(BF16)
    nc = kc_ref.shape[0]

    lc = _nt(qs, kc_ref[...].astype(BF16)) * SCALE
    qp_c = i * Q_BLOCK + lax.broadcasted_iota(jnp.int32, (Q_BLOCK, nc), 0)
    n_c = lax.broadcasted_iota(jnp.int32, (Q_BLOCK, nc), 1)
    cmask = _tile_rows((n_c * NSA_CMP_STRIDE + NSA_CMP_LEN - 1 <= qp_c) & (n_c < n_cmp), r)
    pc = _masked_softmax(lc, cmask)
    ocmp = jnp.dot(pc.astype(BF16), vc_ref[...].astype(BF16), preferred_element_type=F32)
    for h in range(r):
        ocmp_ref[:, h * HEAD_DIM:(h + 1) * HEAD_DIM] = ocmp[h * Q_BLOCK:(h + 1) * Q_BLOCK]

    p_sum = pc[0:Q_BLOCK]
    for h in range(1, r):
        p_sum = p_sum + pc[h * Q_BLOCK:(h + 1) * Q_BLOCK]
    nj = -(-n_sel // SUBLANES) * SUBLANES
    cover_t = cover_ref[...]
    hi, mid, lo = _split3(p_sum)
    imp = (_nt(cover_t, hi) + _nt(cover_t, mid) + _nt(cover_t, lo))[:nj]
    qp = i * Q_BLOCK + lax.broadcasted_iota(jnp.int32, (nj, Q_BLOCK), 1)
    jj = lax.broadcasted_iota(jnp.int32, (nj, Q_BLOCK), 0)
    cur = qp // NSA_SEL_LEN
    vis = jj * NSA_SEL_LEN <= qp
    forced = (jj == 0) | (jj == cur) | (jj == cur - 1)
    score = jnp.where(vis, jnp.where(forced, SEL_FORCE, imp), -1.0)
    score = jnp.where(jj < n_sel, score, -2.0)
    rank = jnp.zeros((nj, Q_BLOCK), F32)
    for t in range(n_sel):
        st = score[t:t + 1, :]
        beats = (st > score) | ((st == score) & (jj > t))
        rank = rank + jnp.where(beats, 1.0, 0.0)
    sel_t = jnp.where((rank < n_top) & (score >= 0.0), 1.0, 0.0)
    if nj < LANES:
        sel_t = jnp.concatenate([sel_t, jnp.zeros((LANES - nj, Q_BLOCK), F32)], axis=0)
    sel_ref[...] = sel_t.T.astype(sel_ref.dtype)


def _nsa_prompt_slc_kernel(q_ref, sel_ref, ks_ref, vs_ref, oslc_ref, *, qb, **kw):
    for blk in range(qb):
        rows = slice(blk * Q_BLOCK, (blk + 1) * Q_BLOCK)
        _nsa_prompt_slc_block(pl.program_id(2) * qb + blk, q_ref.at[rows, :], sel_ref.at[rows, :], ks_ref,
                              vs_ref, oslc_ref.at[rows, :], **kw)


def _nsa_prompt_slc_block(i, q_ref, sel_ref, ks_ref, vs_ref, oslc_ref, *, r, kc):
    rq = r * Q_BLOCK
    qs = _stack_heads(q_ref[...], r).astype(BF16)
    sel = sel_ref[...]
    row = lax.broadcasted_iota(jnp.int32, (Q_BLOCK, kc), 0)
    col = lax.broadcasted_iota(jnp.int32, (Q_BLOCK, kc), 1)

    def body(kb, carry):
        m, l, acc = carry
        off = pl.multiple_of(kb * kc, kc)
        k = ks_ref[pl.ds(off, kc), :].astype(BF16)
        v = vs_ref[pl.ds(off, kc), :].astype(BF16)
        s = _nt(qs, k) * SCALE
        expand = jnp.where(row == (off + col) // NSA_SEL_LEN, 1.0, 0.0).astype(BF16)
        picked = jnp.dot(sel, expand, preferred_element_type=F32) > 0.5
        mask = _tile_rows(picked & ((off + col) <= (i * Q_BLOCK + row)), r)
        s = jnp.where(mask, s, NEG_INF)
        m_new = jnp.maximum(m, jnp.max(s, axis=-1, keepdims=True))
        m_safe = jnp.where(m_new == NEG_INF, 0.0, m_new)
        p = jnp.exp(s - m_safe)
        alpha = jnp.exp(m - m_safe)
        l = alpha * l + jnp.sum(p, axis=-1, keepdims=True)
        acc = alpha * acc + jnp.dot(p.astype(BF16), v, preferred_element_type=F32)
        return m_new, l, acc

    init = (jnp.full((rq, 1), NEG_INF, F32), jnp.zeros((rq, 1), F32), jnp.zeros((rq, HEAD_DIM), F32))
    n_it = ((i + 1) * Q_BLOCK + kc - 1) // kc
    _, l, acc = lax.fori_loop(0, n_it, body, init)
    oslc = acc / jnp.where(l > 0, l, 1.0)
    for h in range(r):
        oslc_ref[:, h * HEAD_DIM:(h + 1) * HEAD_DIM] = oslc[h * Q_BLOCK:(h + 1) * Q_BLOCK]


def nsa_prompt_global(y, kcmp, vcmp, bsz, t):
    r = N_HEADS // NSA_KV
    qb = _pick(t // Q_BLOCK, (2, 1))
    nb = t // (Q_BLOCK * qb)
    rw = r * HEAD_DIM
    n_seg = t // NSA_CMP_STRIDE
    n_cmp = n_seg - NSA_CMP_LEN // NSA_CMP_STRIDE + 1
    n_sel = -(-t // NSA_SEL_LEN)
    assert n_sel <= LANES and t % NSA_SEL_LEN == 0
    n_top = min(NSA_TOP, n_sel)
    y3 = y.reshape(bsz, t, y.shape[1])
    cover = _cover_matrix(n_seg, n_cmp, LANES, n_sel).T
    ks_col = N_HEADS + 2 * NSA_KV
    vs_col = N_HEADS + 3 * NSA_KV
    o_shape = jax.ShapeDtypeStruct((bsz, t, N_HEADS * HEAD_DIM), F32)
    q_spec = pl.BlockSpec((None, Q_BLOCK * qb, rw), lambda b, g, i: (b, i, g))
    sel_spec = pl.BlockSpec((None, None, Q_BLOCK * qb, LANES), lambda b, g, i: (b, g, i, 0))
    ocmp, sel = pl.pallas_call(
        functools.partial(_nsa_prompt_cmp_kernel, qb=qb, r=r, n_cmp=n_cmp, n_sel=n_sel, n_top=n_top),
        grid=(bsz, NSA_KV, nb),
        in_specs=[q_spec,
                  pl.BlockSpec((None, n_seg, HEAD_DIM), lambda b, g, i: (b, 0, g)),
                  pl.BlockSpec((None, n_seg, HEAD_DIM), lambda b, g, i: (b, 0, g)),
                  pl.BlockSpec(cover.shape, lambda b, g, i: (0, 0))],
        out_specs=[q_spec, sel_spec],
        out_shape=[o_shape, jax.ShapeDtypeStruct((bsz, NSA_KV, t, LANES), BF16)],
        compiler_params=_cparams(("parallel", "parallel", "arbitrary")),
        name="nsa_prompt_cmp",
    )(y3, kcmp, vcmp, cover)
    oslc = pl.pallas_call(
        functools.partial(_nsa_prompt_slc_kernel, qb=qb, r=r, kc=_pick(t, (512, 256, 128))),
        grid=(bsz, NSA_KV, nb),
        in_specs=[q_spec, sel_spec,
                  pl.BlockSpec((None, t, HEAD_DIM), lambda b, g, i: (b, 0, ks_col + g)),
                  pl.BlockSpec((None, t, HEAD_DIM), lambda b, g, i: (b, 0, vs_col + g))],
        out_specs=q_spec,
        out_shape=o_shape,
        compiler_params=_cparams(("parallel", "parallel", "arbitrary")),
        name="nsa_prompt_slc",
    )(y3, sel, y3, y3)
    return ocmp.reshape(bsz * t, -1), oslc.reshape(bsz * t, -1)


def _combine_kernel(gl_ref, oc_ref, os_ref, ow_ref, o_ref):
    gates = jax.nn.sigmoid(gl_ref[...])
    for h in range(N_HEADS):
        sl = slice(h * HEAD_DIM, (h + 1) * HEAD_DIM)
        acc = gates[:, 3 * h:3 * h + 1] * oc_ref[:, sl]
        acc = acc + gates[:, 3 * h + 1:3 * h + 2] * os_ref[:, sl]
        acc = acc + gates[:, 3 * h + 2:3 * h + 3] * ow_ref[:, sl]
        o_ref[:, sl] = acc.astype(o_ref.dtype)


def nsa_combine(gate_logits, o_cmp, o_slc, o_win, tm):
    m, d = o_cmp.shape
    row = lambda w: pl.BlockSpec((tm, w), lambda i: (i, 0))
    return pl.pallas_call(
        _combine_kernel,
        grid=(m // tm,),
        in_specs=[row(gate_logits.shape[1]), row(d), row(d), row(d)],
        out_specs=row(d),
        out_shape=jax.ShapeDtypeStruct((m, d), BF16),
        compiler_params=_cparams(("parallel",)),
        name="nsa_combine",
    )(gate_logits, o_cmp, o_slc, o_win)


def _block_diag_q(q, g_kv):
    db = q.shape[0]
    r = N_HEADS // g_kv
    qh = q.reshape(db, N_HEADS, 1, HEAD_DIM)
    onehot = (jnp.arange(N_HEADS)[:, None] // r == jnp.arange(g_kv)[None, :]).astype(q.dtype)
    return (qh * onehot[None, :, :, None]).reshape(db, N_HEADS, g_kv * HEAD_DIM).astype(BF16)


def _diag_extract(o_all, g_kv):
    r = N_HEADS // g_kv
    hrow = lax.broadcasted_iota(jnp.int32, (N_HEADS, HEAD_DIM), 0)
    out = jnp.zeros((N_HEADS, HEAD_DIM), F32)
    for g in range(g_kv):
        out = out + jnp.where(hrow // r == g, o_all[:, g * HEAD_DIM:(g + 1) * HEAD_DIM], 0.0)
    return out


def _bf16_round(x):
    return x.astype(BF16).astype(F32)


def _group_rows(x_ref, g_kv):
    r = N_HEADS // g_kv
    hrow = lax.broadcasted_iota(jnp.int32, (N_HEADS, HEAD_DIM), 0)
    out = jnp.zeros((N_HEADS, HEAD_DIM), F32)
    for g in range(g_kv):
        out = out + jnp.where(hrow // r == g, x_ref[g:g + 1, :], 0.0)
    return out


def _win_sample_kernel(*refs, window, past_len, g_kv, has_sink):
    if has_sink:
        sink_ref, q_ref, kb_ref, vb_ref, kn_ref, vn_ref, o_ref = refs
    else:
        q_ref, kb_ref, vb_ref, kn_ref, vn_ref, o_ref = refs
    rows = kb_ref.shape[0]
    wb = rows // g_kv
    r = N_HEADS // g_kv
    q = q_ref[...]
    z = _nt(q, kb_ref[...].astype(BF16)) * SCALE
    zn = jnp.sum(q.astype(F32) * _bf16_round(_group_rows(kn_ref, g_kv)), axis=-1, keepdims=True) * SCALE
    hrow = lax.broadcasted_iota(jnp.int32, (N_HEADS, rows), 0)
    lane = lax.broadcasted_iota(jnp.int32, (N_HEADS, rows), 1)
    k_pos = past_len - wb + lane // g_kv
    rel = past_len - k_pos
    mask = (lane % g_kv == hrow // r) & (rel >= 0) & (rel <= window) & (k_pos >= 0)
    z = jnp.where(mask, z, NEG_INF)
    m = jnp.maximum(jnp.max(z, axis=-1, keepdims=True), zn)
    if has_sink:
        m = jnp.maximum(m, sink_ref[...])
    p = jnp.exp(z - m)
    pn = jnp.exp(zn - m)
    den = jnp.sum(p, axis=-1, keepdims=True) + pn
    if has_sink:
        den = den + jnp.exp(sink_ref[...] - m)
    o = jnp.dot((p / den).astype(BF16), vb_ref[...].astype(BF16), preferred_element_type=F32)
    o = o + _bf16_round(pn / den) * _bf16_round(_group_rows(vn_ref, g_kv))
    o_ref[...] = o.astype(o_ref.dtype)


def window_sample_attn(q, k_new, v_new, k_buf, v_buf, past_len, window, sink, out_dtype):
    db, wb, g_kv, _ = k_buf.shape
    rows = wb * g_kv
    in_specs = [pl.BlockSpec((None, N_HEADS, HEAD_DIM), lambda b: (b, 0, 0)),
                pl.BlockSpec((None, rows, HEAD_DIM), lambda b: (b, 0, 0)),
                pl.BlockSpec((None, rows, HEAD_DIM), lambda b: (b, 0, 0)),
                pl.BlockSpec((None, g_kv, HEAD_DIM), lambda b: (b, 0, 0)),
                pl.BlockSpec((None, g_kv, HEAD_DIM), lambda b: (b, 0, 0))]
    args = [q.reshape(db, N_HEADS, HEAD_DIM).astype(BF16),
            k_buf.reshape(db, rows, HEAD_DIM), v_buf.reshape(db, rows, HEAD_DIM),
            k_new.reshape(db, g_kv, HEAD_DIM), v_new.reshape(db, g_kv, HEAD_DIM)]
    if sink is not None:
        in_specs = [pl.BlockSpec((N_HEADS, 1), lambda b: (0, 0))] + in_specs
        args = [sink.reshape(N_HEADS, 1)] + args
    out = pl.pallas_call(
        functools.partial(_win_sample_kernel, window=window, past_len=past_len, g_kv=g_kv,
                          has_sink=sink is not None),
        grid=(db,),
        in_specs=in_specs,
        out_specs=pl.BlockSpec((None, N_HEADS, HEAD_DIM), lambda b: (b, 0, 0)),
        out_shape=jax.ShapeDtypeStruct((db, N_HEADS, HEAD_DIM), out_dtype),
        compiler_params=_cparams(("parallel",)),
        name="window_sample",
    )(*args)
    return out.reshape(db, N_HEADS * HEAD_DIM)


def _sb_sample_kernel(*refs, pp, n_pages, first_rank, q_pos, g_kv):
    refs = refs[1:]
    q_ref, cs0_ref, acc0_ref = refs[:3]
    k_refs = refs[3:3 + pp]
    v_refs = refs[3 + pp:3 + 2 * pp]
    cs_out_ref, acc_out_ref, cs_ref, acc_ref = refs[3 + 2 * pp:]
    p = pl.program_id(1)
    r = N_HEADS // g_kv
    n_ch = g_kv
    cp = LANES // g_kv

    @pl.when(p == 0)
    def _():
        cs_ref[...] = cs0_ref[:, 0:1]
        acc_ref[...] = acc0_ref[...]

    q = q_ref[...]
    row = lax.broadcasted_iota(jnp.int32, (LANES, LANES), 0)
    col = lax.broadcasted_iota(jnp.int32, (LANES, LANES), 1)
    later_sel = (row > col).astype(BF16)
    n_rank = pp * n_ch
    chunks = [(s, c) for s in range(pp) for c in reversed(range(n_ch))]
    srow = lax.broadcasted_iota(jnp.int32, (n_rank * N_HEADS, LANES), 0)
    slane = lax.broadcasted_iota(jnp.int32, (n_rank * N_HEADS, LANES), 1)
    own = (slane % g_kv) == ((srow % N_HEADS) // r)
    rank = srow // N_HEADS
    k_pos = ((n_pages - 1 - first_rank - p * pp) * n_ch - rank) * cp + slane // g_kv
    causal = own & (k_pos < q_pos)
    z_pages = [_nt(q, k_refs[s][...].astype(BF16)) for s in range(pp)]
    z = jnp.concatenate([z_pages[s][:, c * LANES:(c + 1) * LANES] for s, c in chunks],
                        axis=0) * SCALE
    sp = _softplus_neg_abs(z)
    log_beta = jnp.minimum(z, 0.0) - sp
    log_keep = jnp.where(causal, jnp.minimum(-z, 0.0) - sp, 0.0)
    rs = jnp.sum(log_keep, axis=-1, keepdims=True)
    cs = cs_ref[...]
    carries = []
    for ci in range(n_rank):
        carries.append(cs)
        cs = cs + rs[ci * N_HEADS:(ci + 1) * N_HEADS]
    later = _dot2(log_keep, later_sel) + jnp.concatenate(carries, axis=0)
    a = jnp.where(causal, jnp.exp(log_beta + later), 0.0).astype(BF16)
    acc = acc_ref[...]
    for s in range(pp):
        ranks = [s * n_ch + (n_ch - 1 - c) for c in range(n_ch)]
        a_page = jnp.concatenate([a[k * N_HEADS:(k + 1) * N_HEADS] for k in ranks], axis=1)
        acc = acc + jnp.dot(a_page, v_refs[s][...].astype(BF16), preferred_element_type=F32)
    cs_ref[...] = cs
    acc_ref[...] = acc

    @pl.when(p == pl.num_programs(1) - 1)
    def _():
        cs_out_ref[...] = jnp.broadcast_to(cs, cs_out_ref.shape)
        acc_out_ref[...] = acc


def _sb_sample_sweep(q16, pk, pv, page_table, cs0, acc0, first_rank, n_steps, pp, g_kv):
    db, n_pages = page_table.shape
    rows = pk.shape[1]
    q_pos = n_pages * (rows // g_kv)
    page_spec = lambda s: pl.BlockSpec(
        (None, rows, HEAD_DIM),
        lambda b, p, pt, s=s: (pt[b, n_pages - 1 - first_rank - (p * pp + s)], 0, 0))
    state_spec = pl.BlockSpec((None, N_HEADS, HEAD_DIM), lambda b, p, pt: (b, 0, 0))
    grid_spec = pltpu.PrefetchScalarGridSpec(
        num_scalar_prefetch=1,
        grid=(db, n_steps),
        in_specs=[state_spec, state_spec, state_spec] + [page_spec(s) for s in range(pp)] * 2,
        out_specs=[state_spec, state_spec],
        scratch_shapes=[pltpu.VMEM((N_HEADS, 1), F32), pltpu.VMEM((N_HEADS, HEAD_DIM), F32)],
    )
    state = jax.ShapeDtypeStruct((db, N_HEADS, HEAD_DIM), F32)
    return pl.pallas_call(
        functools.partial(_sb_sample_kernel, pp=pp, n_pages=n_pages, first_rank=first_rank, q_pos=q_pos,
                          g_kv=g_kv),
        grid_spec=grid_spec,
        out_shape=[state, state],
        compiler_params=_cparams(("parallel", "arbitrary")),
        name="sb_sample",
    )(page_table, q16, cs0, acc0, *([pk] * pp), *([pv] * pp))


def sb_sample_attn(q, pool_k, pool_v, page_table, pp, out_dtype):
    db, n_pages = page_table.shape
    n_pool, page, g_kv, _ = pool_k.shape
    assert page == 128 and n_pages % pp == 0
    rows = page * g_kv
    pk = pool_k.reshape(n_pool, rows, HEAD_DIM)
    pv = pool_v.reshape(n_pool, rows, HEAD_DIM)
    q16 = q.reshape(db, N_HEADS, HEAD_DIM).astype(BF16)
    zero = jnp.zeros((db, N_HEADS, HEAD_DIM), F32)
    cs, acc = _sb_sample_sweep(q16, pk, pv, page_table, zero, zero, 0, 1, pp, g_kv)
    if n_pages > pp:
        rest = lambda st: _sb_sample_sweep(q16, pk, pv, page_table, st[0], st[1], pp, n_pages // pp - 1,
                                           pp, g_kv)[1]
        acc = lax.cond(jnp.max(cs) > SB_CUTOFF, rest, lambda st: st[1], (cs, acc))
    return acc.astype(out_dtype).reshape(db, N_HEADS * HEAD_DIM)


def _nsa_sample_cmp_kernel(q_ref, kc_ref, vc_ref, cover_ref, ocmp_ref, idx_ref, val_ref,
                           *, n_cmp, n_sel, n_top, q_pos):
    q = q_ref[...]
    nc = kc_ref.shape[0]
    ns = cover_ref.shape[1]
    r = N_HEADS // NSA_KV
    lc = _nt(q, kc_ref[...].astype(BF16)) * SCALE
    n_c = lax.broadcasted_iota(jnp.int32, (N_HEADS, nc), 1)
    cmask = (n_c * NSA_CMP_STRIDE + NSA_CMP_LEN - 1 <= q_pos) & (n_c < n_cmp)
    pc = _masked_softmax(lc, cmask)
    o_all = jnp.dot(pc.astype(BF16), vc_ref[...].astype(BF16), preferred_element_type=F32)
    ocmp_ref[...] = _diag_extract(o_all, NSA_KV)

    grow = lax.broadcasted_iota(jnp.int32, (SUBLANES, N_HEADS), 0)
    hcol = lax.broadcasted_iota(jnp.int32, (SUBLANES, N_HEADS), 1)
    group_sel = (hcol // r == grow).astype(BF16)
    p_sum = _dot3r(group_sel, pc)
    imp = _dot3(p_sum, cover_ref[...])
    jj = lax.broadcasted_iota(jnp.int32, (SUBLANES, ns), 1)
    cur = q_pos // NSA_SEL_LEN
    vis = jj * NSA_SEL_LEN <= q_pos
    forced = (jj == 0) | (jj == cur) | (jj == cur - 1)
    score = jnp.where(vis, jnp.where(forced, SEL_FORCE, imp), -1.0)
    score = jnp.where(jj < n_sel, score, -2.0)
    jf = jj.astype(F32)
    tl = lax.broadcasted_iota(jnp.int32, (SUBLANES, LANES), 1)
    idx_out = jnp.zeros((SUBLANES, LANES), F32)
    val_out = jnp.full((SUBLANES, LANES), -1.0, F32)
    for t in range(n_top):
        mx = jnp.max(score, axis=-1, keepdims=True)
        first = jnp.min(jnp.where(score == mx, jf, float(ns)), axis=-1, keepdims=True)
        idx_out = jnp.where(tl == t, first, idx_out)
        val_out = jnp.where(tl == t, mx, val_out)
        score = jnp.where(jf == first, NEG_INF, score)
    idx_ref[...] = idx_out.astype(jnp.int32)
    val_ref[...] = val_out


def nsa_sample_cmp(q, kcmp, vcmp, n_cmp, n_sel, q_pos):
    db, nc, gw = kcmp.shape
    ns = -(-n_sel // LANES) * LANES
    n_top = min(NSA_TOP, n_sel)
    cover = _cover_matrix(nc, n_cmp, ns, n_sel)
    qbd = _block_diag_q(q, NSA_KV)
    return pl.pallas_call(
        functools.partial(_nsa_sample_cmp_kernel, n_cmp=n_cmp, n_sel=n_sel, n_top=n_top, q_pos=q_pos),
        grid=(db,),
        in_specs=[pl.BlockSpec((None, N_HEADS, gw), lambda b: (b, 0, 0)),
                  pl.BlockSpec((None, nc, gw), lambda b: (b, 0, 0)),
                  pl.BlockSpec((None, nc, gw), lambda b: (b, 0, 0)),
                  pl.BlockSpec(cover.shape, lambda b: (0, 0))],
        out_specs=[pl.BlockSpec((None, N_HEADS, HEAD_DIM), lambda b: (b, 0, 0)),
                   pl.BlockSpec((None, SUBLANES, LANES), lambda b: (b, 0, 0)),
                   pl.BlockSpec((None, SUBLANES, LANES), lambda b: (b, 0, 0))],
        out_shape=[jax.ShapeDtypeStruct((db, N_HEADS, HEAD_DIM), F32),
                   jax.ShapeDtypeStruct((db, SUBLANES, LANES), jnp.int32),
                   jax.ShapeDtypeStruct((db, SUBLANES, LANES), F32)],
        compiler_params=_cparams(("parallel",)),
        name="nsa_sample_cmp",
    )(qbd, kcmp, vcmp, cover)


def _nsa_sample_slc_kernel(*refs, n_top, n_sel, q_pos):
    idx_ref, ok_ref, pt_ref = refs[:3]
    refs = refs[3:]
    q_ref = refs[0]
    k_refs = refs[1:1 + n_top]
    v_refs = refs[1 + n_top:1 + 2 * n_top]
    kn_ref, vn_ref, o_ref = refs[1 + 2 * n_top:]
    b = pl.program_id(0)
    g = pl.program_id(1)
    r = N_HEADS // NSA_KV
    rows = NSA_SEL_LEN * NSA_KV
    base = (b * NSA_KV + g) * n_top

    @pl.when(g == 0)
    def _():
        o_ref[...] = jnp.zeros_like(o_ref)

    q = q_ref[...]
    lane = lax.broadcasted_iota(jnp.int32, (N_HEADS, rows), 1)
    lane_pos = lane // NSA_KV
    lane_own = (lane % NSA_KV) == g
    zs = []
    new_ok = jnp.int32(0)
    for t in range(n_top):
        j = idx_ref[base + t]
        ok = ok_ref[base + t]
        in_pool = j < n_sel - 1
        z = _nt(q, k_refs[t][...].astype(BF16)) * SCALE
        last_pos = jnp.where((ok > 0) & in_pool, q_pos, -1)
        zs.append(jnp.where(lane_own & ((j * NSA_SEL_LEN + lane_pos) <= last_pos), z, NEG_INF))
        new_ok = new_ok | jnp.where((ok > 0) & jnp.logical_not(in_pool), 1, 0)
    new_vis = (new_ok > 0) & ((n_sel - 1) * NSA_SEL_LEN <= q_pos)
    zn = jnp.sum(q.astype(F32) * _bf16_round(kn_ref[...]), axis=-1, keepdims=True) * SCALE
    zn = zn + jnp.where(new_vis, 0.0, NEG_INF)
    m = zn
    for z in zs:
        m = jnp.maximum(m, jnp.max(z, axis=-1, keepdims=True))
    m = jnp.where(m == NEG_INF, 0.0, m)
    pn = jnp.exp(zn - m)
    den = pn
    ps = []
    for z in zs:
        e = jnp.exp(z - m)
        ps.append(e)
        den = den + jnp.sum(e, axis=-1, keepdims=True)
    den = jnp.where(den > 0, den, 1.0)
    o = _bf16_round(pn / den) * _bf16_round(vn_ref[...])
    for t in range(n_top):
        o = o + jnp.dot((ps[t] / den).astype(BF16), v_refs[t][...].astype(BF16),
                        preferred_element_type=F32)
    hrow = lax.broadcasted_iota(jnp.int32, (N_HEADS, HEAD_DIM), 0)
    o_ref[...] = jnp.where(hrow // r == g, o, o_ref[...])


def nsa_sample_slc(q, top_idx, top_val, pool_k, pool_v, page_table, k_new, v_new, n_sel, q_pos):
    db, n_pages = page_table.shape
    n_top = min(NSA_TOP, n_sel)
    n_pool, page, g_kv, _ = pool_k.shape
    assert g_kv == NSA_KV
    half = page // NSA_SEL_LEN
    rows = NSA_SEL_LEN * g_kv
    pk = pool_k.reshape(n_pool * half, rows, HEAD_DIM)
    pv = pool_v.reshape(n_pool * half, rows, HEAD_DIM)
    idx = top_idx[:, :NSA_KV, :n_top].reshape(-1)
    ok = (top_val[:, :NSA_KV, :n_top] >= 0.0).astype(jnp.int32).reshape(-1)

    def blk_spec(t):
        def index_map(b, g, idx_ref, ok_ref, pt_ref):
            j = jnp.minimum(idx_ref[(b * NSA_KV + g) * n_top + t], n_sel - 2)
            return (pt_ref[b, j // half] * half + j % half, 0, 0)
        return pl.BlockSpec((None, rows, HEAD_DIM), index_map)

    row_spec = pl.BlockSpec((None, 1, HEAD_DIM), lambda b, g, *_: (b * NSA_KV + g, 0, 0))
    grid_spec = pltpu.PrefetchScalarGridSpec(
        num_scalar_prefetch=3,
        grid=(db, NSA_KV),
        in_specs=([pl.BlockSpec((None, N_HEADS, HEAD_DIM), lambda b, g, *_: (b, 0, 0))]
                  + [blk_spec(t) for t in range(n_top)] * 2 + [row_spec, row_spec]),
        out_specs=pl.BlockSpec((None, N_HEADS, HEAD_DIM), lambda b, g, *_: (b, 0, 0)),
    )
    return pl.pallas_call(
        functools.partial(_nsa_sample_slc_kernel, n_top=n_top, n_sel=n_sel, q_pos=q_pos),
        grid_spec=grid_spec,
        out_shape=jax.ShapeDtypeStruct((db, N_HEADS, HEAD_DIM), F32),
        compiler_params=_cparams(("parallel", "arbitrary")),
        name="nsa_sample_slc",
    )(idx, ok, page_table, q.reshape(db, N_HEADS, HEAD_DIM).astype(BF16),
      *([pk] * n_top), *([pv] * n_top),
      k_new.reshape(db * NSA_KV, 1, HEAD_DIM), v_new.reshape(db * NSA_KV, 1, HEAD_DIM))


def _rope_tables(pos):
    half = HEAD_DIM // 2
    inv = ROPE_THETA ** (-jnp.arange(half, dtype=F32) / half)
    ang = pos.astype(F32)[:, None] * inv[None, :]
    cos, sin = jnp.cos(ang), jnp.sin(ang)
    return jnp.concatenate([cos, cos], -1), jnp.concatenate([-sin, sin], -1)


def _pick(n, cands):
    for c in cands:
        if n % c == 0:
            return c
    return n


def _res_ln_rows(h16, w16_all, widx, x32, g, b, scale):
    tm = _pick(x32.shape[0], (256, 128))
    return res_ln(h16, w16_all, widx, x32, g, b, scale, tm, _pick(tm, (128,)))


def _ffn_step(xp, xs, w_in_all, w_out16_all, which, g, b):
    m = xp[0].shape[0]
    f = w_out16_all.shape[-2]
    hp, hs = glu(xp[1], xs[1], w_in_all, which, _pick(m, (2048, 1024, 512, 256, 128)), _pick(f, (512, 256, 128)))
    return (_res_ln_rows(hp, w_out16_all, which, xp[0], g, b, 0.5),
            _res_ln_rows(hs, w_out16_all, which, xs[0], g, b, 0.5))


def _project(x16, w16_all, widx, flags, tables, tn):
    m = x16.shape[0]
    tm = _pick(min(m, tables[0].shape[0]), (1024, 512, 256, 128))
    return proj(x16, w16_all, widx, len(flags) * tn, jnp.asarray(flags, jnp.int32), tables[0], tables[1], tm, tn)


def kernel(x_prompt, x_sample, cache_swa_k, cache_swa_v, cache_nsa_kc, cache_nsa_vc, cache_nsa_ks,
           cache_nsa_vs, cache_nsa_kw, cache_nsa_vw, cache_sb_k, cache_sb_v, page_table, ln_g, ln_b,
           ffn_w_in, ffn_w_out, swa_w_in, swa_sink, swa_w_o, nsa_w_in, nsa_cmp_a, nsa_cmp_pe,
           nsa_cmp_w1, nsa_cmp_w2, nsa_w_o, sb_w_in, sb_w_o):
    bsz, t, d = x_prompt.shape
    db, tn_new, _ = x_sample.shape
    assert tn_new == 1
    depth = ffn_w_in.shape[0]
    n_pages = page_table.shape[1]
    page = cache_nsa_kc.shape[2]
    past_len = n_pages * page
    hq = N_HEADS * HEAD_DIM

    tab_p = _rope_tables(jnp.arange(t, dtype=jnp.int32))
    tab_s = _rope_tables(jnp.full((db,), past_len, jnp.int32))

    xp32 = x_prompt.reshape(bsz * t, d)
    xs32 = x_sample.reshape(db, d)
    xp16, xs16 = xp32.astype(BF16), xs32.astype(BF16)

    ffn_w_out16 = ffn_w_out.astype(BF16)
    swa_w_in16, swa_w_o16 = swa_w_in.astype(BF16), swa_w_o.astype(BF16)
    nsa_w_in16, nsa_w_o16 = nsa_w_in.astype(BF16), nsa_w_o.astype(BF16)
    sb_w_in16, sb_w_o16 = sb_w_in.astype(BF16), sb_w_o.astype(BF16)

    outs = {k: [] for k in ("swa_kp", "swa_vp", "swa_ks", "swa_vs", "nkc_p", "nvc_p", "nks_p", "nvs_p",
                            "nkw_p", "nvw_p", "nkc_s", "nvc_s", "nks_s", "nvs_s", "nkw_s", "nvw_s",
                            "sbk_p", "sbv_p", "sbk_s", "sbv_s")}

    for l in range(depth):
        kind, j = l % 3, l // 3
        (xp32, xp16), (xs32, xs16) = _ffn_step((xp32, xp16), (xs32, xs16), ffn_w_in, ffn_w_out16, (l, 0),
                                               ln_g[l, 0], ln_b[l, 0])

        if kind == 0:
            kd = SWA_KV * HEAD_DIM
            flags = [1] * ((hq + kd) // 512) + [0] * (kd // 512)
            yp = _project(xp16, swa_w_in16, (j,), flags, tab_p, 512)
            ys = _project(xs16, swa_w_in16, (j,), flags, tab_s, 512)
            op = window_prompt_attn(yp, bsz, t, SWA_KV, 0, N_HEADS, N_HEADS + SWA_KV, SWA_WINDOW,
                                    swa_sink[j], BF16)
            keep = min(SWA_WINDOW, t)
            yp3 = yp.reshape(bsz, t, -1)
            outs["swa_kp"].append(yp3[:, t - keep:, hq:hq + kd].reshape(bsz, keep, SWA_KV, HEAD_DIM))
            outs["swa_vp"].append(yp3[:, t - keep:, hq + kd:].reshape(bsz, keep, SWA_KV, HEAD_DIM))
            k_new, v_new = ys[:, hq:hq + kd], ys[:, hq + kd:]
            k_buf, v_buf = cache_swa_k[j], cache_swa_v[j]
            osm = window_sample_attn(ys[:, :hq], k_new, v_new, k_buf, v_buf, past_len, SWA_WINDOW,
                                     swa_sink[j], BF16)
            wb = k_buf.shape[1]
            keep_s = min(SWA_WINDOW, past_len + 1)
            new4 = lambda x: x.reshape(db, 1, SWA_KV, HEAD_DIM)
            outs["swa_ks"].append(jnp.concatenate([k_buf, new4(k_new)], axis=1)[:, wb + 1 - keep_s:])
            outs["swa_vs"].append(jnp.concatenate([v_buf, new4(v_new)], axis=1)[:, wb + 1 - keep_s:])
            w_o16 = swa_w_o16
        elif kind == 1:
            kd = NSA_KV * HEAD_DIM
            main = hq + 6 * kd
            w_gate = jnp.pad(nsa_w_in16[j][:, main:], ((0, 0), (0, LANES - 3 * N_HEADS)))
            flags = [1] * (hq // 512) + [1, 0, 1, 0, 1, 0]
            a_k, a_v = nsa_cmp_a[j, 0], nsa_cmp_a[j, 1]
            mlp_k = (a_k, nsa_cmp_pe[j, 0], nsa_cmp_w1[j, 0], nsa_cmp_w2[j, 0])
            mlp_v = (a_v, nsa_cmp_pe[j, 1], nsa_cmp_w1[j, 1], nsa_cmp_w2[j, 1])
            yp = _project(xp16, nsa_w_in16, (j,), flags, tab_p, 512)
            gp = _project(xp16, w_gate, (), [0], tab_p, LANES)
            pp = _pick(t // 128, (8, 4, 2, 1))
            yp3 = yp.reshape(bsz, t, -1)
            seg = lambda c: yp3[:, :, hq + c * kd:hq + (c + 1) * kd].reshape(bsz, t, NSA_KV, HEAD_DIM)
            kc_p, vc_p = seg(0), seg(1)
            kcmp = cmp_mlp(*pool_prompt(kc_p, a_k, pp), *mlp_k)
            vcmp = cmp_mlp(*pool_prompt(vc_p, a_v, pp), *mlp_v)
            oc, osl = nsa_prompt_global(yp, kcmp, vcmp, bsz, t)
            ow = window_prompt_attn(yp, bsz, t, NSA_KV, 0, N_HEADS + 4 * NSA_KV, N_HEADS + 5 * NSA_KV,
                                    NSA_WINDOW, None, F32)
            op = nsa_combine(gp, oc, osl, ow, _pick(bsz * t, (512, 256, 128)))
            keep = min(NSA_WINDOW, t)
            outs["nkc_p"].append(kc_p)
            outs["nvc_p"].append(vc_p)
            outs["nks_p"].append(seg(2))
            outs["nvs_p"].append(seg(3))
            outs["nkw_p"].append(seg(4)[:, t - keep:])
            outs["nvw_p"].append(seg(5)[:, t - keep:])
            ys = _project(xs16, nsa_w_in16, (j,), flags, tab_s, 512)
            gs = _project(xs16, w_gate, (), [0], tab_s, LANES)
            sseg = lambda c: ys[:, hq + c * kd:hq + (c + 1) * kd]
            assert page == 128
            total = past_len + 1
            n_seg = total // NSA_CMP_STRIDE
            assert n_seg == n_pages * SEG_PER_PAGE
            n_cmp = n_seg - NSA_CMP_LEN // NSA_CMP_STRIDE + 1
            n_sel = -(-total // NSA_SEL_LEN)
            pps = _pick(n_pages, (32, 16, 8, 4, 2, 1))
            kcmp_s = cmp_mlp(*pool_paged(cache_nsa_kc[j], page_table, a_k, pps), *mlp_k)
            vcmp_s = cmp_mlp(*pool_paged(cache_nsa_vc[j], page_table, a_v, pps), *mlp_v)
            oc_s, top_idx, top_val = nsa_sample_cmp(ys[:, :hq], kcmp_s, vcmp_s, n_cmp, n_sel, past_len)
            osl_s = nsa_sample_slc(ys[:, :hq], top_idx, top_val, cache_nsa_ks[j], cache_nsa_vs[j],
                                   page_table, sseg(2), sseg(3), n_sel, past_len)
            kw_buf, vw_buf = cache_nsa_kw[j], cache_nsa_vw[j]
            ow_s = window_sample_attn(ys[:, :hq], sseg(4), sseg(5), kw_buf, vw_buf, past_len, NSA_WINDOW,
                                      None, F32)
            osm = nsa_combine(gs, oc_s.reshape(db, hq), osl_s.reshape(db, hq), ow_s, db)
            wb = kw_buf.shape[1]
            keep_s = min(NSA_WINDOW, past_len + 1)
            new4 = lambda c: sseg(c).reshape(db, 1, NSA_KV, HEAD_DIM)
            for name, c in (("nkc_s", 0), ("nvc_s", 1), ("nks_s", 2), ("nvs_s", 3)):
                outs[name].append(new4(c))
            outs["nkw_s"].append(jnp.concatenate([kw_buf, new4(4)], axis=1)[:, wb + 1 - keep_s:])
            outs["nvw_s"].append(jnp.concatenate([vw_buf, new4(5)], axis=1)[:, wb + 1 - keep_s:])
            w_o16 = nsa_w_o16
        else:
            kd = SB_KV * HEAD_DIM
            flags = [0] * (sb_w_in.shape[-1] // 512)
            yp = _project(xp16, sb_w_in16, (j,), flags, tab_p, 512)
            ys = _project(xs16, sb_w_in16, (j,), flags, tab_s, 512)
            op = sb_prompt_attn(yp, bsz, t, BF16)
            yp3 = yp.reshape(bsz, t, -1)
            outs["sbk_p"].append(yp3[:, :, hq:hq + kd].reshape(bsz, t, SB_KV, HEAD_DIM))
            outs["sbv_p"].append(yp3[:, :, hq + kd:].reshape(bsz, t, SB_KV, HEAD_DIM))
            osm = sb_sample_attn(ys[:, :hq], cache_sb_k[j], cache_sb_v[j], page_table,
                                 _pick(n_pages, (8, 4, 2, 1)), BF16)
            outs["sbk_s"].append(ys[:, hq:hq + kd].reshape(db, 1, SB_KV, HEAD_DIM))
            outs["sbv_s"].append(ys[:, hq + kd:].reshape(db, 1, SB_KV, HEAD_DIM))
            w_o16 = sb_w_o16

        xp32, xp16 = _res_ln_rows(op, w_o16, (j,), xp32, ln_g[l, 1], ln_b[l, 1], 1.0)
        xs32, xs16 = _res_ln_rows(osm, w_o16, (j,), xs32, ln_g[l, 1], ln_b[l, 1], 1.0)
        (xp32, xp16), (xs32, xs16) = _ffn_step((xp32, xp16), (xs32, xs16), ffn_w_in, ffn_w_out16, (l, 1),
                                               ln_g[l, 2], ln_b[l, 2])

    st = lambda name: jnp.stack(outs[name])
    return (xp32.reshape(bsz, t, d), xs32.reshape(db, 1, d),
            st("swa_kp"), st("swa_vp"),
            st("nkc_p"), st("nvc_p"), st("nks_p"), st("nvs_p"), st("nkw_p"), st("nvw_p"),
            st("sbk_p"), st("sbv_p"),
            st("swa_ks"), st("swa_vs"),
            st("nkc_s"), st("nvc_s"), st("nks_s"), st("nvs_s"), st("nkw_s"), st("nvw_s"),
            st("sbk_s"), st("sbv_s"))
```

```python
import functools

import numpy as np
import jax
import jax.numpy as jnp
from jax import lax
from jax.experimental import pallas as pl
from jax.experimental.pallas import tpu as pltpu

F32 = jnp.float32
BF16 = jnp.bfloat16

HEAD_DIM = 128
N_HEADS = 16
Q_BLOCK = 128
SWA_KV = 4
SWA_WINDOW = 128
NSA_KV = 4
NSA_CMP_LEN = 32
NSA_CMP_STRIDE = 16
NSA_SEL_LEN = 64
NSA_TOP = 16
NSA_WINDOW = 512
SEL_FORCE = 1e4
SB_KV = 8
ROPE_THETA = 10000.0
LN_EPS = 1e-5
DEPTH = 4
DN_ALPHA = (2 * DEPTH) ** 0.25
SCALE = HEAD_DIM ** -0.5
NEG_INF = float("-inf")

LANES = 128
SUBLANES = 8
VMEM_LIMIT = 56 * 1024 * 1024


def _cparams(sem):
    return pltpu.CompilerParams(dimension_semantics=sem, vmem_limit_bytes=VMEM_LIMIT)


def _nt(a, b):
    return lax.dot_general(a, b, (((1,), (1,)), ((), ())), preferred_element_type=F32)


def _split3(x):
    hi = x.astype(BF16)
    r = x - hi.astype(F32)
    mid = r.astype(BF16)
    lo = (r - mid.astype(F32)).astype(BF16)
    return hi, mid, lo


def _dot3(x, u):
    hi, mid, lo = _split3(x)
    return (jnp.dot(hi, u, preferred_element_type=F32)
            + jnp.dot(mid, u, preferred_element_type=F32)
            + jnp.dot(lo, u, preferred_element_type=F32))


def _dot2(x, u):
    hi = x.astype(BF16)
    mid = (x - hi.astype(F32)).astype(BF16)
    return jnp.dot(hi, u, preferred_element_type=F32) + jnp.dot(mid, u, preferred_element_type=F32)


def _dot3r(u, x):
    hi, mid, lo = _split3(x)
    return (jnp.dot(u, hi, preferred_element_type=F32)
            + jnp.dot(u, mid, preferred_element_type=F32)
            + jnp.dot(u, lo, preferred_element_type=F32))


def _stack_heads(q, r):
    return jnp.concatenate([q[:, h * HEAD_DIM:(h + 1) * HEAD_DIM] for h in range(r)], axis=0)


def _tile_rows(x, r):
    return jnp.concatenate([x] * r, axis=0) if r > 1 else x


def _silu_mul(g, u):
    return g * jax.nn.sigmoid(g) * u


def _glu_kernel(x_ref, xs_ref, wg_ref, wu_ref, o_ref, os_ref, wg16_ref, wu16_ref):
    @pl.when(pl.program_id(1) == 0)
    def _():
        wg16_ref[...] = wg_ref[...].astype(BF16)
        wu16_ref[...] = wu_ref[...].astype(BF16)
        xs = xs_ref[...]
        gs = jnp.dot(xs, wg16_ref[...], preferred_element_type=F32)
        us = jnp.dot(xs, wu16_ref[...], preferred_element_type=F32)
        os_ref[...] = _silu_mul(gs, us).astype(os_ref.dtype)

    x = x_ref[...]
    g = jnp.dot(x, wg16_ref[...], preferred_element_type=F32)
    u = jnp.dot(x, wu16_ref[...], preferred_element_type=F32)
    o_ref[...] = _silu_mul(g, u).astype(o_ref.dtype)


def glu(xb, xsb, w_in_all, which, tm, tn):
    m, k = xb.shape
    ms = xsb.shape[0]
    f = w_in_all.shape[-1] // 2
    nj = f // tn
    l0, l1 = which
    return pl.pallas_call(
        _glu_kernel,
        grid=(nj, m // tm),
        in_specs=[pl.BlockSpec((tm, k), lambda j, i: (i, 0)),
                  pl.BlockSpec((ms, k), lambda j, i: (0, 0)),
                  pl.BlockSpec((None, None, k, tn), lambda j, i: (l0, l1, 0, j)),
                  pl.BlockSpec((None, None, k, tn), lambda j, i: (l0, l1, 0, j + nj))],
        out_specs=[pl.BlockSpec((tm, tn), lambda j, i: (i, j)),
                   pl.BlockSpec((ms, tn), lambda j, i: (0, j))],
        out_shape=[jax.ShapeDtypeStruct((m, f), BF16), jax.ShapeDtypeStruct((ms, f), BF16)],
        scratch_shapes=[pltpu.VMEM((k, tn), BF16), pltpu.VMEM((k, tn), BF16)],
        compiler_params=_cparams(("parallel", "arbitrary")),
        name="glu",
    )(xb, xsb, w_in_all, w_in_all)


def _res_ln_math(h, w, x, g, b, scale):
    y = DN_ALPHA * x + scale * jnp.dot(h, w, preferred_element_type=F32)
    mu = jnp.mean(y, axis=-1, keepdims=True)
    d = y - mu
    var = jnp.mean(d * d, axis=-1, keepdims=True)
    return d * lax.rsqrt(var + LN_EPS) * g + b


def _resln_kernel(h_ref, hs_ref, w_ref, x_ref, xs_ref, g_ref, b_ref, o32_ref, o16_ref, os32_ref, os16_ref,
                  *, scale, sub):
    @pl.when(pl.program_id(0) == 0)
    def _():
        outs = _res_ln_math(hs_ref[...], w_ref[...], xs_ref[...], g_ref[...], b_ref[...], scale)
        os32_ref[...] = outs
        os16_ref[...] = outs.astype(BF16)

    for u in range(h_ref.shape[0] // sub):
        rows = slice(u * sub, (u + 1) * sub)
        out = _res_ln_math(h_ref[rows, :], w_ref[...], x_ref[rows, :], g_ref[...], b_ref[...], scale)
        o32_ref[rows, :] = out
        o16_ref[rows, :] = out.astype(BF16)


def res_ln(hb, hsb, w_all, widx, x, xs, g, b, scale, tm, sub):
    m, kdim = hb.shape
    ms = hsb.shape[0]
    d = w_all.shape[-1]
    lead = (None,) * len(widx)
    tile = lambda w: pl.BlockSpec((tm, w), lambda i: (i, 0))
    whole = lambda r, w: pl.BlockSpec((r, w), lambda i: (0, 0))
    return pl.pallas_call(
        functools.partial(_resln_kernel, scale=scale, sub=sub),
        grid=(m // tm,),
        in_specs=[tile(kdim), whole(ms, kdim),
                  pl.BlockSpec(lead + (kdim, d), lambda i: tuple(widx) + (0, 0), pipeline_mode=pl.Buffered(1)),
                  tile(d), whole(ms, d), whole(1, d), whole(1, d)],
        out_specs=[tile(d), tile(d), whole(ms, d), whole(ms, d)],
        out_shape=[jax.ShapeDtypeStruct((m, d), F32), jax.ShapeDtypeStruct((m, d), BF16),
                   jax.ShapeDtypeStruct((ms, d), F32), jax.ShapeDtypeStruct((ms, d), BF16)],
        compiler_params=_cparams(("arbitrary",)),
        name="res_ln",
    )(hb, hsb, w_all, x, xs, g.reshape(1, d), b.reshape(1, d))


def _proj_kernel(flag_ref, x_ref, w_ref, c_ref, s_ref, o_ref, *, nh):
    j = pl.program_id(1)
    y = jnp.dot(x_ref[...], w_ref[...], preferred_element_type=F32)

    @pl.when(flag_ref[j] == 0)
    def _():
        o_ref[...] = y

    @pl.when(flag_ref[j] != 0)
    def _():
        c = c_ref[...]
        s = s_ref[...]
        for h in range(nh):
            yh = y[:, h * HEAD_DIM:(h + 1) * HEAD_DIM]
            o_ref[:, h * HEAD_DIM:(h + 1) * HEAD_DIM] = yh * c + pltpu.roll(yh, HEAD_DIM // 2, 1) * s


def proj(xb, w_all, widx, n, rope_flags, cos_t, sin_t, tm, tn):
    m, k = xb.shape
    lead = (None,) * len(widx)
    tbl_blocks = cos_t.shape[0] // tm
    grid_spec = pltpu.PrefetchScalarGridSpec(
        num_scalar_prefetch=1,
        grid=(m // tm, n // tn),
        in_specs=[pl.BlockSpec((tm, k), lambda i, j, f: (i, 0)),
                  pl.BlockSpec(lead + (k, tn), lambda i, j, f: tuple(widx) + (0, j)),
                  pl.BlockSpec((tm, HEAD_DIM), lambda i, j, f: (i % tbl_blocks, 0)),
                  pl.BlockSpec((tm, HEAD_DIM), lambda i, j, f: (i % tbl_blocks, 0))],
        out_specs=pl.BlockSpec((tm, tn), lambda i, j, f: (i, j)),
    )
    return pl.pallas_call(
        functools.partial(_proj_kernel, nh=tn // HEAD_DIM),
        grid_spec=grid_spec,
        out_shape=jax.ShapeDtypeStruct((m, n), F32),
        compiler_params=_cparams(("parallel", "arbitrary")),
        name="proj",
    )(rope_flags, xb, w_all, cos_t, sin_t)


def _win_kernel(*refs, window, r, qb, has_sink):
    if has_sink:
        sink_ref, q_ref, k_ref, v_ref, o_ref = refs
    else:
        q_ref, k_ref, v_ref, o_ref = refs
    g = pl.program_id(1)
    kl = window + Q_BLOCK
    if has_sink:
        s = jnp.concatenate([jnp.full((Q_BLOCK, 1), sink_ref[g * r + h], F32) for h in range(r)], axis=0)
    for u in range(qb):
        i = pl.program_id(2) * qb + u
        rows = slice(u * Q_BLOCK, (u + 1) * Q_BLOCK)
        start = pl.multiple_of(jnp.maximum(i * Q_BLOCK - window, 0), Q_BLOCK)
        kb = k_ref[pl.ds(start, kl), :].astype(BF16)
        vb = v_ref[pl.ds(start, kl), :].astype(BF16)
        qs = _stack_heads(q_ref[rows, :], r).astype(BF16)
        logits = _nt(qs, kb) * SCALE
        q_pos = i * Q_BLOCK + lax.broadcasted_iota(jnp.int32, (Q_BLOCK, kl), 0)
        k_pos = start + lax.broadcasted_iota(jnp.int32, (Q_BLOCK, kl), 1)
        rel = q_pos - k_pos
        mask = _tile_rows((rel >= 0) & (rel <= window), r)
        logits = jnp.where(mask, logits, NEG_INF)
        m = jnp.max(logits, axis=-1, keepdims=True)
        if has_sink:
            m = jnp.maximum(m, s)
        p = jnp.exp(logits - m)
        den = jnp.sum(p, axis=-1, keepdims=True)
        if has_sink:
            den = den + jnp.exp(s - m)
        o = jnp.dot((p / den).astype(BF16), vb, preferred_element_type=F32)
        for h in range(r):
            o_ref[rows, h * HEAD_DIM:(h + 1) * HEAD_DIM] = o[h * Q_BLOCK:(h + 1) * Q_BLOCK].astype(o_ref.dtype)


def window_prompt_attn(y, bsz, t, g_kv, q_col, k_col, v_col, window, sink, out_dtype):
    r = N_HEADS // g_kv
    assert window + Q_BLOCK <= t
    rw = r * HEAD_DIM
    qb = _pick(t // Q_BLOCK, (8, 4, 2, 1))
    nb = t // (Q_BLOCK * qb)
    y3 = y.reshape(bsz, t, y.shape[1])
    in_specs = [pl.BlockSpec((None, Q_BLOCK * qb, rw), lambda b, g, i: (b, i, q_col // r + g)),
                pl.BlockSpec((None, t, HEAD_DIM), lambda b, g, i: (b, 0, k_col + g)),
                pl.BlockSpec((None, t, HEAD_DIM), lambda b, g, i: (b, 0, v_col + g))]
    args = [y3, y3, y3]
    if sink is not None:
        in_specs = [pl.BlockSpec(memory_space=pltpu.SMEM)] + in_specs
        args = [sink] + args
    out = pl.pallas_call(
        functools.partial(_win_kernel, window=window, r=r, qb=qb, has_sink=sink is not None),
        grid=(bsz, g_kv, nb),
        in_specs=in_specs,
        out_specs=pl.BlockSpec((None, Q_BLOCK * qb, rw), lambda b, g, i: (b, i, g)),
        out_shape=jax.ShapeDtypeStruct((bsz, t, N_HEADS * HEAD_DIM), out_dtype),
        compiler_params=_cparams(("parallel", "parallel", "arbitrary")),
        name="window_prompt",
    )(*args)
    return out.reshape(bsz * t, N_HEADS * HEAD_DIM)


SB_UNROLL = 3
SB_CUTOFF = -104.0


def _softplus_neg_abs(z):
    return jnp.log(1.0 + jnp.exp(-jnp.abs(z)))


def _sb_kernel(q_ref, k_ref, v_ref, o_ref, *, r, qb):
    rq = r * Q_BLOCK
    row = lax.broadcasted_iota(jnp.int32, (Q_BLOCK, Q_BLOCK), 0)
    col = lax.broadcasted_iota(jnp.int32, (Q_BLOCK, Q_BLOCK), 1)
    later_sel = (row > col).astype(BF16)
    for blk in range(qb):
        i = pl.program_id(2) * qb + blk
        rows = slice(blk * Q_BLOCK, (blk + 1) * Q_BLOCK)
        qs = _stack_heads(q_ref[rows, :], r).astype(BF16)
        n_it = (i + SB_UNROLL) // SB_UNROLL

        def cond(carry, n_it=n_it):
            t, live, _, _ = carry
            return jnp.logical_and(t < n_it, live > 0)

        def body(carry, i=i, qs=qs):
            t, _, acc, cs = carry
            for u in range(SB_UNROLL):
                j = i - (t * SB_UNROLL + u)
                q_lim = jnp.where(j >= 0, i * Q_BLOCK, -Q_BLOCK)
                off = pl.multiple_of(jnp.maximum(j, 0) * Q_BLOCK, Q_BLOCK)
                kb = k_ref[pl.ds(off, Q_BLOCK), :].astype(BF16)
                vb = v_ref[pl.ds(off, Q_BLOCK), :].astype(BF16)
                z = _nt(qs, kb) * SCALE
                causal = _tile_rows((off + col) < (q_lim + row), r)
                sp = _softplus_neg_abs(z)
                log_beta = jnp.minimum(z, 0.0) - sp
                log_keep = jnp.where(causal, jnp.minimum(-z, 0.0) - sp, 0.0)
                later = _dot2(log_keep, later_sel) + cs
                a = jnp.where(causal, jnp.exp(log_beta + later), 0.0)
                acc = acc + jnp.dot(a.astype(BF16), vb, preferred_element_type=F32)
                cs = cs + jnp.sum(log_keep, axis=-1, keepdims=True)
            live = jnp.where(jnp.max(cs) > SB_CUTOFF, 1, 0)
            return t + 1, live, acc, cs

        init = (jnp.int32(0), jnp.int32(1), jnp.zeros((rq, HEAD_DIM), F32), jnp.zeros((rq, 1), F32))
        _, _, acc, _ = lax.while_loop(cond, body, init)
        for h in range(r):
            o_ref[rows, h * HEAD_DIM:(h + 1) * HEAD_DIM] = acc[h * Q_BLOCK:(h + 1) * Q_BLOCK].astype(o_ref.dtype)


def sb_prompt_attn(y, bsz, t, out_dtype):
    r = N_HEADS // SB_KV
    qb = _pick(t // Q_BLOCK, (2, 1))
    nb = t // (Q_BLOCK * qb)
    rw = r * HEAD_DIM
    y3 = y.reshape(bsz, t, y.shape[1])
    out = pl.pallas_call(
        functools.partial(_sb_kernel, r=r, qb=qb),
        grid=(bsz, SB_KV, nb),
        in_specs=[pl.BlockSpec((None, Q_BLOCK * qb, rw), lambda b, g, i: (b, i, g)),
                  pl.BlockSpec((None, t, HEAD_DIM), lambda b, g, i: (b, 0, N_HEADS + g)),
                  pl.BlockSpec((None, t, HEAD_DIM), lambda b, g, i: (b, 0, N_HEADS + SB_KV + g))],
        out_specs=pl.BlockSpec((None, Q_BLOCK * qb, rw), lambda b, g, i: (b, i, g)),
        out_shape=jax.ShapeDtypeStruct((bsz, t, N_HEADS * HEAD_DIM), out_dtype),
        compiler_params=_cparams(("parallel", "parallel", "arbitrary")),
        name="sb_prompt",
    )(y3, y3, y3)
    return out.reshape(bsz * t, N_HEADS * HEAD_DIM)


SEG_PER_PAGE = 128 // NSA_CMP_STRIDE


def _pool_kernel(*refs, pp, g_kv, prefetch):
    refs = refs[prefetch:]
    x_refs = refs[:pp]
    w0_ref, w1_ref, u0_ref, u1_ref = refs[pp:]
    vps = NSA_CMP_STRIDE * g_kv // SUBLANES
    sub = lax.broadcasted_iota(jnp.int32, (SUBLANES, HEAD_DIM), 0)
    for s_i, x_ref in enumerate(x_refs):
        rows = slice(s_i * SEG_PER_PAGE, (s_i + 1) * SEG_PER_PAGE)
        tiles = [[jnp.zeros((SEG_PER_PAGE, HEAD_DIM), F32) for _ in range(g_kv)] for _ in range(2)]
        for n in range(SEG_PER_PAGE):
            accs = [jnp.zeros((SUBLANES, HEAD_DIM), F32), jnp.zeros((SUBLANES, HEAD_DIM), F32)]
            for v in range(vps):
                xv = x_ref[(n * vps + v) * SUBLANES:(n * vps + v + 1) * SUBLANES, :]
                accs[0] = accs[0] + xv * w0_ref[v]
                accs[1] = accs[1] + xv * w1_ref[v]
            for half in range(2):
                acc = accs[half]
                sh = g_kv
                while sh < SUBLANES:
                    acc = acc + pltpu.roll(acc, sh, 0)
                    sh *= 2
                for g in range(g_kv):
                    k = (n - g) % SUBLANES
                    moved = pltpu.roll(acc, k, 0) if k else acc
                    tiles[half][g] = jnp.where(sub == n, moved, tiles[half][g])
        for g in range(g_kv):
            u0_ref[rows, g * HEAD_DIM:(g + 1) * HEAD_DIM] = tiles[0][g]
            u1_ref[rows, g * HEAD_DIM:(g + 1) * HEAD_DIM] = tiles[1][g]


def _pool_weights(a, g_kv):
    ppv = SUBLANES // g_kv
    a_r = a.reshape(NSA_CMP_LEN // NSA_CMP_STRIDE, NSA_CMP_STRIDE // ppv, ppv, 1, HEAD_DIM)
    a_r = jnp.broadcast_to(a_r, a_r.shape[:3] + (g_kv, HEAD_DIM))
    a_r = a_r.reshape(2, NSA_CMP_STRIDE // ppv, SUBLANES, HEAD_DIM)
    return a_r[0], a_r[1]


def pool_prompt(x, a, pp):
    bsz, t, g_kv, _ = x.shape
    gw = g_kv * HEAD_DIM
    n_pages = t // 128
    xr = x.reshape(bsz, t * g_kv, HEAD_DIM)
    a0, a1 = _pool_weights(a, g_kv)
    in_specs = [pl.BlockSpec((None, 128 * g_kv, HEAD_DIM), lambda b, p, s=s: (b, p * pp + s, 0))
                for s in range(pp)]
    in_specs += [pl.BlockSpec(a0.shape, lambda b, p: (0, 0, 0))] * 2
    seg = pp * SEG_PER_PAGE
    return pl.pallas_call(
        functools.partial(_pool_kernel, pp=pp, g_kv=g_kv, prefetch=0),
        grid=(bsz, n_pages // pp),
        in_specs=in_specs,
        out_specs=[pl.BlockSpec((None, seg, gw), lambda b, p: (b, p, 0))] * 2,
        out_shape=[jax.ShapeDtypeStruct((bsz, t // NSA_CMP_STRIDE, gw), F32)] * 2,
        compiler_params=_cparams(("parallel", "arbitrary")),
        name="pool_prompt",
    )(*([xr] * pp), a0, a1)


def pool_paged(pool, page_table, a, pp):
    db, n_pages = page_table.shape
    n_pool, page, g_kv, _ = pool.shape
    assert page == 128
    gw = g_kv * HEAD_DIM
    pr = pool.reshape(n_pool, page * g_kv, HEAD_DIM)
    a0, a1 = _pool_weights(a, g_kv)
    in_specs = [pl.BlockSpec((None, page * g_kv, HEAD_DIM), lambda b, p, pt, s=s: (pt[b, p * pp + s], 0, 0))
                for s in range(pp)]
    in_specs += [pl.BlockSpec(a0.shape, lambda b, p, pt: (0, 0, 0))] * 2
    seg = pp * SEG_PER_PAGE
    grid_spec = pltpu.PrefetchScalarGridSpec(
        num_scalar_prefetch=1,
        grid=(db, n_pages // pp),
        in_specs=in_specs,
        out_specs=[pl.BlockSpec((None, seg, gw), lambda b, p, pt: (b, p, 0))] * 2,
    )
    return pl.pallas_call(
        functools.partial(_pool_kernel, pp=pp, g_kv=g_kv, prefetch=1),
        grid_spec=grid_spec,
        out_shape=[jax.ShapeDtypeStruct((db, n_pages * SEG_PER_PAGE, gw), F32)] * 2,
        compiler_params=_cparams(("parallel", "arbitrary")),
        name="pool_paged",
    )(page_table, *([pr] * pp), a0, a1)


def _cmp_mlp_kernel(u0_ref, u1_ref, a_ref, pe_ref, w1_ref, w2_ref, o_ref):
    n_seg = u0_ref.shape[0]
    c = jnp.sum(a_ref[...] * pe_ref[...], axis=0, keepdims=True)
    w1 = w1_ref[...].astype(BF16)
    w2 = w2_ref[...].astype(BF16)
    for g in range(u0_ref.shape[1] // HEAD_DIM):
        cols = slice(g * HEAD_DIM, (g + 1) * HEAD_DIM)
        u = u0_ref[:, cols] + pltpu.roll(u1_ref[:, cols], n_seg - 1, 0)
        u = u + c
        h = jax.nn.gelu(jnp.dot(u.astype(BF16), w1, preferred_element_type=F32))
        o_ref[:, cols] = jnp.dot(h.astype(BF16), w2, preferred_element_type=F32)


def cmp_mlp(u0, u1, a, pe, w1, w2):
    bx, n_seg, gw = u0.shape
    full = lambda arr: pl.BlockSpec(arr.shape, lambda b: (0,) * arr.ndim)
    rows = pl.BlockSpec((None, n_seg, gw), lambda b: (b, 0, 0))
    return pl.pallas_call(
        _cmp_mlp_kernel,
        grid=(bx,),
        in_specs=[rows, rows, full(a), full(pe), full(w1), full(w2)],
        out_specs=rows,
        out_shape=jax.ShapeDtypeStruct((bx, n_seg, gw), F32),
        compiler_params=_cparams(("parallel",)),
        name="cmp_mlp",
    )(u0, u1, a, pe, w1, w2)


def _cover_matrix(n_rows, n_cmp, n_cols, n_sel):
    cs = np.arange(n_rows) * NSA_CMP_STRIDE
    ss = np.arange(n_cols) * NSA_SEL_LEN
    m = (cs[:, None] < ss[None, :] + NSA_SEL_LEN) & (cs[:, None] + NSA_CMP_LEN > ss[None, :])
    m = m & (np.arange(n_rows)[:, None] < n_cmp) & (np.arange(n_cols)[None, :] < n_sel)
    return jnp.asarray(m, BF16)


def _masked_softmax(logits, mask):
    logits = jnp.where(mask, logits, NEG_INF)
    m = jnp.max(logits, axis=-1, keepdims=True)
    m = jnp.where(m == NEG_INF, 0.0, m)
    e = jnp.exp(logits - m)
    s = jnp.sum(e, axis=-1, keepdims=True)
    return e / jnp.where(s > 0, s, 1.0)


def _nsa_prompt_cmp_kernel(q_ref, kc_ref, vc_ref, cover_ref, ocmp_ref, sel_ref, *, qb, **kw):
    for blk in range(qb):
        rows = slice(blk * Q_BLOCK, (blk + 1) * Q_BLOCK)
        _nsa_prompt_cmp_block(pl.program_id(2) * qb + blk, q_ref.at[rows, :], kc_ref, vc_ref, cover_ref,
                              ocmp_ref.at[rows, :], sel_ref.at[rows, :], **kw)


def _nsa_prompt_cmp_block(i, q_ref, kc_ref, vc_ref, cover_ref, ocmp_ref, sel_ref, *, r, n_cmp, n_sel, n_top):
    qs = _stack_heads(q_ref[...], r).astype(BF16)
    nc = kc_ref.shape[0]

    lc = _nt(qs, kc_ref[...].astype(BF16)) * SCALE
    qp_c = i * Q_BLOCK + lax.broadcasted_iota(jnp.int32, (Q_BLOCK, nc), 0)
    n_c = lax.broadcasted_iota(jnp.int32, (Q_BLOCK, nc), 1)
    cmask = _tile_rows((n_c * NSA_CMP_STRIDE + NSA_CMP_LEN - 1 <= qp_c) & (n_c < n_cmp), r)
    pc = _masked_softmax(lc, cmask)
    ocmp = jnp.dot(pc.astype(BF16), vc_ref[...].astype(BF16), preferred_element_type=F32)
    for h in range(r):
        ocmp_ref[:, h * HEAD_DIM:(h + 1) * HEAD_DIM] = ocmp[h * Q_BLOCK:(h + 1) * Q_BLOCK]

    p_sum = pc[0:Q_BLOCK]
    for h in range(1, r):
        p_sum = p_sum + pc[h * Q_BLOCK:(h + 1) * Q_BLOCK]
    nj = -(-n_sel // SUBLANES) * SUBLANES
    cover_t = cover_ref[...]
    hi, mid, lo = _split3(p_sum)
    imp = (_nt(cover_t, hi) + _nt(cover_t, mid) + _nt(cover_t, lo))[:nj]
    qp = i * Q_BLOCK + lax.broadcasted_iota(jnp.int32, (nj, Q_BLOCK), 1)
    jj = lax.broadcasted_iota(jnp.int32, (nj, Q_BLOCK), 0)
    cur = qp // NSA_SEL_LEN
    vis = jj * NSA_SEL_LEN <= qp
    forced = (jj == 0) | (jj == cur) | (jj == cur - 1)
    score = jnp.where(vis, jnp.where(forced, SEL_FORCE, imp), -1.0)
    score = jnp.where(jj < n_sel, score, -2.0)
    rank = jnp.zeros((nj, Q_BLOCK), F32)
    for t in range(n_sel):
        st = score[t:t + 1, :]
        beats = (st > score) | ((st == score) & (jj > t))
        rank = rank + jnp.where(beats, 1.0, 0.0)
    sel_t = jnp.where((rank < n_top) & (score >= 0.0), 1.0, 0.0)
    if nj < LANES:
        sel_t = jnp.concatenate([sel_t, jnp.zeros((LANES - nj, Q_BLOCK), F32)], axis=0)
    sel_ref[...] = sel_t.T.astype(sel_ref.dtype)


def _nsa_prompt_slc_kernel(q_ref, sel_ref, ks_ref, vs_ref, oslc_ref, *, qb, **kw):
    for blk in range(qb):
        rows = slice(blk * Q_BLOCK, (blk + 1) * Q_BLOCK)
        _nsa_prompt_slc_block(pl.program_id(2) * qb + blk, q_ref.at[rows, :], sel_ref.at[rows, :], ks_ref,
                              vs_ref, oslc_ref.at[rows, :], **kw)


def _nsa_prompt_slc_block(i, q_ref, sel_ref, ks_ref, vs_ref, oslc_ref, *, r, kc):
    rq = r * Q_BLOCK
    qs = _stack_heads(q_ref[...], r).astype(BF16)
    sel = sel_ref[...]
    row = lax.broadcasted_iota(jnp.int32, (Q_BLOCK, kc), 0)
    col = lax.broadcasted_iota(jnp.int32, (Q_BLOCK, kc), 1)

    def body(kb, carry):
        m, l, acc = carry
        off = pl.multiple_of(kb * kc, kc)
        k = ks_ref[pl.ds(off, kc), :].astype(BF16)
        v = vs_ref[pl.ds(off, kc), :].astype(BF16)
        s = _nt(qs, k) * SCALE
        expand = jnp.where(row == (off + col) // NSA_SEL_LEN, 1.0, 0.0).astype(BF16)
        picked = jnp.dot(sel, expand, preferred_element_type=F32) > 0.5
        mask = _tile_rows(picked & ((off + col) <= (i * Q_BLOCK + row)), r)
        s = jnp.where(mask, s, NEG_INF)
        m_new = jnp.maximum(m, jnp.max(s, axis=-1, keepdims=True))
        m_safe = jnp.where(m_new == NEG_INF, 0.0, m_new)
        p = jnp.exp(s - m_safe)
        alpha = jnp.exp(m - m_safe)
        l = alpha * l + jnp.sum(p, axis=-1, keepdims=True)
        acc = alpha * acc + jnp.dot(p.astype(BF16), v, preferred_element_type=F32)
        return m_new, l, acc

    init = (jnp.full((rq, 1), NEG_INF, F32), jnp.zeros((rq, 1), F32), jnp.zeros((rq, HEAD_DIM), F32))
    n_it = ((i + 1) * Q_BLOCK + kc - 1) // kc
    _, l, acc = lax.fori_loop(0, n_it, body, init)
    oslc = acc / jnp.where(l > 0, l, 1.0)
    for h in range(r):
        oslc_ref[:, h * HEAD_DIM:(h + 1) * HEAD_DIM] = oslc[h * Q_BLOCK:(h + 1) * Q_BLOCK]


def nsa_prompt_global(y, kcmp, vcmp, bsz, t):
    r = N_HEADS // NSA_KV
    qb = _pick(t // Q_BLOCK, (2, 1))
    nb = t // (Q_BLOCK * qb)
    rw = r * HEAD_DIM
    n_seg = t // NSA_CMP_STRIDE
    n_cmp = n_seg - NSA_CMP_LEN // NSA_CMP_STRIDE + 1
    n_sel = -(-t // NSA_SEL_LEN)
    assert n_sel <= LANES and t % NSA_SEL_LEN == 0
    n_top = min(NSA_TOP, n_sel)
    y3 = y.reshape(bsz, t, y.shape[1])
    cover = _cover_matrix(n_seg, n_cmp, LANES, n_sel).T
    ks_col = N_HEADS + 2 * NSA_KV
    vs_col = N_HEADS + 3 * NSA_KV
    o_shape = jax.ShapeDtypeStruct((bsz, t, N_HEADS * HEAD_DIM), F32)
    q_spec = pl.BlockSpec((None, Q_BLOCK * qb, rw), lambda b, g, i: (b, i, g))
    sel_spec = pl.BlockSpec((None, None, Q_BLOCK * qb, LANES), lambda b, g, i: (b, g, i, 0))
    ocmp, sel = pl.pallas_call(
        functools.partial(_nsa_prompt_cmp_kernel, qb=qb, r=r, n_cmp=n_cmp, n_sel=n_sel, n_top=n_top),
        grid=(bsz, NSA_KV, nb),
        in_specs=[q_spec,
                  pl.BlockSpec((None, n_seg, HEAD_DIM), lambda b, g, i: (b, 0, g)),
                  pl.BlockSpec((None, n_seg, HEAD_DIM), lambda b, g, i: (b, 0, g)),
                  pl.BlockSpec(cover.shape, lambda b, g, i: (0, 0))],
        out_specs=[q_spec, sel_spec],
        out_shape=[o_shape, jax.ShapeDtypeStruct((bsz, NSA_KV, t, LANES), BF16)],
        compiler_params=_cparams(("parallel", "parallel", "arbitrary")),
        name="nsa_prompt_cmp",
    )(y3, kcmp, vcmp, cover)
    oslc = pl.pallas_call(
        functools.partial(_nsa_prompt_slc_kernel, qb=qb, r=r, kc=_pick(t, (512, 256, 128))),
        grid=(bsz, NSA_KV, nb),
        in_specs=[q_spec, sel_spec,
                  pl.BlockSpec((None, t, HEAD_DIM), lambda b, g, i: (b, 0, ks_col + g)),
                  pl.BlockSpec((None, t, HEAD_DIM), lambda b, g, i: (b, 0, vs_col + g))],
        out_specs=q_spec,
        out_shape=o_shape,
        compiler_params=_cparams(("parallel", "parallel", "arbitrary")),
        name="nsa_prompt_slc",
    )(y3, sel, y3, y3)
    return ocmp.reshape(bsz * t, -1), oslc.reshape(bsz * t, -1)


def _combine_kernel(gl_ref, oc_ref, os_ref, ow_ref, o_ref):
    gates = jax.nn.sigmoid(gl_ref[...])
    for h in range(N_HEADS):
        sl = slice(h * HEAD_DIM, (h + 1) * HEAD_DIM)
        acc = gates[:, 3 * h:3 * h + 1] * oc_ref[:, sl]
        acc = acc + gates[:, 3 * h + 1:3 * h + 2] * os_ref[:, sl]
        acc = acc + gates[:, 3 * h + 2:3 * h + 3] * ow_ref[:, sl]
        o_ref[:, sl] = acc.astype(o_ref.dtype)


def nsa_combine(gate_logits, o_cmp, o_slc, o_win, tm):
    m, d = o_cmp.shape
    row = lambda w: pl.BlockSpec((tm, w), lambda i: (i, 0))
    return pl.pallas_call(
        _combine_kernel,
        grid=(m // tm,),
        in_specs=[row(gate_logits.shape[1]), row(d), row(d), row(d)],
        out_specs=row(d),
        out_shape=jax.ShapeDtypeStruct((m, d), BF16),
        compiler_params=_cparams(("parallel",)),
        name="nsa_combine",
    )(gate_logits, o_cmp, o_slc, o_win)


def _block_diag_q(q, g_kv):
    db = q.shape[0]
    r = N_HEADS // g_kv
    qh = q.reshape(db, N_HEADS, 1, HEAD_DIM)
    onehot = (jnp.arange(N_HEADS)[:, None] // r == jnp.arange(g_kv)[None, :]).astype(q.dtype)
    return (qh * onehot[None, :, :, None]).reshape(db, N_HEADS, g_kv * HEAD_DIM).astype(BF16)


def _diag_extract(o_all, g_kv):
    r = N_HEADS // g_kv
    hrow = lax.broadcasted_iota(jnp.int32, (N_HEADS, HEAD_DIM), 0)
    out = jnp.zeros((N_HEADS, HEAD_DIM), F32)
    for g in range(g_kv):
        out = out + jnp.where(hrow // r == g, o_all[:, g * HEAD_DIM:(g + 1) * HEAD_DIM], 0.0)
    return out


def _bf16_round(x):
    return x.astype(BF16).astype(F32)


def _group_rows(x_ref, g_kv):
    r = N_HEADS // g_kv
    hrow = lax.broadcasted_iota(jnp.int32, (N_HEADS, HEAD_DIM), 0)
    out = jnp.zeros((N_HEADS, HEAD_DIM), F32)
    for g in range(g_kv):
        out = out + jnp.where(hrow // r == g, x_ref[g:g + 1, :], 0.0)
    return out


def _win_sample_kernel(*refs, window, past_len, g_kv, has_sink):
    if has_sink:
        sink_ref, q_ref, kb_ref, vb_ref, kn_ref, vn_ref, o_ref = refs
    else:
        q_ref, kb_ref, vb_ref, kn_ref, vn_ref, o_ref = refs
    rows = kb_ref.shape[0]
    wb = rows // g_kv
    r = N_HEADS // g_kv
    q = q_ref[...]
    z = _nt(q, kb_ref[...].astype(BF16)) * SCALE
    zn = jnp.sum(q.astype(F32) * _bf16_round(_group_rows(kn_ref, g_kv)), axis=-1, keepdims=True) * SCALE
    hrow = lax.broadcasted_iota(jnp.int32, (N_HEADS, rows), 0)
    lane = lax.broadcasted_iota(jnp.int32, (N_HEADS, rows), 1)
    k_pos = past_len - wb + lane // g_kv
    rel = past_len - k_pos
    mask = (lane % g_kv == hrow // r) & (rel >= 0) & (rel <= window) & (k_pos >= 0)
    z = jnp.where(mask, z, NEG_INF)
    m = jnp.maximum(jnp.max(z, axis=-1, keepdims=True), zn)
    if has_sink:
        m = jnp.maximum(m, sink_ref[...])
    p = jnp.exp(z - m)
    pn = jnp.exp(zn - m)
    den = jnp.sum(p, axis=-1, keepdims=True) + pn
    if has_sink:
        den = den + jnp.exp(sink_ref[...] - m)
    o = jnp.dot((p / den).astype(BF16), vb_ref[...].astype(BF16), preferred_element_type=F32)
    o = o + _bf16_round(pn / den) * _bf16_round(_group_rows(vn_ref, g_kv))
    o_ref[...] = o.astype(o_ref.dtype)


def window_sample_attn(q, k_new, v_new, k_buf, v_buf, past_len, window, sink, out_dtype):
    db, wb, g_kv, _ = k_buf.shape
    rows = wb * g_kv
    in_specs = [pl.BlockSpec((None, N_HEADS, HEAD_DIM), lambda b: (b, 0, 0)),
                pl.BlockSpec((None, rows, HEAD_DIM), lambda b: (b, 0, 0)),
                pl.BlockSpec((None, rows, HEAD_DIM), lambda b: (b, 0, 0)),
                pl.BlockSpec((None, g_kv, HEAD_DIM), lambda b: (b, 0, 0)),
                pl.BlockSpec((None, g_kv, HEAD_DIM), lambda b: (b, 0, 0))]
    args = [q.reshape(db, N_HEADS, HEAD_DIM).astype(BF16),
            k_buf.reshape(db, rows, HEAD_DIM), v_buf.reshape(db, rows, HEAD_DIM),
            k_new.reshape(db, g_kv, HEAD_DIM), v_new.reshape(db, g_kv, HEAD_DIM)]
    if sink is not None:
        in_specs = [pl.BlockSpec((N_HEADS, 1), lambda b: (0, 0))] + in_specs
        args = [sink.reshape(N_HEADS, 1)] + args
    out = pl.pallas_call(
        functools.partial(_win_sample_kernel, window=window, past_len=past_len, g_kv=g_kv,
                          has_sink=sink is not None),
        grid=(db,),
        in_specs=in_specs,
        out_specs=pl.BlockSpec((None, N_HEADS, HEAD_DIM), lambda b: (b, 0, 0)),
        out_shape=jax.ShapeDtypeStruct((db, N_HEADS, HEAD_DIM), out_dtype),
        compiler_params=_cparams(("parallel",)),
        name="window_sample",
    )(*args)
    return out.reshape(db, N_HEADS * HEAD_DIM)


def _sb_sample_kernel(*refs, pp, n_pages, first_rank, q_pos, g_kv):
    refs = refs[1:]
    q_ref, cs0_ref, acc0_ref = refs[:3]
    k_refs = refs[3:3 + pp]
    v_refs = refs[3 + pp:3 + 2 * pp]
    cs_out_ref, acc_out_ref, cs_ref, acc_ref = refs[3 + 2 * pp:]
    p = pl.program_id(1)
    r = N_HEADS // g_kv
    n_ch = g_kv
    cp = LANES // g_kv

    @pl.when(p == 0)
    def _():
        cs_ref[...] = cs0_ref[:, 0:1]
        acc_ref[...] = acc0_ref[...]

    q = q_ref[...]
    row = lax.broadcasted_iota(jnp.int32, (LANES, LANES), 0)
    col = lax.broadcasted_iota(jnp.int32, (LANES, LANES), 1)
    later_sel = (row > col).astype(BF16)
    n_rank = pp * n_ch
    chunks = [(s, c) for s in range(pp) for c in reversed(range(n_ch))]
    srow = lax.broadcasted_iota(jnp.int32, (n_rank * N_HEADS, LANES), 0)
    slane = lax.broadcasted_iota(jnp.int32, (n_rank * N_HEADS, LANES), 1)
    own = (slane % g_kv) == ((srow % N_HEADS) // r)
    rank = srow // N_HEADS
    k_pos = ((n_pages - 1 - first_rank - p * pp) * n_ch - rank) * cp + slane // g_kv
    causal = own & (k_pos < q_pos)
    z_pages = [_nt(q, k_refs[s][...].astype(BF16)) for s in range(pp)]
    z = jnp.concatenate([z_pages[s][:, c * LANES:(c + 1) * LANES] for s, c in chunks],
                        axis=0) * SCALE
    sp = _softplus_neg_abs(z)
    log_beta = jnp.minimum(z, 0.0) - sp
    log_keep = jnp.where(causal, jnp.minimum(-z, 0.0) - sp, 0.0)
    rs = jnp.sum(log_keep, axis=-1, keepdims=True)
    cs = cs_ref[...]
    carries = []
    for ci in range(n_rank):
        carries.append(cs)
        cs = cs + rs[ci * N_HEADS:(ci + 1) * N_HEADS]
    later = _dot2(log_keep, later_sel) + jnp.concatenate(carries, axis=0)
    a = jnp.where(causal, jnp.exp(log_beta + later), 0.0).astype(BF16)
    acc = acc_ref[...]
    for s in range(pp):
        ranks = [s * n_ch + (n_ch - 1 - c) for c in range(n_ch)]
        a_page = jnp.concatenate([a[k * N_HEADS:(k + 1) * N_HEADS] for k in ranks], axis=1)
        acc = acc + jnp.dot(a_page, v_refs[s][...].astype(BF16), preferred_element_type=F32)
    cs_ref[...] = cs
    acc_ref[...] = acc

    @pl.when(p == pl.num_programs(1) - 1)
    def _():
        cs_out_ref[...] = jnp.broadcast_to(cs, cs_out_ref.shape)
        acc_out_ref[...] = acc


def _sb_sample_sweep(q16, pk, pv, page_table, cs0, acc0, first_rank, n_steps, pp, g_kv):
    db, n_pages = page_table.shape
    rows = pk.shape[1]
    q_pos = n_pages * (rows // g_kv)
    page_spec = lambda s: pl.BlockSpec(
        (None, rows, HEAD_DIM),
        lambda b, p, pt, s=s: (pt[b, n_pages - 1 - first_rank - (p * pp + s)], 0, 0))
    state_spec = pl.BlockSpec((None, N_HEADS, HEAD_DIM), lambda b, p, pt: (b, 0, 0))
    grid_spec = pltpu.PrefetchScalarGridSpec(
        num_scalar_prefetch=1,
        grid=(db, n_steps),
        in_specs=[state_spec, state_spec, state_spec] + [page_spec(s) for s in range(pp)] * 2,
        out_specs=[state_spec, state_spec],
        scratch_shapes=[pltpu.VMEM((N_HEADS, 1), F32), pltpu.VMEM((N_HEADS, HEAD_DIM), F32)],
    )
    state = jax.ShapeDtypeStruct((db, N_HEADS, HEAD_DIM), F32)
    return pl.pallas_call(
        functools.partial(_sb_sample_kernel, pp=pp, n_pages=n_pages, first_rank=first_rank, q_pos=q_pos,
                          g_kv=g_kv),
        grid_spec=grid_spec,
        out_shape=[state, state],
        compiler_params=_cparams(("parallel", "arbitrary")),
        name="sb_sample",
    )(page_table, q16, cs0, acc0, *([pk] * pp), *([pv] * pp))


def sb_sample_attn(q, pool_k, pool_v, page_table, pp, out_dtype):
    db, n_pages = page_table.shape
    n_pool, page, g_kv, _ = pool_k.shape
    assert page == 128 and n_pages % pp == 0
    rows = page * g_kv
    pk = pool_k.reshape(n_pool, rows, HEAD_DIM)
    pv = pool_v.reshape(n_pool, rows, HEAD_DIM)
    q16 = q.reshape(db, N_HEADS, HEAD_DIM).astype(BF16)
    zero = jnp.zeros((db, N_HEADS, HEAD_DIM), F32)
    cs, acc = _sb_sample_sweep(q16, pk, pv, page_table, zero, zero, 0, 1, pp, g_kv)
    if n_pages > pp:
        rest = lambda st: _sb_sample_sweep(q16, pk, pv, page_table, st[0], st[1], pp, n_pages // pp - 1,
                                           pp, g_kv)[1]
        acc = lax.cond(jnp.max(cs) > SB_CUTOFF, rest, lambda st: st[1], (cs, acc))
    return acc.astype(out_dtype).reshape(db, N_HEADS * HEAD_DIM)


def _nsa_sample_cmp_kernel(q_ref, kc_ref, vc_ref, cover_ref, ocmp_ref, idx_ref, val_ref,
                           *, n_cmp, n_sel, n_top, q_pos):
    q = q_ref[...]
    nc = kc_ref.shape[0]
    ns = cover_ref.shape[1]
    r = N_HEADS // NSA_KV
    lc = _nt(q, kc_ref[...].astype(BF16)) * SCALE
    n_c = lax.broadcasted_iota(jnp.int32, (N_HEADS, nc), 1)
    cmask = (n_c * NSA_CMP_STRIDE + NSA_CMP_LEN - 1 <= q_pos) & (n_c < n_cmp)
    pc = _masked_softmax(lc, cmask)
    o_all = jnp.dot(pc.astype(BF16), vc_ref[...].astype(BF16), preferred_element_type=F32)
    ocmp_ref[...] = _diag_extract(o_all, NSA_KV)

    grow = lax.broadcasted_iota(jnp.int32, (SUBLANES, N_HEADS), 0)
    hcol = lax.broadcasted_iota(jnp.int32, (SUBLANES, N_HEADS), 1)
    group_sel = (hcol // r == grow).astype(BF16)
    p_sum = _dot3r(group_sel, pc)
    imp = _dot3(p_sum, cover_ref[...])
    jj = lax.broadcasted_iota(jnp.int32, (SUBLANES, ns), 1)
    cur = q_pos // NSA_SEL_LEN
    vis = jj * NSA_SEL_LEN <= q_pos
    forced = (jj == 0) | (jj == cur) | (jj == cur - 1)
    score = jnp.where(vis, jnp.where(forced, SEL_FORCE, imp), -1.0)
    score = jnp.where(jj < n_sel, score, -2.0)
    jf = jj.astype(F32)
    tl = lax.broadcasted_iota(jnp.int32, (SUBLANES, LANES), 1)
    idx_out = jnp.zeros((SUBLANES, LANES), F32)
    val_out = jnp.full((SUBLANES, LANES), -1.0, F32)
    for t in range(n_top):
        mx = jnp.max(score, axis=-1, keepdims=True)
        first = jnp.min(jnp.where(score == mx, jf, float(ns)), axis=-1, keepdims=True)
        idx_out = jnp.where(tl == t, first, idx_out)
        val_out = jnp.where(tl == t, mx, val_out)
        score = jnp.where(jf == first, NEG_INF, score)
    idx_ref[...] = idx_out.astype(jnp.int32)
    val_ref[...] = val_out


def nsa_sample_cmp(q, kcmp, vcmp, n_cmp, n_sel, q_pos):
    db, nc, gw = kcmp.shape
    ns = -(-n_sel // LANES) * LANES
    n_top = min(NSA_TOP, n_sel)
    cover = _cover_matrix(nc, n_cmp, ns, n_sel)
    qbd = _block_diag_q(q, NSA_KV)
    return pl.pallas_call(
        functools.partial(_nsa_sample_cmp_kernel, n_cmp=n_cmp, n_sel=n_sel, n_top=n_top, q_pos=q_pos),
        grid=(db,),
        in_specs=[pl.BlockSpec((None, N_HEADS, gw), lambda b: (b, 0, 0)),
                  pl.BlockSpec((None, nc, gw), lambda b: (b, 0, 0)),
                  pl.BlockSpec((None, nc, gw), lambda b: (b, 0, 0)),
                  pl.BlockSpec(cover.shape, lambda b: (0, 0))],
        out_specs=[pl.BlockSpec((None, N_HEADS, HEAD_DIM), lambda b: (b, 0, 0)),
                   pl.BlockSpec((None, SUBLANES, LANES), lambda b: (b, 0, 0)),
                   pl.BlockSpec((None, SUBLANES, LANES), lambda b: (b, 0, 0))],
        out_shape=[jax.ShapeDtypeStruct((db, N_HEADS, HEAD_DIM), F32),
                   jax.ShapeDtypeStruct((db, SUBLANES, LANES), jnp.int32),
                   jax.ShapeDtypeStruct((db, SUBLANES, LANES), F32)],
        compiler_params=_cparams(("parallel",)),
        name="nsa_sample_cmp",
    )(qbd, kcmp, vcmp, cover)


def _nsa_sample_slc_kernel(*refs, n_top, n_sel, q_pos):
    idx_ref, ok_ref, pt_ref = refs[:3]
    refs = refs[3:]
    q_ref = refs[0]
    k_refs = refs[1:1 + n_top]
    v_refs = refs[1 + n_top:1 + 2 * n_top]
    kn_ref, vn_ref, o_ref = refs[1 + 2 * n_top:]
    b = pl.program_id(0)
    g = pl.program_id(1)
    r = N_HEADS // NSA_KV
    rows = NSA_SEL_LEN * NSA_KV
    base = (b * NSA_KV + g) * n_top

    @pl.when(g == 0)
    def _():
        o_ref[...] = jnp.zeros_like(o_ref)

    q = q_ref[...]
    lane = lax.broadcasted_iota(jnp.int32, (N_HEADS, rows), 1)
    lane_pos = lane // NSA_KV
    lane_own = (lane % NSA_KV) == g
    zs = []
    new_ok = jnp.int32(0)
    for t in range(n_top):
        j = idx_ref[base + t]
        ok = ok_ref[base + t]
        in_pool = j < n_sel - 1
        z = _nt(q, k_refs[t][...].astype(BF16)) * SCALE
        last_pos = jnp.where((ok > 0) & in_pool, q_pos, -1)
        zs.append(jnp.where(lane_own & ((j * NSA_SEL_LEN + lane_pos) <= last_pos), z, NEG_INF))
        new_ok = new_ok | jnp.where((ok > 0) & jnp.logical_not(in_pool), 1, 0)
    new_vis = (new_ok > 0) & ((n_sel - 1) * NSA_SEL_LEN <= q_pos)
    zn = jnp.sum(q.astype(F32) * _bf16_round(kn_ref[...]), axis=-1, keepdims=True) * SCALE
    zn = zn + jnp.where(new_vis, 0.0, NEG_INF)
    m = zn
    for z in zs:
        m = jnp.maximum(m, jnp.max(z, axis=-1, keepdims=True))
    m = jnp.where(m == NEG_INF, 0.0, m)
    pn = jnp.exp(zn - m)
    den = pn
    ps = []
    for z in zs:
        e = jnp.exp(z - m)
        ps.append(e)
        den = den + jnp.sum(e, axis=-1, keepdims=True)
    den = jnp.where(den > 0, den, 1.0)
    o = _bf16_round(pn / den) * _bf16_round(vn_ref[...])
    for t in range(n_top):
        o = o + jnp.dot((ps[t] / den).astype(BF16), v_refs[t][...].astype(BF16),
                        preferred_element_type=F32)
    hrow = lax.broadcasted_iota(jnp.int32, (N_HEADS, HEAD_DIM), 0)
    o_ref[...] = jnp.where(hrow // r == g, o, o_ref[...])


def nsa_sample_slc(q, top_idx, top_val, pool_k, pool_v, page_table, k_new, v_new, n_sel, q_pos):
    db, n_pages = page_table.shape
    n_top = min(NSA_TOP, n_sel)
    n_pool, page, g_kv, _ = pool_k.shape
    assert g_kv == NSA_KV
    half = page // NSA_SEL_LEN
    rows = NSA_SEL_LEN * g_kv
    pk = pool_k.reshape(n_pool * half, rows, HEAD_DIM)
    pv = pool_v.reshape(n_pool * half, rows, HEAD_DIM)
    idx = top_idx[:, :NSA_KV, :n_top].reshape(-1)
    ok = (top_val[:, :NSA_KV, :n_top] >= 0.0).astype(jnp.int32).reshape(-1)

    def blk_spec(t):
        def index_map(b, g, idx_ref, ok_ref, pt_ref):
            j = jnp.minimum(idx_ref[(b * NSA_KV + g) * n_top + t], n_sel - 2)
            return (pt_ref[b, j // half] * half + j % half, 0, 0)
        return pl.BlockSpec((None, rows, HEAD_DIM), index_map)

    row_spec = pl.BlockSpec((None, 1, HEAD_DIM), lambda b, g, *_: (b * NSA_KV + g, 0, 0))
    grid_spec = pltpu.PrefetchScalarGridSpec(
        num_scalar_prefetch=3,
        grid=(db, NSA_KV),
        in_specs=([pl.BlockSpec((None, N_HEADS, HEAD_DIM), lambda b, g, *_: (b, 0, 0))]
                  + [blk_spec(t) for t in range(n_top)] * 2 + [row_spec, row_spec]),
        out_specs=pl.BlockSpec((None, N_HEADS, HEAD_DIM), lambda b, g, *_: (b, 0, 0)),
    )
    return pl.pallas_call(
        functools.partial(_nsa_sample_slc_kernel, n_top=n_top, n_sel=n_sel, q_pos=q_pos),
        grid_spec=grid_spec,
        out_shape=jax.ShapeDtypeStruct((db, N_HEADS, HEAD_DIM), F32),
        compiler_params=_cparams(("parallel", "arbitrary")),
        name="nsa_sample_slc",
    )(idx, ok, page_table, q.reshape(db, N_HEADS, HEAD_DIM).astype(BF16),
      *([pk] * n_top), *([pv] * n_top),
      k_new.reshape(db * NSA_KV, 1, HEAD_DIM), v_new.reshape(db * NSA_KV, 1, HEAD_DIM))


def _rope_tables(pos):
    half = HEAD_DIM // 2
    inv = ROPE_THETA ** (-jnp.arange(half, dtype=F32) / half)
    ang = pos.astype(F32)[:, None] * inv[None, :]
    cos, sin = jnp.cos(ang), jnp.sin(ang)
    return jnp.concatenate([cos, cos], -1), jnp.concatenate([-sin, sin], -1)


def _pick(n, cands):
    for c in cands:
        if n % c == 0:
            return c
    return n


def _res_ln_rows(hp, hs, w16_all, widx, xp32, xs32, g, b, scale):
    tm = _pick(xp32.shape[0], (256, 128))
    op32, op16, os32, os16 = res_ln(hp, hs, w16_all, widx, xp32, xs32, g, b, scale, tm, _pick(tm, (128,)))
    return (op32, op16), (os32, os16)


def _ffn_step(xp, xs, w_in_all, w_out16_all, which, g, b):
    m = xp[0].shape[0]
    f = w_out16_all.shape[-2]
    hp, hs = glu(xp[1], xs[1], w_in_all, which, _pick(m, (1024, 512, 256, 128)), _pick(f, (512, 256, 128)))
    return _res_ln_rows(hp, hs, w_out16_all, which, xp[0], xs[0], g, b, 0.5)


def _project(x16, w16_all, widx, flags, tables, tn):
    m = x16.shape[0]
    tm = _pick(min(m, tables[0].shape[0]), (1024, 512, 256, 128))
    return proj(x16, w16_all, widx, len(flags) * tn, jnp.asarray(flags, jnp.int32), tables[0], tables[1], tm, tn)


def kernel(x_prompt, x_sample, cache_swa_k, cache_swa_v, cache_nsa_kc, cache_nsa_vc, cache_nsa_ks,
           cache_nsa_vs, cache_nsa_kw, cache_nsa_vw, cache_sb_k, cache_sb_v, page_table, ln_g, ln_b,
           ffn_w_in, ffn_w_out, swa_w_in, swa_sink, swa_w_o, nsa_w_in, nsa_cmp_a, nsa_cmp_pe,
           nsa_cmp_w1, nsa_cmp_w2, nsa_w_o, sb_w_in, sb_w_o):
    bsz, t, d = x_prompt.shape
    db, tn_new, _ = x_sample.shape
    assert tn_new == 1
    depth = ffn_w_in.shape[0]
    n_pages = page_table.shape[1]
    page = cache_nsa_kc.shape[2]
    past_len = n_pages * page
    hq = N_HEADS * HEAD_DIM

    tab_p = _rope_tables(jnp.arange(t, dtype=jnp.int32))
    tab_s = _rope_tables(jnp.full((db,), past_len, jnp.int32))

    xp32 = x_prompt.reshape(bsz * t, d)
    xs32 = x_sample.reshape(db, d)
    xp16, xs16 = xp32.astype(BF16), xs32.astype(BF16)

    ffn_w_out16 = ffn_w_out.astype(BF16)
    swa_w_in16, swa_w_o16 = swa_w_in.astype(BF16), swa_w_o.astype(BF16)
    nsa_w_in16, nsa_w_o16 = nsa_w_in.astype(BF16), nsa_w_o.astype(BF16)
    sb_w_in16, sb_w_o16 = sb_w_in.astype(BF16), sb_w_o.astype(BF16)

    outs = {k: [] for k in ("swa_kp", "swa_vp", "swa_ks", "swa_vs", "nkc_p", "nvc_p", "nks_p", "nvs_p",
                            "nkw_p", "nvw_p", "nkc_s", "nvc_s", "nks_s", "nvs_s", "nkw_s", "nvw_s",
                            "sbk_p", "sbv_p", "sbk_s", "sbv_s")}

    for l in range(depth):
        kind, j = l % 3, l // 3
        (xp32, xp16), (xs32, xs16) = _ffn_step((xp32, xp16), (xs32, xs16), ffn_w_in, ffn_w_out16, (l, 0),
                                               ln_g[l, 0], ln_b[l, 0])

        if kind == 0:
            kd = SWA_KV * HEAD_DIM
            flags = [1] * ((hq + kd) // 512) + [0] * (kd // 512)
            yp = _project(xp16, swa_w_in16, (j,), flags, tab_p, 512)
            ys = _project(xs16, swa_w_in16, (j,), flags, tab_s, 512)
            op = window_prompt_attn(yp, bsz, t, SWA_KV, 0, N_HEADS, N_HEADS + SWA_KV, SWA_WINDOW,
                                    swa_sink[j], BF16)
            keep = min(SWA_WINDOW, t)
            yp3 = yp.reshape(bsz, t, -1)
            outs["swa_kp"].append(yp3[:, t - keep:, hq:hq + kd].reshape(bsz, keep, SWA_KV, HEAD_DIM))
            outs["swa_vp"].append(yp3[:, t - keep:, hq + kd:].reshape(bsz, keep, SWA_KV, HEAD_DIM))
            k_new, v_new = ys[:, hq:hq + kd], ys[:, hq + kd:]
            k_buf, v_buf = cache_swa_k[j], cache_swa_v[j]
            osm = window_sample_attn(ys[:, :hq], k_new, v_new, k_buf, v_buf, past_len, SWA_WINDOW,
                                     swa_sink[j], BF16)
            wb = k_buf.shape[1]
            keep_s = min(SWA_WINDOW, past_len + 1)
            new4 = lambda x: x.reshape(db, 1, SWA_KV, HEAD_DIM)
            outs["swa_ks"].append(jnp.concatenate([k_buf, new4(k_new)], axis=1)[:, wb + 1 - keep_s:])
            outs["swa_vs"].append(jnp.concatenate([v_buf, new4(v_new)], axis=1)[:, wb + 1 - keep_s:])
            w_o16 = swa_w_o16
        elif kind == 1:
            kd = NSA_KV * HEAD_DIM
            main = hq + 6 * kd
            w_gate = jnp.pad(nsa_w_in16[j][:, main:], ((0, 0), (0, LANES - 3 * N_HEADS)))
            flags = [1] * (hq // 512) + [1, 0, 1, 0, 1, 0]
            a_k, a_v = nsa_cmp_a[j, 0], nsa_cmp_a[j, 1]
            mlp_k = (a_k, nsa_cmp_pe[j, 0], nsa_cmp_w1[j, 0], nsa_cmp_w2[j, 0])
            mlp_v = (a_v, nsa_cmp_pe[j, 1], nsa_cmp_w1[j, 1], nsa_cmp_w2[j, 1])
            yp = _project(xp16, nsa_w_in16, (j,), flags, tab_p, 512)
            gp = _project(xp16, w_gate, (), [0], tab_p, LANES)
            pp = _pick(t // 128, (8, 4, 2, 1))
            yp3 = yp.reshape(bsz, t, -1)
            seg = lambda c: yp3[:, :, hq + c * kd:hq + (c + 1) * kd].reshape(bsz, t, NSA_KV, HEAD_DIM)
            kc_p, vc_p = seg(0), seg(1)
            kcmp = cmp_mlp(*pool_prompt(kc_p, a_k, pp), *mlp_k)
            vcmp = cmp_mlp(*pool_prompt(vc_p, a_v, pp), *mlp_v)
            oc, osl = nsa_prompt_global(yp, kcmp, vcmp, bsz, t)
            ow = window_prompt_attn(yp, bsz, t, NSA_KV, 0, N_HEADS + 4 * NSA_KV, N_HEADS + 5 * NSA_KV,
                                    NSA_WINDOW, None, F32)
            op = nsa_combine(gp, oc, osl, ow, _pick(bsz * t, (512, 256, 128)))
            keep = min(NSA_WINDOW, t)
            outs["nkc_p"].append(kc_p)
            outs["nvc_p"].append(vc_p)
            outs["nks_p"].append(seg(2))
            outs["nvs_p"].append(seg(3))
            outs["nkw_p"].append(seg(4)[:, t - keep:])
            outs["nvw_p"].append(seg(5)[:, t - keep:])
            ys = _project(xs16, nsa_w_in16, (j,), flags, tab_s, 512)
            gs = _project(xs16, w_gate, (), [0], tab_s, LANES)
            sseg = lambda c: ys[:, hq + c * kd:hq + (c + 1) * kd]
            assert page == 128
            total = past_len + 1
            n_seg = total // NSA_CMP_STRIDE
            assert n_seg == n_pages * SEG_PER_PAGE
            n_cmp = n_seg - NSA_CMP_LEN // NSA_CMP_STRIDE + 1
            n_sel = -(-total // NSA_SEL_LEN)
            pps = _pick(n_pages, (32, 16, 8, 4, 2, 1))
            kcmp_s = cmp_mlp(*pool_paged(cache_nsa_kc[j], page_table, a_k, pps), *mlp_k)
            vcmp_s = cmp_mlp(*pool_paged(cache_nsa_vc[j], page_table, a_v, pps), *mlp_v)
            oc_s, top_idx, top_val = nsa_sample_cmp(ys[:, :hq], kcmp_s, vcmp_s, n_cmp, n_sel, past_len)
            osl_s = nsa_sample_slc(ys[:, :hq], top_idx, top_val, cache_nsa_ks[j], cache_nsa_vs[j],
                                   page_table, sseg(2), sseg(3), n_sel, past_len)
            kw_buf, vw_buf = cache_nsa_kw[j], cache_nsa_vw[j]
            ow_s = window_sample_attn(ys[:, :hq], sseg(4), sseg(5), kw_buf, vw_buf, past_len, NSA_WINDOW,
                                      None, F32)
            osm = nsa_combine(gs, oc_s.reshape(db, hq), osl_s.reshape(db, hq), ow_s, db)
            wb = kw_buf.shape[1]
            keep_s = min(NSA_WINDOW, past_len + 1)
            new4 = lambda c: sseg(c).reshape(db, 1, NSA_KV, HEAD_DIM)
            for name, c in (("nkc_s", 0), ("nvc_s", 1), ("nks_s", 2), ("nvs_s", 3)):
                outs[name].append(new4(c))
            outs["nkw_s"].append(jnp.concatenate([kw_buf, new4(4)], axis=1)[:, wb + 1 - keep_s:])
            outs["nvw_s"].append(jnp.concatenate([vw_buf, new4(5)], axis=1)[:, wb + 1 - keep_s:])
            w_o16 = nsa_w_o16
        else:
            kd = SB_KV * HEAD_DIM
            flags = [0] * (sb_w_in.shape[-1] // 512)
            yp = _project(xp16, sb_w_in16, (j,), flags, tab_p, 512)
            ys = _project(xs16, sb_w_in16, (j,), flags, tab_s, 512)
            op = sb_prompt_attn(yp, bsz, t, BF16)
            yp3 = yp.reshape(bsz, t, -1)
            outs["sbk_p"].append(yp3[:, :, hq:hq + kd].reshape(bsz, t, SB_KV, HEAD_DIM))
            outs["sbv_p"].append(yp3[:, :, hq + kd:].reshape(bsz, t, SB_KV, HEAD_DIM))
            osm = sb_sample_attn(ys[:, :hq], cache_sb_k[j], cache_sb_v[j], page_table,
                                 _pick(n_pages, (8, 4, 2, 1)), BF16)
            outs["sbk_s"].append(ys[:, hq:hq + kd].reshape(db, 1, SB_KV, HEAD_DIM))
            outs["sbv_s"].append(ys[:, hq + kd:].reshape(db, 1, SB_KV, HEAD_DIM))
            w_o16 = sb_w_o16

        (xp32, xp16), (xs32, xs16) = _res_ln_rows(op, osm, w_o16, (j,), xp32, xs32, ln_g[l, 1], ln_b[l, 1], 1.0)
        (xp32, xp16), (xs32, xs16) = _ffn_step((xp32, xp16), (xs32, xs16), ffn_w_in, ffn_w_out16, (l, 1),
                                               ln_g[l, 2], ln_b[l, 2])

    st = lambda name: jnp.stack(outs[name])
    return (xp32.reshape(bsz, t, d), xs32.reshape(db, 1, d),
            st("swa_kp"), st("swa_vp"),
            st("nkc_p"), st("nvc_p"), st("nks_p"), st("nvs_p"), st("nkw_p"), st("nvw_p"),
            st("sbk_p"), st("sbv_p"),
            st("swa_ks"), st("swa_vs"),
            st("nkc_s"), st("nvc_s"), st("nks_s"), st("nvs_s"), st("nkw_s"), st("nvw_s"),
            st("sbk_s"), st("sbv_s"))
```

```python
import functools

import numpy as np
import jax
import jax.numpy as jnp
from jax import lax
from jax.experimental import pallas as pl
from jax.experimental.pallas import tpu as pltpu

F32 = jnp.float32
BF16 = jnp.bfloat16

HEAD_DIM = 128
N_HEADS = 16
Q_BLOCK = 128
SWA_KV = 4
SWA_WINDOW = 128
NSA_KV = 4
NSA_CMP_LEN = 32
NSA_CMP_STRIDE = 16
NSA_SEL_LEN = 64
NSA_TOP = 16
NSA_WINDOW = 512
SEL_FORCE = 1e4
SB_KV = 8
ROPE_THETA = 10000.0
LN_EPS = 1e-5
DEPTH = 4
DN_ALPHA = (2 * DEPTH) ** 0.25
SCALE = HEAD_DIM ** -0.5
NEG_INF = float("-inf")

LANES = 128
SUBLANES = 8
VMEM_LIMIT = 56 * 1024 * 1024


def _cparams(sem):
    return pltpu.CompilerParams(dimension_semantics=sem, vmem_limit_bytes=VMEM_LIMIT)


def _nt(a, b):
    return lax.dot_general(a, b, (((1,), (1,)), ((), ())), preferred_element_type=F32)


def _split3(x):
    hi = x.astype(BF16)
    r = x - hi.astype(F32)
    mid = r.astype(BF16)
    lo = (r - mid.astype(F32)).astype(BF16)
    return hi, mid, lo


def _dot3(x, u):
    hi, mid, lo = _split3(x)
    return (jnp.dot(hi, u, preferred_element_type=F32)
            + jnp.dot(mid, u, preferred_element_type=F32)
            + jnp.dot(lo, u, preferred_element_type=F32))


def _dot2(x, u):
    hi = x.astype(BF16)
    mid = (x - hi.astype(F32)).astype(BF16)
    return jnp.dot(hi, u, preferred_element_type=F32) + jnp.dot(mid, u, preferred_element_type=F32)


def _dot3r(u, x):
    hi, mid, lo = _split3(x)
    return (jnp.dot(u, hi, preferred_element_type=F32)
            + jnp.dot(u, mid, preferred_element_type=F32)
            + jnp.dot(u, lo, preferred_element_type=F32))


def _stack_heads(q, r):
    return jnp.concatenate([q[:, h * HEAD_DIM:(h + 1) * HEAD_DIM] for h in range(r)], axis=0)


def _tile_rows(x, r):
    return jnp.concatenate([x] * r, axis=0) if r > 1 else x


def _silu_mul(g, u):
    return g * jax.nn.sigmoid(g) * u


def _glu_kernel(x_ref, xs_ref, wg_ref, wu_ref, o_ref, os_ref, wg16_ref, wu16_ref):
    @pl.when(pl.program_id(1) == 0)
    def _():
        wg16_ref[...] = wg_ref[...].astype(BF16)
        wu16_ref[...] = wu_ref[...].astype(BF16)
        xs = xs_ref[...]
        gs = jnp.dot(xs, wg16_ref[...], preferred_element_type=F32)
        us = jnp.dot(xs, wu16_ref[...], preferred_element_type=F32)
        os_ref[...] = _silu_mul(gs, us).astype(os_ref.dtype)

    x = x_ref[...]
    g = jnp.dot(x, wg16_ref[...], preferred_element_type=F32)
    u = jnp.dot(x, wu16_ref[...], preferred_element_type=F32)
    o_ref[...] = _silu_mul(g, u).astype(o_ref.dtype)


def glu(xb, xsb, w_in_all, which, tm, tn):
    m, k = xb.shape
    ms = xsb.shape[0]
    f = w_in_all.shape[-1] // 2
    nj = f // tn
    l0, l1 = which
    return pl.pallas_call(
        _glu_kernel,
        grid=(nj, m // tm),
        in_specs=[pl.BlockSpec((tm, k), lambda j, i: (i, 0)),
                  pl.BlockSpec((ms, k), lambda j, i: (0, 0)),
                  pl.BlockSpec((None, None, k, tn), lambda j, i: (l0, l1, 0, j)),
                  pl.BlockSpec((None, None, k, tn), lambda j, i: (l0, l1, 0, j + nj))],
        out_specs=[pl.BlockSpec((tm, tn), lambda j, i: (i, j)),
                   pl.BlockSpec((ms, tn), lambda j, i: (0, j))],
        out_shape=[jax.ShapeDtypeStruct((m, f), BF16), jax.ShapeDtypeStruct((ms, f), BF16)],
        scratch_shapes=[pltpu.VMEM((k, tn), BF16), pltpu.VMEM((k, tn), BF16)],
        compiler_params=_cparams(("parallel", "arbitrary")),
        name="glu",
    )(xb, xsb, w_in_all, w_in_all)


def _res_ln_math(h, w, x, g, b, scale):
    y = DN_ALPHA * x + scale * jnp.dot(h, w, preferred_element_type=F32)
    mu = jnp.mean(y, axis=-1, keepdims=True)
    d = y - mu
    var = jnp.mean(d * d, axis=-1, keepdims=True)
    return d * lax.rsqrt(var + LN_EPS) * g + b


def _resln_kernel(h_ref, hs_ref, w_ref, x_ref, xs_ref, g_ref, b_ref, o32_ref, o16_ref, os32_ref, os16_ref,
                  *, scale, sub):
    @pl.when(pl.program_id(0) == 0)
    def _():
        outs = _res_ln_math(hs_ref[...], w_ref[...], xs_ref[...], g_ref[...], b_ref[...], scale)
        os32_ref[...] = outs
        os16_ref[...] = outs.astype(BF16)

    for u in range(h_ref.shape[0] // sub):
        rows = slice(u * sub, (u + 1) * sub)
        out = _res_ln_math(h_ref[rows, :], w_ref[...], x_ref[rows, :], g_ref[...], b_ref[...], scale)
        o32_ref[rows, :] = out
        o16_ref[rows, :] = out.astype(BF16)


def res_ln(hb, hsb, w_all, widx, x, xs, g, b, scale, tm, sub):
    m, kdim = hb.shape
    ms = hsb.shape[0]
    d = w_all.shape[-1]
    lead = (None,) * len(widx)
    tile = lambda w: pl.BlockSpec((tm, w), lambda i: (i, 0))
    whole = lambda r, w: pl.BlockSpec((r, w), lambda i: (0, 0))
    return pl.pallas_call(
        functools.partial(_resln_kernel, scale=scale, sub=sub),
        grid=(m // tm,),
        in_specs=[tile(kdim), whole(ms, kdim),
                  pl.BlockSpec(lead + (kdim, d), lambda i: tuple(widx) + (0, 0), pipeline_mode=pl.Buffered(1)),
                  tile(d), whole(ms, d), whole(1, d), whole(1, d)],
        out_specs=[tile(d), tile(d), whole(ms, d), whole(ms, d)],
        out_shape=[jax.ShapeDtypeStruct((m, d), F32), jax.ShapeDtypeStruct((m, d), BF16),
                   jax.ShapeDtypeStruct((ms, d), F32), jax.ShapeDtypeStruct((ms, d), BF16)],
        compiler_params=_cparams(("arbitrary",)),
        name="res_ln",
    )(hb, hsb, w_all, x, xs, g.reshape(1, d), b.reshape(1, d))


def _rope_store(o_ref, y, c, s, nh):
    for h in range(nh):
        yh = y[:, h * HEAD_DIM:(h + 1) * HEAD_DIM]
        o_ref[:, h * HEAD_DIM:(h + 1) * HEAD_DIM] = yh * c + pltpu.roll(yh, HEAD_DIM // 2, 1) * s


def _proj_kernel(flag_ref, x_ref, xs_ref, w_ref, c_ref, s_ref, cs_ref, ss_ref, o_ref, os_ref, *, nh):
    roped = flag_ref[pl.program_id(0)] != 0

    @pl.when(pl.program_id(1) == 0)
    def _():
        ys = jnp.dot(xs_ref[...], w_ref[...], preferred_element_type=F32)

        @pl.when(jnp.logical_not(roped))
        def _():
            os_ref[...] = ys

        @pl.when(roped)
        def _():
            _rope_store(os_ref, ys, cs_ref[...], ss_ref[...], nh)

    y = jnp.dot(x_ref[...], w_ref[...], preferred_element_type=F32)

    @pl.when(jnp.logical_not(roped))
    def _():
        o_ref[...] = y

    @pl.when(roped)
    def _():
        _rope_store(o_ref, y, c_ref[...], s_ref[...], nh)


def proj(xb, xsb, w_all, widx, n, rope_flags, tab_p, tab_s, tm, tn):
    m, k = xb.shape
    ms = xsb.shape[0]
    lead = (None,) * len(widx)
    tbl_blocks = tab_p[0].shape[0] // tm
    tbl = pl.BlockSpec((tm, HEAD_DIM), lambda j, i, f: (i % tbl_blocks, 0))
    tbl_s = pl.BlockSpec((ms, HEAD_DIM), lambda j, i, f: (0, 0))
    grid_spec = pltpu.PrefetchScalarGridSpec(
        num_scalar_prefetch=1,
        grid=(n // tn, m // tm),
        in_specs=[pl.BlockSpec((tm, k), lambda j, i, f: (i, 0)),
                  pl.BlockSpec((ms, k), lambda j, i, f: (0, 0)),
                  pl.BlockSpec(lead + (k, tn), lambda j, i, f: tuple(widx) + (0, j)),
                  tbl, tbl, tbl_s, tbl_s],
        out_specs=[pl.BlockSpec((tm, tn), lambda j, i, f: (i, j)),
                   pl.BlockSpec((ms, tn), lambda j, i, f: (0, j))],
    )
    return pl.pallas_call(
        functools.partial(_proj_kernel, nh=tn // HEAD_DIM),
        grid_spec=grid_spec,
        out_shape=[jax.ShapeDtypeStruct((m, n), F32), jax.ShapeDtypeStruct((ms, n), F32)],
        compiler_params=_cparams(("parallel", "arbitrary")),
        name="proj",
    )(rope_flags, xb, xsb, w_all, tab_p[0], tab_p[1], tab_s[0], tab_s[1])


def _win_kernel(*refs, window, r, qb, has_sink):
    if has_sink:
        sink_ref, q_ref, k_ref, v_ref, o_ref = refs
    else:
        q_ref, k_ref, v_ref, o_ref = refs
    g = pl.program_id(1)
    kl = window + Q_BLOCK
    if has_sink:
        s = jnp.concatenate([jnp.full((Q_BLOCK, 1), sink_ref[g * r + h], F32) for h in range(r)], axis=0)
    for u in range(qb):
        i = pl.program_id(2) * qb + u
        rows = slice(u * Q_BLOCK, (u + 1) * Q_BLOCK)
        start = pl.multiple_of(jnp.maximum(i * Q_BLOCK - window, 0), Q_BLOCK)
        kb = k_ref[pl.ds(start, kl), :].astype(BF16)
        vb = v_ref[pl.ds(start, kl), :].astype(BF16)
        qs = _stack_heads(q_ref[rows, :], r).astype(BF16)
        logits = _nt(qs, kb) * SCALE
        q_pos = i * Q_BLOCK + lax.broadcasted_iota(jnp.int32, (Q_BLOCK, kl), 0)
        k_pos = start + lax.broadcasted_iota(jnp.int32, (Q_BLOCK, kl), 1)
        rel = q_pos - k_pos
        mask = _tile_rows((rel >= 0) & (rel <= window), r)
        logits = jnp.where(mask, logits, NEG_INF)
        m = jnp.max(logits, axis=-1, keepdims=True)
        if has_sink:
            m = jnp.maximum(m, s)
        p = jnp.exp(logits - m)
        den = jnp.sum(p, axis=-1, keepdims=True)
        if has_sink:
            den = den + jnp.exp(s - m)
        o = jnp.dot((p / den).astype(BF16), vb, preferred_element_type=F32)
        for h in range(r):
            o_ref[rows, h * HEAD_DIM:(h + 1) * HEAD_DIM] = o[h * Q_BLOCK:(h + 1) * Q_BLOCK].astype(o_ref.dtype)


def window_prompt_attn(y, bsz, t, g_kv, q_col, k_col, v_col, window, sink, out_dtype):
    r = N_HEADS // g_kv
    assert window + Q_BLOCK <= t
    rw = r * HEAD_DIM
    qb = _pick(t // Q_BLOCK, (8, 4, 2, 1))
    nb = t // (Q_BLOCK * qb)
    y3 = y.reshape(bsz, t, y.shape[1])
    in_specs = [pl.BlockSpec((None, Q_BLOCK * qb, rw), lambda b, g, i: (b, i, q_col // r + g)),
                pl.BlockSpec((None, t, HEAD_DIM), lambda b, g, i: (b, 0, k_col + g)),
                pl.BlockSpec((None, t, HEAD_DIM), lambda b, g, i: (b, 0, v_col + g))]
    args = [y3, y3, y3]
    if sink is not None:
        in_specs = [pl.BlockSpec(memory_space=pltpu.SMEM)] + in_specs
        args = [sink] + args
    out = pl.pallas_call(
        functools.partial(_win_kernel, window=window, r=r, qb=qb, has_sink=sink is not None),
        grid=(bsz, g_kv, nb),
        in_specs=in_specs,
        out_specs=pl.BlockSpec((None, Q_BLOCK * qb, rw), lambda b, g, i: (b, i, g)),
        out_shape=jax.ShapeDtypeStruct((bsz, t, N_HEADS * HEAD_DIM), out_dtype),
        compiler_params=_cparams(("parallel", "parallel", "arbitrary")),
        name="window_prompt",
    )(*args)
    return out.reshape(bsz * t, N_HEADS * HEAD_DIM)


SB_UNROLL = 3
SB_CUTOFF = -104.0


def _softplus_neg_abs(z):
    return jnp.log(1.0 + jnp.exp(-jnp.abs(z)))


def _sb_kernel(q_ref, k_ref, v_ref, o_ref, *, r, qb):
    rq = r * Q_BLOCK
    row = lax.broadcasted_iota(jnp.int32, (Q_BLOCK, Q_BLOCK), 0)
    col = lax.broadcasted_iota(jnp.int32, (Q_BLOCK, Q_BLOCK), 1)
    later_sel = (row > col).astype(BF16)
    for blk in range(qb):
        i = pl.program_id(2) * qb + blk
        rows = slice(blk * Q_BLOCK, (blk + 1) * Q_BLOCK)
        qs = _stack_heads(q_ref[rows, :], r).astype(BF16)
        n_it = (i + SB_UNROLL) // SB_UNROLL

        def cond(carry, n_it=n_it):
            t, live, _, _ = carry
            return jnp.logical_and(t < n_it, live > 0)

        def body(carry, i=i, qs=qs):
            t, _, acc, cs = carry
            for u in range(SB_UNROLL):
                j = i - (t * SB_UNROLL + u)
                q_lim = jnp.where(j >= 0, i * Q_BLOCK, -Q_BLOCK)
                off = pl.multiple_of(jnp.maximum(j, 0) * Q_BLOCK, Q_BLOCK)
                kb = k_ref[pl.ds(off, Q_BLOCK), :].astype(BF16)
                vb = v_ref[pl.ds(off, Q_BLOCK), :].astype(BF16)
                z = _nt(qs, kb) * SCALE
                causal = _tile_rows((off + col) < (q_lim + row), r)
                sp = _softplus_neg_abs(z)
                log_beta = jnp.minimum(z, 0.0) - sp
                log_keep = jnp.where(causal, jnp.minimum(-z, 0.0) - sp, 0.0)
                later = _dot2(log_keep, later_sel) + cs
                a = jnp.where(causal, jnp.exp(log_beta + later), 0.0)
                acc = acc + jnp.dot(a.astype(BF16), vb, preferred_element_type=F32)
                cs = cs + jnp.sum(log_keep, axis=-1, keepdims=True)
            live = jnp.where(jnp.max(cs) > SB_CUTOFF, 1, 0)
            return t + 1, live, acc, cs

        init = (jnp.int32(0), jnp.int32(1), jnp.zeros((rq, HEAD_DIM), F32), jnp.zeros((rq, 1), F32))
        _, _, acc, _ = lax.while_loop(cond, body, init)
        for h in range(r):
            o_ref[rows, h * HEAD_DIM:(h + 1) * HEAD_DIM] = acc[h * Q_BLOCK:(h + 1) * Q_BLOCK].astype(o_ref.dtype)


def sb_prompt_attn(y, bsz, t, out_dtype):
    r = N_HEADS // SB_KV
    qb = _pick(t // Q_BLOCK, (2, 1))
    nb = t // (Q_BLOCK * qb)
    rw = r * HEAD_DIM
    y3 = y.reshape(bsz, t, y.shape[1])
    out = pl.pallas_call(
        functools.partial(_sb_kernel, r=r, qb=qb),
        grid=(bsz, SB_KV, nb),
        in_specs=[pl.BlockSpec((None, Q_BLOCK * qb, rw), lambda b, g, i: (b, i, g)),
                  pl.BlockSpec((None, t, HEAD_DIM), lambda b, g, i: (b, 0, N_HEADS + g)),
                  pl.BlockSpec((None, t, HEAD_DIM), lambda b, g, i: (b, 0, N_HEADS + SB_KV + g))],
        out_specs=pl.BlockSpec((None, Q_BLOCK * qb, rw), lambda b, g, i: (b, i, g)),
        out_shape=jax.ShapeDtypeStruct((bsz, t, N_HEADS * HEAD_DIM), out_dtype),
        compiler_params=_cparams(("parallel", "parallel", "arbitrary")),
        name="sb_prompt",
    )(y3, y3, y3)
    return out.reshape(bsz * t, N_HEADS * HEAD_DIM)


SEG_PER_PAGE = 128 // NSA_CMP_STRIDE


def _pool_kernel(*refs, pp, g_kv, prefetch):
    refs = refs[prefetch:]
    x_refs = refs[:pp]
    w0_ref, w1_ref, u0_ref, u1_ref = refs[pp:]
    vps = NSA_CMP_STRIDE * g_kv // SUBLANES
    sub = lax.broadcasted_iota(jnp.int32, (SUBLANES, HEAD_DIM), 0)
    for s_i, x_ref in enumerate(x_refs):
        rows = slice(s_i * SEG_PER_PAGE, (s_i + 1) * SEG_PER_PAGE)
        tiles = [[jnp.zeros((SEG_PER_PAGE, HEAD_DIM), F32) for _ in range(g_kv)] for _ in range(2)]
        for n in range(SEG_PER_PAGE):
            accs = [jnp.zeros((SUBLANES, HEAD_DIM), F32), jnp.zeros((SUBLANES, HEAD_DIM), F32)]
            for v in range(vps):
                xv = x_ref[(n * vps + v) * SUBLANES:(n * vps + v + 1) * SUBLANES, :]
                accs[0] = accs[0] + xv * w0_ref[v]
                accs[1] = accs[1] + xv * w1_ref[v]
            for half in range(2):
                acc = accs[half]
                sh = g_kv
                while sh < SUBLANES:
                    acc = acc + pltpu.roll(acc, sh, 0)
                    sh *= 2
                for g in range(g_kv):
                    k = (n - g) % SUBLANES
                    moved = pltpu.roll(acc, k, 0) if k else acc
                    tiles[half][g] = jnp.where(sub == n, moved, tiles[half][g])
        for g in range(g_kv):
            u0_ref[rows, g * HEAD_DIM:(g + 1) * HEAD_DIM] = tiles[0][g]
            u1_ref[rows, g * HEAD_DIM:(g + 1) * HEAD_DIM] = tiles[1][g]


def _pool_weights(a, g_kv):
    ppv = SUBLANES // g_kv
    a_r = a.reshape(NSA_CMP_LEN // NSA_CMP_STRIDE, NSA_CMP_STRIDE // ppv, ppv, 1, HEAD_DIM)
    a_r = jnp.broadcast_to(a_r, a_r.shape[:3] + (g_kv, HEAD_DIM))
    a_r = a_r.reshape(2, NSA_CMP_STRIDE // ppv, SUBLANES, HEAD_DIM)
    return a_r[0], a_r[1]


def pool_prompt(x, a, pp):
    bsz, t, g_kv, _ = x.shape
    gw = g_kv * HEAD_DIM
    n_pages = t // 128
    xr = x.reshape(bsz, t * g_kv, HEAD_DIM)
    a0, a1 = _pool_weights(a, g_kv)
    in_specs = [pl.BlockSpec((None, 128 * g_kv, HEAD_DIM), lambda b, p, s=s: (b, p * pp + s, 0))
                for s in range(pp)]
    in_specs += [pl.BlockSpec(a0.shape, lambda b, p: (0, 0, 0))] * 2
    seg = pp * SEG_PER_PAGE
    return pl.pallas_call(
        functools.partial(_pool_kernel, pp=pp, g_kv=g_kv, prefetch=0),
        grid=(bsz, n_pages // pp),
        in_specs=in_specs,
        out_specs=[pl.BlockSpec((None, seg, gw), lambda b, p: (b, p, 0))] * 2,
        out_shape=[jax.ShapeDtypeStruct((bsz, t // NSA_CMP_STRIDE, gw), F32)] * 2,
        compiler_params=_cparams(("parallel", "arbitrary")),
        name="pool_prompt",
    )(*([xr] * pp), a0, a1)


def pool_paged(pool, page_table, a, pp):
    db, n_pages = page_table.shape
    n_pool, page, g_kv, _ = pool.shape
    assert page == 128
    gw = g_kv * HEAD_DIM
    pr = pool.reshape(n_pool, page * g_kv, HEAD_DIM)
    a0, a1 = _pool_weights(a, g_kv)
    in_specs = [pl.BlockSpec((None, page * g_kv, HEAD_DIM), lambda b, p, pt, s=s: (pt[b, p * pp + s], 0, 0))
                for s in range(pp)]
    in_specs += [pl.BlockSpec(a0.shape, lambda b, p, pt: (0, 0, 0))] * 2
    seg = pp * SEG_PER_PAGE
    grid_spec = pltpu.PrefetchScalarGridSpec(
        num_scalar_prefetch=1,
        grid=(db, n_pages // pp),
        in_specs=in_specs,
        out_specs=[pl.BlockSpec((None, seg, gw), lambda b, p, pt: (b, p, 0))] * 2,
    )
    return pl.pallas_call(
        functools.partial(_pool_kernel, pp=pp, g_kv=g_kv, prefetch=1),
        grid_spec=grid_spec,
        out_shape=[jax.ShapeDtypeStruct((db, n_pages * SEG_PER_PAGE, gw), F32)] * 2,
        compiler_params=_cparams(("parallel", "arbitrary")),
        name="pool_paged",
    )(page_table, *([pr] * pp), a0, a1)


def _cmp_mlp_kernel(u0_ref, u1_ref, a_ref, pe_ref, w1_ref, w2_ref, o_ref):
    n_seg = u0_ref.shape[0]
    c = jnp.sum(a_ref[...] * pe_ref[...], axis=0, keepdims=True)
    w1 = w1_ref[...].astype(BF16)
    w2 = w2_ref[...].astype(BF16)
    for g in range(u0_ref.shape[1] // HEAD_DIM):
        cols = slice(g * HEAD_DIM, (g + 1) * HEAD_DIM)
        u = u0_ref[:, cols] + pltpu.roll(u1_ref[:, cols], n_seg - 1, 0)
        u = u + c
        h = jax.nn.gelu(jnp.dot(u.astype(BF16), w1, preferred_element_type=F32))
        o_ref[:, cols] = jnp.dot(h.astype(BF16), w2, preferred_element_type=F32)


def cmp_mlp(u0, u1, a, pe, w1, w2):
    bx, n_seg, gw = u0.shape
    full = lambda arr: pl.BlockSpec(arr.shape, lambda b: (0,) * arr.ndim)
    rows = pl.BlockSpec((None, n_seg, gw), lambda b: (b, 0, 0))
    return pl.pallas_call(
        _cmp_mlp_kernel,
        grid=(bx,),
        in_specs=[rows, rows, full(a), full(pe), full(w1), full(w2)],
        out_specs=rows,
        out_shape=jax.ShapeDtypeStruct((bx, n_seg, gw), F32),
        compiler_params=_cparams(("parallel",)),
        name="cmp_mlp",
    )(u0, u1, a, pe, w1, w2)


def _cover_matrix(n_rows, n_cmp, n_cols, n_sel):
    cs = np.arange(n_rows) * NSA_CMP_STRIDE
    ss = np.arange(n_cols) * NSA_SEL_LEN
    m = (cs[:, None] < ss[None, :] + NSA_SEL_LEN) & (cs[:, None] + NSA_CMP_LEN > ss[None, :])
    m = m & (np.arange(n_rows)[:, None] < n_cmp) & (np.arange(n_cols)[None, :] < n_sel)
    return jnp.asarray(m, BF16)


def _masked_softmax(logits, mask):
    logits = jnp.where(mask, logits, NEG_INF)
    m = jnp.max(logits, axis=-1, keepdims=True)
    m = jnp.where(m == NEG_INF, 0.0, m)
    e = jnp.exp(logits - m)
    s = jnp.sum(e, axis=-1, keepdims=True)
    return e / jnp.where(s > 0, s, 1.0)


def _nsa_prompt_cmp_kernel(q_ref, kc_ref, vc_ref, cover_ref, ocmp_ref, sel_ref, *, qb, **kw):
    for blk in range(qb):
        rows = slice(blk * Q_BLOCK, (blk + 1) * Q_BLOCK)
        _nsa_prompt_cmp_block(pl.program_id(2) * qb + blk, q_ref.at[rows, :], kc_ref, vc_ref, cover_ref,
                              ocmp_ref.at[rows, :], sel_ref.at[rows, :], **kw)


def _nsa_prompt_cmp_block(i, q_ref, kc_ref, vc_ref, cover_ref, ocmp_ref, sel_ref, *, r, n_cmp, n_sel, n_top):
    qs = _stack_heads(q_ref[...], r).astype(BF16)
    nc = kc_ref.shape[0]

    lc = _nt(qs, kc_ref[...].astype(BF16)) * SCALE
    qp_c = i * Q_BLOCK + lax.broadcasted_iota(jnp.int32, (Q_BLOCK, nc), 0)
    n_c = lax.broadcasted_iota(jnp.int32, (Q_BLOCK, nc), 1)
    cmask = _tile_rows((n_c * NSA_CMP_STRIDE + NSA_CMP_LEN - 1 <= qp_c) & (n_c < n_cmp), r)
    pc = _masked_softmax(lc, cmask)
    ocmp = jnp.dot(pc.astype(BF16), vc_ref[...].astype(BF16), preferred_element_type=F32)
    for h in range(r):
        ocmp_ref[:, h * HEAD_DIM:(h + 1) * HEAD_DIM] = ocmp[h * Q_BLOCK:(h + 1) * Q_BLOCK]

    p_sum = pc[0:Q_BLOCK]
    for h in range(1, r):
        p_sum = p_sum + pc[h * Q_BLOCK:(h + 1) * Q_BLOCK]
    nj = -(-n_sel // SUBLANES) * SUBLANES
    cover_t = cover_ref[...]
    hi, mid, lo = _split3(p_sum)
    imp = (_nt(cover_t, hi) + _nt(cover_t, mid) + _nt(cover_t, lo))[:nj]
    qp = i * Q_BLOCK + lax.broadcasted_iota(jnp.int32, (nj, Q_BLOCK), 1)
    jj = lax.broadcasted_iota(jnp.int32, (nj, Q_BLOCK), 0)
    cur = qp // NSA_SEL_LEN
    vis = jj * NSA_SEL_LEN <= qp
    forced = (jj == 0) | (jj == cur) | (jj == cur - 1)
    score = jnp.where(vis, jnp.where(forced, SEL_FORCE, imp), -1.0)
    score = jnp.where(jj < n_sel, score, -2.0)
    rank = jnp.zeros((nj, Q_BLOCK), F32)
    for t in range(n_sel):
        st = score[t:t + 1, :]
        beats = (st > score) | ((st == score) & (jj > t))
        rank = rank + jnp.where(beats, 1.0, 0.0)
    sel_t = jnp.where((rank < n_top) & (score >= 0.0), 1.0, 0.0)
    if nj < LANES:
        sel_t = jnp.concatenate([sel_t, jnp.zeros((LANES - nj, Q_BLOCK), F32)], axis=0)
    sel_ref[...] = sel_t.T.astype(sel_ref.dtype)


def _nsa_prompt_slc_kernel(q_ref, sel_ref, ks_ref, vs_ref, oslc_ref, *, qb, **kw):
    for blk in range(qb):
        rows = slice(blk * Q_BLOCK, (blk + 1) * Q_BLOCK)
        _nsa_prompt_slc_block(pl.program_id(2) * qb + blk, q_ref.at[rows, :], sel_ref.at[rows, :], ks_ref,
                              vs_ref, oslc_ref.at[rows, :], **kw)


def _nsa_prompt_slc_block(i, q_ref, sel_ref, ks_ref, vs_ref, oslc_ref, *, r, kc):
    rq = r * Q_BLOCK
    qs = _stack_heads(q_ref[...], r).astype(BF16)
    sel = sel_ref[...]
    row = lax.broadcasted_iota(jnp.int32, (Q_BLOCK, kc), 0)
    col = lax.broadcasted_iota(jnp.int32, (Q_BLOCK, kc), 1)

    def body(kb, carry):
        m, l, acc = carry
        off = pl.multiple_of(kb * kc, kc)
        k = ks_ref[pl.ds(off, kc), :].astype(BF16)
        v = vs_ref[pl.ds(off, kc), :].astype(BF16)
        s = _nt(qs, k) * SCALE
        expand = jnp.where(row == (off + col) // NSA_SEL_LEN, 1.0, 0.0).astype(BF16)
        picked = jnp.dot(sel, expand, preferred_element_type=F32) > 0.5
        mask = _tile_rows(picked & ((off + col) <= (i * Q_BLOCK + row)), r)
        s = jnp.where(mask, s, NEG_INF)
        m_new = jnp.maximum(m, jnp.max(s, axis=-1, keepdims=True))
        m_safe = jnp.where(m_new == NEG_INF, 0.0, m_new)
        p = jnp.exp(s - m_safe)
        alpha = jnp.exp(m - m_safe)
        l = alpha * l + jnp.sum(p, axis=-1, keepdims=True)
        acc = alpha * acc + jnp.dot(p.astype(BF16), v, preferred_element_type=F32)
        return m_new, l, acc

    init = (jnp.full((rq, 1), NEG_INF, F32), jnp.zeros((rq, 1), F32), jnp.zeros((rq, HEAD_DIM), F32))
    n_it = ((i + 1) * Q_BLOCK + kc - 1) // kc
    _, l, acc = lax.fori_loop(0, n_it, body, init)
    oslc = acc / jnp.where(l > 0, l, 1.0)
    for h in range(r):
        oslc_ref[:, h * HEAD_DIM:(h + 1) * HEAD_DIM] = oslc[h * Q_BLOCK:(h + 1) * Q_BLOCK]


def nsa_prompt_global(y, kcmp, vcmp, bsz, t):
    r = N_HEADS // NSA_KV
    qb = _pick(t // Q_BLOCK, (2, 1))
    nb = t // (Q_BLOCK * qb)
    rw = r * HEAD_DIM
    n_seg = t // NSA_CMP_STRIDE
    n_cmp = n_seg - NSA_CMP_LEN // NSA_CMP_STRIDE + 1
    n_sel = -(-t // NSA_SEL_LEN)
    assert n_sel <= LANES and t % NSA_SEL_LEN == 0
    n_top = min(NSA_TOP, n_sel)
    y3 = y.reshape(bsz, t, y.shape[1])
    cover = _cover_matrix(n_seg, n_cmp, LANES, n_sel).T
    ks_col = N_HEADS + 2 * NSA_KV
    vs_col = N_HEADS + 3 * NSA_KV
    o_shape = jax.ShapeDtypeStruct((bsz, t, N_HEADS * HEAD_DIM), F32)
    q_spec = pl.BlockSpec((None, Q_BLOCK * qb, rw), lambda b, g, i: (b, i, g))
    sel_spec = pl.BlockSpec((None, None, Q_BLOCK * qb, LANES), lambda b, g, i: (b, g, i, 0))
    ocmp, sel = pl.pallas_call(
        functools.partial(_nsa_prompt_cmp_kernel, qb=qb, r=r, n_cmp=n_cmp, n_sel=n_sel, n_top=n_top),
        grid=(bsz, NSA_KV, nb),
        in_specs=[q_spec,
                  pl.BlockSpec((None, n_seg, HEAD_DIM), lambda b, g, i: (b, 0, g)),
                  pl.BlockSpec((None, n_seg, HEAD_DIM), lambda b, g, i: (b, 0, g)),
                  pl.BlockSpec(cover.shape, lambda b, g, i: (0, 0))],
        out_specs=[q_spec, sel_spec],
        out_shape=[o_shape, jax.ShapeDtypeStruct((bsz, NSA_KV, t, LANES), BF16)],
        compiler_params=_cparams(("parallel", "parallel", "arbitrary")),
        name="nsa_prompt_cmp",
    )(y3, kcmp, vcmp, cover)
    oslc = pl.pallas_call(
        functools.partial(_nsa_prompt_slc_kernel, qb=qb, r=r, kc=_pick(t, (512, 256, 128))),
        grid=(bsz, NSA_KV, nb),
        in_specs=[q_spec, sel_spec,
                  pl.BlockSpec((None, t, HEAD_DIM), lambda b, g, i: (b, 0, ks_col + g)),
                  pl.BlockSpec((None, t, HEAD_DIM), lambda b, g, i: (b, 0, vs_col + g))],
        out_specs=q_spec,
        out_shape=o_shape,
        compiler_params=_cparams(("parallel", "parallel", "arbitrary")),
        name="nsa_prompt_slc",
    )(y3, sel, y3, y3)
    return ocmp.reshape(bsz * t, -1), oslc.reshape(bsz * t, -1)


def _combine_kernel(gl_ref, oc_ref, os_ref, ow_ref, o_ref):
    gates = jax.nn.sigmoid(gl_ref[...])
    for h in range(N_HEADS):
        sl = slice(h * HEAD_DIM, (h + 1) * HEAD_DIM)
        acc = gates[:, 3 * h:3 * h + 1] * oc_ref[:, sl]
        acc = acc + gates[:, 3 * h + 1:3 * h + 2] * os_ref[:, sl]
        acc = acc + gates[:, 3 * h + 2:3 * h + 3] * ow_ref[:, sl]
        o_ref[:, sl] = acc.astype(o_ref.dtype)


def nsa_combine(gate_logits, o_cmp, o_slc, o_win, tm):
    m, d = o_cmp.shape
    row = lambda w: pl.BlockSpec((tm, w), lambda i: (i, 0))
    return pl.pallas_call(
        _combine_kernel,
        grid=(m // tm,),
        in_specs=[row(gate_logits.shape[1]), row(d), row(d), row(d)],
        out_specs=row(d),
        out_shape=jax.ShapeDtypeStruct((m, d), BF16),
        compiler_params=_cparams(("parallel",)),
        name="nsa_combine",
    )(gate_logits, o_cmp, o_slc, o_win)


def _block_diag_q(q, g_kv):
    db = q.shape[0]
    r = N_HEADS // g_kv
    qh = q.reshape(db, N_HEADS, 1, HEAD_DIM)
    onehot = (jnp.arange(N_HEADS)[:, None] // r == jnp.arange(g_kv)[None, :]).astype(q.dtype)
    return (qh * onehot[None, :, :, None]).reshape(db, N_HEADS, g_kv * HEAD_DIM).astype(BF16)


def _diag_extract(o_all, g_kv):
    r = N_HEADS // g_kv
    hrow = lax.broadcasted_iota(jnp.int32, (N_HEADS, HEAD_DIM), 0)
    out = jnp.zeros((N_HEADS, HEAD_DIM), F32)
    for g in range(g_kv):
        out = out + jnp.where(hrow // r == g, o_all[:, g * HEAD_DIM:(g + 1) * HEAD_DIM], 0.0)
    return out


def _bf16_round(x):
    return x.astype(BF16).astype(F32)


def _group_rows(x_ref, g_kv):
    r = N_HEADS // g_kv
    hrow = lax.broadcasted_iota(jnp.int32, (N_HEADS, HEAD_DIM), 0)
    out = jnp.zeros((N_HEADS, HEAD_DIM), F32)
    for g in range(g_kv):
        out = out + jnp.where(hrow // r == g, x_ref[g:g + 1, :], 0.0)
    return out


def _win_sample_kernel(*refs, window, past_len, g_kv, has_sink):
    if has_sink:
        sink_ref, q_ref, kb_ref, vb_ref, kn_ref, vn_ref, o_ref = refs
    else:
        q_ref, kb_ref, vb_ref, kn_ref, vn_ref, o_ref = refs
    rows = kb_ref.shape[0]
    wb = rows // g_kv
    r = N_HEADS // g_kv
    q = q_ref[...]
    z = _nt(q, kb_ref[...].astype(BF16)) * SCALE
    zn = jnp.sum(q.astype(F32) * _bf16_round(_group_rows(kn_ref, g_kv)), axis=-1, keepdims=True) * SCALE
    hrow = lax.broadcasted_iota(jnp.int32, (N_HEADS, rows), 0)
    lane = lax.broadcasted_iota(jnp.int32, (N_HEADS, rows), 1)
    k_pos = past_len - wb + lane // g_kv
    rel = past_len - k_pos
    mask = (lane % g_kv == hrow // r) & (rel >= 0) & (rel <= window) & (k_pos >= 0)
    z = jnp.where(mask, z, NEG_INF)
    m = jnp.maximum(jnp.max(z, axis=-1, keepdims=True), zn)
    if has_sink:
        m = jnp.maximum(m, sink_ref[...])
    p = jnp.exp(z - m)
    pn = jnp.exp(zn - m)
    den = jnp.sum(p, axis=-1, keepdims=True) + pn
    if has_sink:
        den = den + jnp.exp(sink_ref[...] - m)
    o = jnp.dot((p / den).astype(BF16), vb_ref[...].astype(BF16), preferred_element_type=F32)
    o = o + _bf16_round(pn / den) * _bf16_round(_group_rows(vn_ref, g_kv))
    o_ref[...] = o.astype(o_ref.dtype)


def window_sample_attn(q, k_new, v_new, k_buf, v_buf, past_len, window, sink, out_dtype):
    db, wb, g_kv, _ = k_buf.shape
    rows = wb * g_kv
    in_specs = [pl.BlockSpec((None, N_HEADS, HEAD_DIM), lambda b: (b, 0, 0)),
                pl.BlockSpec((None, rows, HEAD_DIM), lambda b: (b, 0, 0)),
                pl.BlockSpec((None, rows, HEAD_DIM), lambda b: (b, 0, 0)),
                pl.BlockSpec((None, g_kv, HEAD_DIM), lambda b: (b, 0, 0)),
                pl.BlockSpec((None, g_kv, HEAD_DIM), lambda b: (b, 0, 0))]
    args = [q.reshape(db, N_HEADS, HEAD_DIM).astype(BF16),
            k_buf.reshape(db, rows, HEAD_DIM), v_buf.reshape(db, rows, HEAD_DIM),
            k_new.reshape(db, g_kv, HEAD_DIM), v_new.reshape(db, g_kv, HEAD_DIM)]
    if sink is not None:
        in_specs = [pl.BlockSpec((N_HEADS, 1), lambda b: (0, 0))] + in_specs
        args = [sink.reshape(N_HEADS, 1)] + args
    out = pl.pallas_call(
        functools.partial(_win_sample_kernel, window=window, past_len=past_len, g_kv=g_kv,
                          has_sink=sink is not None),
        grid=(db,),
        in_specs=in_specs,
        out_specs=pl.BlockSpec((None, N_HEADS, HEAD_DIM), lambda b: (b, 0, 0)),
        out_shape=jax.ShapeDtypeStruct((db, N_HEADS, HEAD_DIM), out_dtype),
        compiler_params=_cparams(("parallel",)),
        name="window_sample",
    )(*args)
    return out.reshape(db, N_HEADS * HEAD_DIM)


def _sb_sample_kernel(*refs, pp, n_pages, first_rank, q_pos, g_kv):
    refs = refs[1:]
    q_ref, cs0_ref, acc0_ref = refs[:3]
    k_refs = refs[3:3 + pp]
    v_refs = refs[3 + pp:3 + 2 * pp]
    cs_out_ref, acc_out_ref, cs_ref, acc_ref = refs[3 + 2 * pp:]
    p = pl.program_id(1)
    r = N_HEADS // g_kv
    n_ch = g_kv
    cp = LANES // g_kv

    @pl.when(p == 0)
    def _():
        cs_ref[...] = cs0_ref[:, 0:1]
        acc_ref[...] = acc0_ref[...]

    q = q_ref[...]
    row = lax.broadcasted_iota(jnp.int32, (LANES, LANES), 0)
    col = lax.broadcasted_iota(jnp.int32, (LANES, LANES), 1)
    later_sel = (row > col).astype(BF16)
    n_rank = pp * n_ch
    chunks = [(s, c) for s in range(pp) for c in reversed(range(n_ch))]
    srow = lax.broadcasted_iota(jnp.int32, (n_rank * N_HEADS, LANES), 0)
    slane = lax.broadcasted_iota(jnp.int32, (n_rank * N_HEADS, LANES), 1)
    own = (slane % g_kv) == ((srow % N_HEADS) // r)
    rank = srow // N_HEADS
    k_pos = ((n_pages - 1 - first_rank - p * pp) * n_ch - rank) * cp + slane // g_kv
    causal = own & (k_pos < q_pos)
    z_pages = [_nt(q, k_refs[s][...].astype(BF16)) for s in range(pp)]
    z = jnp.concatenate([z_pages[s][:, c * LANES:(c + 1) * LANES] for s, c in chunks],
                        axis=0) * SCALE
    sp = _softplus_neg_abs(z)
    log_beta = jnp.minimum(z, 0.0) - sp
    log_keep = jnp.where(causal, jnp.minimum(-z, 0.0) - sp, 0.0)
    rs = jnp.sum(log_keep, axis=-1, keepdims=True)
    cs = cs_ref[...]
    carries = []
    for ci in range(n_rank):
        carries.append(cs)
        cs = cs + rs[ci * N_HEADS:(ci + 1) * N_HEADS]
    later = _dot2(log_keep, later_sel) + jnp.concatenate(carries, axis=0)
    a = jnp.where(causal, jnp.exp(log_beta + later), 0.0).astype(BF16)
    acc = acc_ref[...]
    for s in range(pp):
        ranks = [s * n_ch + (n_ch - 1 - c) for c in range(n_ch)]
        a_page = jnp.concatenate([a[k * N_HEADS:(k + 1) * N_HEADS] for k in ranks], axis=1)
        acc = acc + jnp.dot(a_page, v_refs[s][...].astype(BF16), preferred_element_type=F32)
    cs_ref[...] = cs
    acc_ref[...] = acc

    @pl.when(p == pl.num_programs(1) - 1)
    def _():
        cs_out_ref[...] = jnp.broadcast_to(cs, cs_out_ref.shape)
        acc_out_ref[...] = acc


def _sb_sample_sweep(q16, pk, pv, page_table, cs0, acc0, first_rank, n_steps, pp, g_kv):
    db, n_pages = page_table.shape
    rows = pk.shape[1]
    q_pos = n_pages * (rows // g_kv)
    page_spec = lambda s: pl.BlockSpec(
        (None, rows, HEAD_DIM),
        lambda b, p, pt, s=s: (pt[b, n_pages - 1 - first_rank - (p * pp + s)], 0, 0))
    state_spec = pl.BlockSpec((None, N_HEADS, HEAD_DIM), lambda b, p, pt: (b, 0, 0))
    grid_spec = pltpu.PrefetchScalarGridSpec(
        num_scalar_prefetch=1,
        grid=(db, n_steps),
        in_specs=[state_spec, state_spec, state_spec] + [page_spec(s) for s in range(pp)] * 2,
        out_specs=[state_spec, state_spec],
        scratch_shapes=[pltpu.VMEM((N_HEADS, 1), F32), pltpu.VMEM((N_HEADS, HEAD_DIM), F32)],
    )
    state = jax.ShapeDtypeStruct((db, N_HEADS, HEAD_DIM), F32)
    return pl.pallas_call(
        functools.partial(_sb_sample_kernel, pp=pp, n_pages=n_pages, first_rank=first_rank, q_pos=q_pos,
                          g_kv=g_kv),
        grid_spec=grid_spec,
        out_shape=[state, state],
        compiler_params=_cparams(("parallel", "arbitrary")),
        name="sb_sample",
    )(page_table, q16, cs0, acc0, *([pk] * pp), *([pv] * pp))


def sb_sample_attn(q, pool_k, pool_v, page_table, pp, out_dtype):
    db, n_pages = page_table.shape
    n_pool, page, g_kv, _ = pool_k.shape
    assert page == 128 and n_pages % pp == 0
    rows = page * g_kv
    pk = pool_k.reshape(n_pool, rows, HEAD_DIM)
    pv = pool_v.reshape(n_pool, rows, HEAD_DIM)
    q16 = q.reshape(db, N_HEADS, HEAD_DIM).astype(BF16)
    zero = jnp.zeros((db, N_HEADS, HEAD_DIM), F32)
    cs, acc = _sb_sample_sweep(q16, pk, pv, page_table, zero, zero, 0, 1, pp, g_kv)
    if n_pages > pp:
        rest = lambda st: _sb_sample_sweep(q16, pk, pv, page_table, st[0], st[1], pp, n_pages // pp - 1,
                                           pp, g_kv)[1]
        acc = lax.cond(jnp.max(cs) > SB_CUTOFF, rest, lambda st: st[1], (cs, acc))
    return acc.astype(out_dtype).reshape(db, N_HEADS * HEAD_DIM)


def _nsa_sample_cmp_kernel(q_ref, kc_ref, vc_ref, cover_ref, ocmp_ref, idx_ref, val_ref,
                           *, n_cmp, n_sel, n_top, q_pos):
    q = q_ref[...]
    nc = kc_ref.shape[0]
    ns = cover_ref.shape[1]
    r = N_HEADS // NSA_KV
    lc = _nt(q, kc_ref[...].astype(BF16)) * SCALE
    n_c = lax.broadcasted_iota(jnp.int32, (N_HEADS, nc), 1)
    cmask = (n_c * NSA_CMP_STRIDE + NSA_CMP_LEN - 1 <= q_pos) & (n_c < n_cmp)
    pc = _masked_softmax(lc, cmask)
    o_all = jnp.dot(pc.astype(BF16), vc_ref[...].astype(BF16), preferred_element_type=F32)
    ocmp_ref[...] = _diag_extract(o_all, NSA_KV)

    grow = lax.broadcasted_iota(jnp.int32, (SUBLANES, N_HEADS), 0)
    hcol = lax.broadcasted_iota(jnp.int32, (SUBLANES, N_HEADS), 1)
    group_sel = (hcol // r == grow).astype(BF16)
    p_sum = _dot3r(group_sel, pc)
    imp = _dot3(p_sum, cover_ref[...])
    jj = lax.broadcasted_iota(jnp.int32, (SUBLANES, ns), 1)
    cur = q_pos // NSA_SEL_LEN
    vis = jj * NSA_SEL_LEN <= q_pos
    forced = (jj == 0) | (jj == cur) | (jj == cur - 1)
    score = jnp.where(vis, jnp.where(forced, SEL_FORCE, imp), -1.0)
    score = jnp.where(jj < n_sel, score, -2.0)
    jf = jj.astype(F32)
    tl = lax.broadcasted_iota(jnp.int32, (SUBLANES, LANES), 1)
    idx_out = jnp.zeros((SUBLANES, LANES), F32)
    val_out = jnp.full((SUBLANES, LANES), -1.0, F32)
    for t in range(n_top):
        mx = jnp.max(score, axis=-1, keepdims=True)
        first = jnp.min(jnp.where(score == mx, jf, float(ns)), axis=-1, keepdims=True)
        idx_out = jnp.where(tl == t, first, idx_out)
        val_out = jnp.where(tl == t, mx, val_out)
        score = jnp.where(jf == first, NEG_INF, score)
    idx_ref[...] = idx_out.astype(jnp.int32)
    val_ref[...] = val_out


def nsa_sample_cmp(q, kcmp, vcmp, n_cmp, n_sel, q_pos):
    db, nc, gw = kcmp.shape
    ns = -(-n_sel // LANES) * LANES
    n_top = min(NSA_TOP, n_sel)
    cover = _cover_matrix(nc, n_cmp, ns, n_sel)
    qbd = _block_diag_q(q, NSA_KV)
    return pl.pallas_call(
        functools.partial(_nsa_sample_cmp_kernel, n_cmp=n_cmp, n_sel=n_sel, n_top=n_top, q_pos=q_pos),
        grid=(db,),
        in_specs=[pl.BlockSpec((None, N_HEADS, gw), lambda b: (b, 0, 0)),
                  pl.BlockSpec((None, nc, gw), lambda b: (b, 0, 0)),
                  pl.BlockSpec((None, nc, gw), lambda b: (b, 0, 0)),
                  pl.BlockSpec(cover.shape, lambda b: (0, 0))],
        out_specs=[pl.BlockSpec((None, N_HEADS, HEAD_DIM), lambda b: (b, 0, 0)),
                   pl.BlockSpec((None, SUBLANES, LANES), lambda b: (b, 0, 0)),
                   pl.BlockSpec((None, SUBLANES, LANES), lambda b: (b, 0, 0))],
        out_shape=[jax.ShapeDtypeStruct((db, N_HEADS, HEAD_DIM), F32),
                   jax.ShapeDtypeStruct((db, SUBLANES, LANES), jnp.int32),
                   jax.ShapeDtypeStruct((db, SUBLANES, LANES), F32)],
        compiler_params=_cparams(("parallel",)),
        name="nsa_sample_cmp",
    )(qbd, kcmp, vcmp, cover)


def _nsa_sample_slc_kernel(*refs, n_top, n_sel, q_pos):
    idx_ref, ok_ref, pt_ref = refs[:3]
    refs = refs[3:]
    q_ref = refs[0]
    k_refs = refs[1:1 + n_top]
    v_refs = refs[1 + n_top:1 + 2 * n_top]
    kn_ref, vn_ref, o_ref = refs[1 + 2 * n_top:]
    b = pl.program_id(0)
    g = pl.program_id(1)
    r = N_HEADS // NSA_KV
    rows = NSA_SEL_LEN * NSA_KV
    base = (b * NSA_KV + g) * n_top

    @pl.when(g == 0)
    def _():
        o_ref[...] = jnp.zeros_like(o_ref)

    q = q_ref[...]
    lane = lax.broadcasted_iota(jnp.int32, (N_HEADS, rows), 1)
    lane_pos = lane // NSA_KV
    lane_own = (lane % NSA_KV) == g
    zs = []
    new_ok = jnp.int32(0)
    for t in range(n_top):
        j = idx_ref[base + t]
        ok = ok_ref[base + t]
        in_pool = j < n_sel - 1
        z = _nt(q, k_refs[t][...].astype(BF16)) * SCALE
        last_pos = jnp.where((ok > 0) & in_pool, q_pos, -1)
        zs.append(jnp.where(lane_own & ((j * NSA_SEL_LEN + lane_pos) <= last_pos), z, NEG_INF))
        new_ok = new_ok | jnp.where((ok > 0) & jnp.logical_not(in_pool), 1, 0)
    new_vis = (new_ok > 0) & ((n_sel - 1) * NSA_SEL_LEN <= q_pos)
    zn = jnp.sum(q.astype(F32) * _bf16_round(kn_ref[...]), axis=-1, keepdims=True) * SCALE
    zn = zn + jnp.where(new_vis, 0.0, NEG_INF)
    m = zn
    for z in zs:
        m = jnp.maximum(m, jnp.max(z, axis=-1, keepdims=True))
    m = jnp.where(m == NEG_INF, 0.0, m)
    pn = jnp.exp(zn - m)
    den = pn
    ps = []
    for z in zs:
        e = jnp.exp(z - m)
        ps.append(e)
        den = den + jnp.sum(e, axis=-1, keepdims=True)
    den = jnp.where(den > 0, den, 1.0)
    o = _bf16_round(pn / den) * _bf16_round(vn_ref[...])
    for t in range(n_top):
        o = o + jnp.dot((ps[t] / den).astype(BF16), v_refs[t][...].astype(BF16),
                        preferred_element_type=F32)
    hrow = lax.broadcasted_iota(jnp.int32, (N_HEADS, HEAD_DIM), 0)
    o_ref[...] = jnp.where(hrow // r == g, o, o_ref[...])


def nsa_sample_slc(q, top_idx, top_val, pool_k, pool_v, page_table, k_new, v_new, n_sel, q_pos):
    db, n_pages = page_table.shape
    n_top = min(NSA_TOP, n_sel)
    n_pool, page, g_kv, _ = pool_k.shape
    assert g_kv == NSA_KV
    half = page // NSA_SEL_LEN
    rows = NSA_SEL_LEN * g_kv
    pk = pool_k.reshape(n_pool * half, rows, HEAD_DIM)
    pv = pool_v.reshape(n_pool * half, rows, HEAD_DIM)
    idx = top_idx[:, :NSA_KV, :n_top].reshape(-1)
    ok = (top_val[:, :NSA_KV, :n_top] >= 0.0).astype(jnp.int32).reshape(-1)

    def blk_spec(t):
        def index_map(b, g, idx_ref, ok_ref, pt_ref):
            j = jnp.minimum(idx_ref[(b * NSA_KV + g) * n_top + t], n_sel - 2)
            return (pt_ref[b, j // half] * half + j % half, 0, 0)
        return pl.BlockSpec((None, rows, HEAD_DIM), index_map)

    row_spec = pl.BlockSpec((None, 1, HEAD_DIM), lambda b, g, *_: (b * NSA_KV + g, 0, 0))
    grid_spec = pltpu.PrefetchScalarGridSpec(
        num_scalar_prefetch=3,
        grid=(db, NSA_KV),
        in_specs=([pl.BlockSpec((None, N_HEADS, HEAD_DIM), lambda b, g, *_: (b, 0, 0))]
                  + [blk_spec(t) for t in range(n_top)] * 2 + [row_spec, row_spec]),
        out_specs=pl.BlockSpec((None, N_HEADS, HEAD_DIM), lambda b, g, *_: (b, 0, 0)),
    )
    return pl.pallas_call(
        functools.partial(_nsa_sample_slc_kernel, n_top=n_top, n_sel=n_sel, q_pos=q_pos),
        grid_spec=grid_spec,
        out_shape=jax.ShapeDtypeStruct((db, N_HEADS, HEAD_DIM), F32),
        compiler_params=_cparams(("parallel", "arbitrary")),
        name="nsa_sample_slc",
    )(idx, ok, page_table, q.reshape(db, N_HEADS, HEAD_DIM).astype(BF16),
      *([pk] * n_top), *([pv] * n_top),
      k_new.reshape(db * NSA_KV, 1, HEAD_DIM), v_new.reshape(db * NSA_KV, 1, HEAD_DIM))


def _rope_tables(pos):
    half = HEAD_DIM // 2
    inv = ROPE_THETA ** (-jnp.arange(half, dtype=F32) / half)
    ang = pos.astype(F32)[:, None] * inv[None, :]
    cos, sin = jnp.cos(ang), jnp.sin(ang)
    return jnp.concatenate([cos, cos], -1), jnp.concatenate([-sin, sin], -1)


def _pick(n, cands):
    for c in cands:
        if n % c == 0:
            return c
    return n


def _res_ln_rows(hp, hs, w16_all, widx, xp32, xs32, g, b, scale):
    tm = _pick(xp32.shape[0], (256, 128))
    op32, op16, os32, os16 = res_ln(hp, hs, w16_all, widx, xp32, xs32, g, b, scale, tm, _pick(tm, (128,)))
    return (op32, op16), (os32, os16)


def _ffn_step(xp, xs, w_in_all, w_out16_all, which, g, b):
    m = xp[0].shape[0]
    f = w_out16_all.shape[-2]
    hp, hs = glu(xp[1], xs[1], w_in_all, which, _pick(m, (1024, 512, 256, 128)), _pick(f, (512, 256, 128)))
    return _res_ln_rows(hp, hs, w_out16_all, which, xp[0], xs[0], g, b, 0.5)


def _project(xp16, xs16, w16_all, widx, flags, tab_p, tab_s, tn):
    tm = _pick(min(xp16.shape[0], tab_p[0].shape[0]), (1024, 512, 256, 128))
    return proj(xp16, xs16, w16_all, widx, len(flags) * tn, jnp.asarray(flags, jnp.int32), tab_p, tab_s, tm, tn)


def kernel(x_prompt, x_sample, cache_swa_k, cache_swa_v, cache_nsa_kc, cache_nsa_vc, cache_nsa_ks,
           cache_nsa_vs, cache_nsa_kw, cache_nsa_vw, cache_sb_k, cache_sb_v, page_table, ln_g, ln_b,
           ffn_w_in, ffn_w_out, swa_w_in, swa_sink, swa_w_o, nsa_w_in, nsa_cmp_a, nsa_cmp_pe,
           nsa_cmp_w1, nsa_cmp_w2, nsa_w_o, sb_w_in, sb_w_o):
    bsz, t, d = x_prompt.shape
    db, tn_new, _ = x_sample.shape
    assert tn_new == 1
    depth = ffn_w_in.shape[0]
    n_pages = page_table.shape[1]
    page = cache_nsa_kc.shape[2]
    past_len = n_pages * page
    hq = N_HEADS * HEAD_DIM

    tab_p = _rope_tables(jnp.arange(t, dtype=jnp.int32))
    tab_s = _rope_tables(jnp.full((db,), past_len, jnp.int32))

    xp32 = x_prompt.reshape(bsz * t, d)
    xs32 = x_sample.reshape(db, d)
    xp16, xs16 = xp32.astype(BF16), xs32.astype(BF16)

    ffn_w_out16 = ffn_w_out.astype(BF16)
    swa_w_in16, swa_w_o16 = swa_w_in.astype(BF16), swa_w_o.astype(BF16)
    nsa_w_in16, nsa_w_o16 = nsa_w_in.astype(BF16), nsa_w_o.astype(BF16)
    sb_w_in16, sb_w_o16 = sb_w_in.astype(BF16), sb_w_o.astype(BF16)

    outs = {k: [] for k in ("swa_kp", "swa_vp", "swa_ks", "swa_vs", "nkc_p", "nvc_p", "nks_p", "nvs_p",
                            "nkw_p", "nvw_p", "nkc_s", "nvc_s", "nks_s", "nvs_s", "nkw_s", "nvw_s",
                            "sbk_p", "sbv_p", "sbk_s", "sbv_s")}

    for l in range(depth):
        kind, j = l % 3, l // 3
        (xp32, xp16), (xs32, xs16) = _ffn_step((xp32, xp16), (xs32, xs16), ffn_w_in, ffn_w_out16, (l, 0),
                                               ln_g[l, 0], ln_b[l, 0])

        if kind == 0:
            kd = SWA_KV * HEAD_DIM
            flags = [1] * ((hq + kd) // 512) + [0] * (kd // 512)
            yp, ys = _project(xp16, xs16, swa_w_in16, (j,), flags, tab_p, tab_s, 512)
            op = window_prompt_attn(yp, bsz, t, SWA_KV, 0, N_HEADS, N_HEADS + SWA_KV, SWA_WINDOW,
                                    swa_sink[j], BF16)
            keep = min(SWA_WINDOW, t)
            yp3 = yp.reshape(bsz, t, -1)
            outs["swa_kp"].append(yp3[:, t - keep:, hq:hq + kd].reshape(bsz, keep, SWA_KV, HEAD_DIM))
            outs["swa_vp"].append(yp3[:, t - keep:, hq + kd:].reshape(bsz, keep, SWA_KV, HEAD_DIM))
            k_new, v_new = ys[:, hq:hq + kd], ys[:, hq + kd:]
            k_buf, v_buf = cache_swa_k[j], cache_swa_v[j]
            osm = window_sample_attn(ys[:, :hq], k_new, v_new, k_buf, v_buf, past_len, SWA_WINDOW,
                                     swa_sink[j], BF16)
            wb = k_buf.shape[1]
            keep_s = min(SWA_WINDOW, past_len + 1)
            new4 = lambda x: x.reshape(db, 1, SWA_KV, HEAD_DIM)
            outs["swa_ks"].append(jnp.concatenate([k_buf, new4(k_new)], axis=1)[:, wb + 1 - keep_s:])
            outs["swa_vs"].append(jnp.concatenate([v_buf, new4(v_new)], axis=1)[:, wb + 1 - keep_s:])
            w_o16 = swa_w_o16
        elif kind == 1:
            kd = NSA_KV * HEAD_DIM
            main = hq + 6 * kd
            w_gate = jnp.pad(nsa_w_in16[j][:, main:], ((0, 0), (0, LANES - 3 * N_HEADS)))
            flags = [1] * (hq // 512) + [1, 0, 1, 0, 1, 0]
            a_k, a_v = nsa_cmp_a[j, 0], nsa_cmp_a[j, 1]
            mlp_k = (a_k, nsa_cmp_pe[j, 0], nsa_cmp_w1[j, 0], nsa_cmp_w2[j, 0])
            mlp_v = (a_v, nsa_cmp_pe[j, 1], nsa_cmp_w1[j, 1], nsa_cmp_w2[j, 1])
            yp, ys = _project(xp16, xs16, nsa_w_in16, (j,), flags, tab_p, tab_s, 512)
            gp, gs = _project(xp16, xs16, w_gate, (), [0], tab_p, tab_s, LANES)
            pp = _pick(t // 128, (8, 4, 2, 1))
            yp3 = yp.reshape(bsz, t, -1)
            seg = lambda c: yp3[:, :, hq + c * kd:hq + (c + 1) * kd].reshape(bsz, t, NSA_KV, HEAD_DIM)
            kc_p, vc_p = seg(0), seg(1)
            kcmp = cmp_mlp(*pool_prompt(kc_p, a_k, pp), *mlp_k)
            vcmp = cmp_mlp(*pool_prompt(vc_p, a_v, pp), *mlp_v)
            oc, osl = nsa_prompt_global(yp, kcmp, vcmp, bsz, t)
            ow = window_prompt_attn(yp, bsz, t, NSA_KV, 0, N_HEADS + 4 * NSA_KV, N_HEADS + 5 * NSA_KV,
                                    NSA_WINDOW, None, F32)
            op = nsa_combine(gp, oc, osl, ow, _pick(bsz * t, (512, 256, 128)))
            keep = min(NSA_WINDOW, t)
            outs["nkc_p"].append(kc_p)
            outs["nvc_p"].append(vc_p)
            outs["nks_p"].append(seg(2))
            outs["nvs_p"].append(seg(3))
            outs["nkw_p"].append(seg(4)[:, t - keep:])
            outs["nvw_p"].append(seg(5)[:, t - keep:])
            sseg = lambda c: ys[:, hq + c * kd:hq + (c + 1) * kd]
            assert page == 128
            total = past_len + 1
            n_seg = total // NSA_CMP_STRIDE
            assert n_seg == n_pages * SEG_PER_PAGE
            n_cmp = n_seg - NSA_CMP_LEN // NSA_CMP_STRIDE + 1
            n_sel = -(-total // NSA_SEL_LEN)
            pps = _pick(n_pages, (32, 16, 8, 4, 2, 1))
            kcmp_s = cmp_mlp(*pool_paged(cache_nsa_kc[j], page_table, a_k, pps), *mlp_k)
            vcmp_s = cmp_mlp(*pool_paged(cache_nsa_vc[j], page_table, a_v, pps), *mlp_v)
            oc_s, top_idx, top_val = nsa_sample_cmp(ys[:, :hq], kcmp_s, vcmp_s, n_cmp, n_sel, past_len)
            osl_s = nsa_sample_slc(ys[:, :hq], top_idx, top_val, cache_nsa_ks[j], cache_nsa_vs[j],
                                   page_table, sseg(2), sseg(3), n_sel, past_len)
            kw_buf, vw_buf = cache_nsa_kw[j], cache_nsa_vw[j]
            ow_s = window_sample_attn(ys[:, :hq], sseg(4), sseg(5), kw_buf, vw_buf, past_len, NSA_WINDOW,
                                      None, F32)
            osm = nsa_combine(gs, oc_s.reshape(db, hq), osl_s.reshape(db, hq), ow_s, db)
            wb = kw_buf.shape[1]
            keep_s = min(NSA_WINDOW, past_len + 1)
            new4 = lambda c: sseg(c).reshape(db, 1, NSA_KV, HEAD_DIM)
            for name, c in (("nkc_s", 0), ("nvc_s", 1), ("nks_s", 2), ("nvs_s", 3)):
                outs[name].append(new4(c))
            outs["nkw_s"].append(jnp.concatenate([kw_buf, new4(4)], axis=1)[:, wb + 1 - keep_s:])
            outs["nvw_s"].append(jnp.concatenate([vw_buf, new4(5)], axis=1)[:, wb + 1 - keep_s:])
            w_o16 = nsa_w_o16
        else:
            kd = SB_KV * HEAD_DIM
            flags = [0] * (sb_w_in.shape[-1] // 512)
            yp, ys = _project(xp16, xs16, sb_w_in16, (j,), flags, tab_p, tab_s, 512)
            op = sb_prompt_attn(yp, bsz, t, BF16)
            yp3 = yp.reshape(bsz, t, -1)
            outs["sbk_p"].append(yp3[:, :, hq:hq + kd].reshape(bsz, t, SB_KV, HEAD_DIM))
            outs["sbv_p"].append(yp3[:, :, hq + kd:].reshape(bsz, t, SB_KV, HEAD_DIM))
            osm = sb_sample_attn(ys[:, :hq], cache_sb_k[j], cache_sb_v[j], page_table,
                                 _pick(n_pages, (8, 4, 2, 1)), BF16)
            outs["sbk_s"].append(ys[:, hq:hq + kd].reshape(db, 1, SB_KV, HEAD_DIM))
            outs["sbv_s"].append(ys[:, hq + kd:].reshape(db, 1, SB_KV, HEAD_DIM))
            w_o16 = sb_w_o16

        (xp32, xp16), (xs32, xs16) = _res_ln_rows(op, osm, w_o16, (j,), xp32, xs32, ln_g[l, 1], ln_b[l, 1], 1.0)
        (xp32, xp16), (xs32, xs16) = _ffn_step((xp32, xp16), (xs32, xs16), ffn_w_in, ffn_w_out16, (l, 1),
                                               ln_g[l, 2], ln_b[l, 2])

    st = lambda name: jnp.stack(outs[name])
    return (xp32.reshape(bsz, t, d), xs32.reshape(db, 1, d),
            st("swa_kp"), st("swa_vp"),
            st("nkc_p"), st("nvc_p"), st("nks_p"), st("nvs_p"), st("nkw_p"), st("nvw_p"),
            st("sbk_p"), st("sbv_p"),
            st("swa_ks"), st("swa_vs"),
            st("nkc_s"), st("nvc_s"), st("nks_s"), st("nvs_s"), st("nkw_s"), st("nvw_s"),
            st("sbk_s"), st("sbv_s"))
```
